```python
import math
import jax, jax.numpy as jnp
from jax import lax
import numpy as np

D_MODEL = 2048
BATCH = 4
SEQ = 4096
DEPTH = 4

HEAD_DIM = 128
ROPE_THETA = 10000.0
NORM_EPS = 1e-6

A_Q_HEADS = 8
A_KV_HEADS = 2
A_GROUP = A_Q_HEADS // A_KV_HEADS
A_RADIUS = 128
A_BLOCK = 128

B_PATTERNS = ((128, 1), (512, 4), (2048, 16))
B_GROUPS = len(B_PATTERNS)
B_HEADS_PER_GROUP = 4
B_HEADS = B_GROUPS * B_HEADS_PER_GROUP
B_BLOCK = 64

C_HEADS = 8
GRID_W = 64
C_WIN_ROWS = 8
C_WIN_COLS = 16

N_BRANCH = 3
A_Q_W = A_Q_HEADS * HEAD_DIM
A_KV_W = A_KV_HEADS * HEAD_DIM
B_W = B_HEADS * HEAD_DIM
B_OUT_W = B_HEADS_PER_GROUP * HEAD_DIM
C_W = C_HEADS * HEAD_DIM
IN_WIDTHS = (A_Q_W, A_KV_W, A_KV_W, B_W, B_W, B_W, C_W, C_W, C_W, N_BRANCH * D_MODEL)
N_IN = sum(IN_WIDTHS)

D_FF = ((8 * D_MODEL + 3 * 256 - 1) // (3 * 256)) * 256

kernel_name = 'hybrid_gated_local_dilated_grid_attention_encoder'


def rmsnorm(x, g):
    x32 = x.astype(jnp.float32)
    y = x32 * lax.rsqrt(jnp.mean(x32 * x32, axis=-1, keepdims=True) + NORM_EPS)
    return (y * g.astype(jnp.float32)).astype(x.dtype)


def rope_tables(n):
    half = HEAD_DIM // 2
    inv_freq = ROPE_THETA ** (-jnp.arange(half, dtype=jnp.float32) * 2.0 / HEAD_DIM)
    ang = jnp.arange(n, dtype=jnp.float32)[:, None] * inv_freq[None, :]
    return jnp.cos(ang), jnp.sin(ang)


def apply_rope(x, cos, sin):
    half = HEAD_DIM // 2
    x32 = x.astype(jnp.float32)
    x1, x2 = x32[..., :half], x32[..., half:]
    return jnp.concatenate([x1 * cos - x2 * sin, x2 * cos + x1 * sin], axis=-1).astype(x.dtype)


def banded_attention(q, k, v, radius, block, sink=None):
    bsz, hk, grp, n, dh = q.shape
    nb = n // block
    width = block + 2 * radius
    pad = ((0, 0), (0, 0), (radius, radius), (0, 0))
    idx = jnp.arange(nb)[:, None] * block + jnp.arange(width)[None, :]
    kb = jnp.pad(k, pad)[:, :, idx]
    vb = jnp.pad(v, pad)[:, :, idx]
    qb = q.reshape(bsz, hk, grp, nb, block, dh)
    s = jnp.einsum('bhgnqd,bhnkd->bhgnqk', qb, kb, preferred_element_type=jnp.float32) * (dh ** -0.5)
    qpos = (jnp.arange(nb)[:, None] * block + jnp.arange(block)[None, :])[:, :, None]
    kpos = (idx - radius)[:, None, :]
    valid = (jnp.abs(kpos - qpos) <= radius) & (kpos >= 0) & (kpos < n)
    s = jnp.where(valid, s, -jnp.inf)
    m = jnp.max(s, axis=-1, keepdims=True)
    if sink is not None:
        sk = sink.astype(jnp.float32).reshape(1, hk, grp, 1, 1, 1)
        m = jnp.maximum(m, sk)
    p = jnp.exp(s - m)
    denom = jnp.sum(p, axis=-1, keepdims=True)
    if sink is not None:
        denom = denom + jnp.exp(sk - m)
    o = jnp.einsum('bhgnqk,bhnkd->bhgnqd', (p / denom).astype(v.dtype), vb)
    lse = (m + jnp.log(denom))[..., 0]
    return o.reshape(bsz, hk, grp, n, dh), lse.reshape(bsz, hk, grp, n)


def mixer_a(qa, ka, va, cos, sin, gq, gk, sink):
    bsz, n, _ = qa.shape
    q = qa.reshape(bsz, n, A_KV_HEADS, A_GROUP, HEAD_DIM).transpose(0, 2, 3, 1, 4)
    k = ka.reshape(bsz, n, A_KV_HEADS, HEAD_DIM).transpose(0, 2, 1, 3)
    v = va.reshape(bsz, n, A_KV_HEADS, HEAD_DIM).transpose(0, 2, 1, 3)
    q = apply_rope(rmsnorm(q, gq), cos, sin)
    k = apply_rope(rmsnorm(k, gk), cos, sin)
    o, _ = banded_attention(q, k, v, A_RADIUS, math.gcd(n, A_BLOCK),
                            sink.reshape(A_KV_HEADS, A_GROUP))
    return o.transpose(0, 3, 1, 2, 4).reshape(bsz, n, A_Q_W)


def to_residue_classes(t, dil):
    bsz, h, n, dh = t.shape
    return t.reshape(bsz, h, n // dil, dil, dh).transpose(0, 1, 3, 2, 4).reshape(bsz, h * dil, n // dil, dh)


def mixer_b(qb, kb, vb, cos, sin, gq, gk):
    bsz, n, _ = qb.shape
    hg = B_HEADS_PER_GROUP

    def heads(t):
        return t.reshape(bsz, n, B_GROUPS, hg, HEAD_DIM).transpose(0, 2, 3, 1, 4)

    q = apply_rope(rmsnorm(heads(qb), gq), cos, sin)
    k = apply_rope(rmsnorm(heads(kb), gk), cos, sin)
    v = heads(vb)
    outs, lses = [], []
    for g, (window, dil) in enumerate(B_PATTERNS):
        m = n // dil
        o, lse = banded_attention(to_residue_classes(q[:, g], dil)[:, :, None],
                                  to_residue_classes(k[:, g], dil),
                                  to_residue_classes(v[:, g], dil),
                                  window // (2 * dil), math.gcd(m, B_BLOCK))
        o = o[:, :, 0].reshape(bsz, hg, dil, m, HEAD_DIM).transpose(0, 1, 3, 2, 4).reshape(bsz, hg, n, HEAD_DIM)
        lse = lse[:, :, 0].reshape(bsz, hg, dil, m).transpose(0, 1, 3, 2).reshape(bsz, hg, n)
        outs.append(o)
        lses.append(lse)
    o = jnp.stack(outs, axis=1)
    lse = jnp.stack(lses, axis=1)
    w = jax.nn.softmax(lse, axis=1)
    out = jnp.einsum('bghl,bghld->bhld', w.astype(o.dtype), o)
    return out.transpose(0, 2, 1, 3).reshape(bsz, n, B_OUT_W)


def mixer_c(qc, kc, vc, gq, gk, rpb):
    bsz, n, _ = qc.shape
    rows = n // GRID_W
    wr = min(C_WIN_ROWS, rows)

    def grid(t):
        return t.reshape(bsz, rows, GRID_W, C_HEADS, HEAD_DIM).transpose(0, 3, 1, 2, 4)

    q = rmsnorm(grid(qc), gq)
    k = rmsnorm(grid(kc), gk)
    v = grid(vc)
    r = jnp.arange(rows)
    row_start = jnp.clip(r - wr // 2, 0, rows - wr)
    krow = row_start[:, None] + jnp.arange(wr)[None, :]
    kg = k[:, :, krow]
    vg = v[:, :, krow]
    s = jnp.einsum('bhrcd,bhrwkd->bhrcwk', q, kg, preferred_element_type=jnp.float32) * (HEAD_DIM ** -0.5)
    cq = jnp.arange(GRID_W)
    col_start = jnp.clip(cq - C_WIN_COLS // 2, 0, GRID_W - C_WIN_COLS)
    col_ok = (cq[None, :] >= col_start[:, None]) & (cq[None, :] < col_start[:, None] + C_WIN_COLS)
    drow = krow - r[:, None]
    dcol = jnp.clip(cq[None, :] - cq[:, None], -(C_WIN_COLS - 1), C_WIN_COLS - 1)
    bias = rpb[:, drow[:, None, :, None] + (C_WIN_ROWS - 1), dcol[None, :, None, :] + (C_WIN_COLS - 1)]
    s = s + bias[None].astype(jnp.float32)
    s = jnp.where(col_ok[:, None, :], s, -jnp.inf)
    p = jax.nn.softmax(s.reshape(bsz, C_HEADS, rows, GRID_W, wr * GRID_W), axis=-1).reshape(s.shape)
    o = jnp.einsum('bhrcwk,bhrwkd->bhrcd', p.astype(v.dtype), vg)
    return o.transpose(0, 2, 3, 1, 4).reshape(bsz, n, C_W)


def setup_inputs(seed: int = 0) -> dict:
    key = jax.random.key(seed)
    ks = jax.random.split(key, 14)
    f32 = jnp.float32

    def w(k, shape, fan_in):
        return jax.random.normal(k, shape, f32) * (fan_in ** -0.5)

    return {
        'x': jax.random.normal(ks[0], (BATCH, SEQ, D_MODEL), f32),
        'norm1_g': 1.0 + 0.02 * jax.random.normal(ks[1], (DEPTH, D_MODEL), f32),
        'w_in': w(ks[2], (DEPTH, D_MODEL, N_IN), D_MODEL),
        'qk_norm_g': 1.0 + 0.02 * jax.random.normal(ks[3], (DEPTH, 6, HEAD_DIM), f32),
        'sink_a': jax.random.normal(ks[4], (DEPTH, A_Q_HEADS), f32),
        'rpb_c': 0.1 * jax.random.normal(ks[5], (DEPTH, C_HEADS, 2 * C_WIN_ROWS - 1, 2 * C_WIN_COLS - 1), f32),
        'w_br_a': w(ks[6], (DEPTH, A_Q_W, D_MODEL), A_Q_W),
        'w_br_b': w(ks[7], (DEPTH, B_OUT_W, D_MODEL), B_OUT_W),
        'w_br_c': w(ks[8], (DEPTH, C_W, D_MODEL), C_W),
        'w_o': w(ks[9], (DEPTH, D_MODEL, D_MODEL), D_MODEL),
        'norm2_g': 1.0 + 0.02 * jax.random.normal(ks[10], (DEPTH, D_MODEL), f32),
        'w_gate_up': w(ks[11], (DEPTH, D_MODEL, 2 * D_FF), D_MODEL),
        'w_down': w(ks[12], (DEPTH, D_FF, D_MODEL), D_FF),
    }


def reference(x, norm1_g, w_in, qk_norm_g, sink_a, rpb_c, w_br_a, w_br_b, w_br_c, w_o,
              norm2_g, w_gate_up, w_down):
    bsz, n, _ = x.shape
    cos, sin = rope_tables(n)
    split_points = []
    acc = 0
    for wdt in IN_WIDTHS[:-1]:
        acc += wdt
        split_points.append(acc)
    for i in range(DEPTH):
        h = rmsnorm(x, norm1_g[i])
        proj = h @ w_in[i]
        qa, ka, va, qb, kb, vb, qc, kc, vc, gl = jnp.split(proj, split_points, axis=-1)
        g = qk_norm_g[i]
        oa = mixer_a(qa, ka, va, cos, sin, g[0], g[1], sink_a[i])
        ob = mixer_b(qb, kb, vb, cos, sin, g[2], g[3])
        oc = mixer_c(qc, kc, vc, g[4], g[5], rpb_c[i])
        gates = jax.nn.sigmoid(gl.astype(jnp.float32)).astype(x.dtype).reshape(bsz, n, N_BRANCH, D_MODEL)
        merged = (gates[:, :, 0] * (oa @ w_br_a[i])
                  + gates[:, :, 1] * (ob @ w_br_b[i])
                  + gates[:, :, 2] * (oc @ w_br_c[i]))
        x = x + merged @ w_o[i]
        h2 = rmsnorm(x, norm2_g[i])
        gt, up = jnp.split(h2 @ w_gate_up[i], 2, axis=-1)
        x = x + (jax.nn.silu(gt) * up) @ w_down[i]
    return x
```

```python
import functools
import math

import numpy as np
import jax
import jax.numpy as jnp
from jax import lax
from jax.experimental import pallas as pl
from jax.experimental.pallas import tpu as pltpu

F32 = jnp.float32
BF16 = jnp.bfloat16

D_MODEL = 2048
DEPTH = 4
HEAD_DIM = 128
ROPE_THETA = 10000.0
NORM_EPS = 1e-6
SCALE = HEAD_DIM ** -0.5
NEG = -1e30

A_Q_HEADS = 8
A_KV_HEADS = 2
A_GROUP = A_Q_HEADS // A_KV_HEADS
A_RADIUS = 128
B_PATTERNS = ((128, 1), (512, 4), (2048, 16))
B_HEADS_PER_GROUP = 4
B_RADIUS = 64
C_HEADS = 8
GRID_W = 64
C_WIN_ROWS = 8
C_WIN_COLS = 16

A_Q_W = A_Q_HEADS * HEAD_DIM
A_KV_W = A_KV_HEADS * HEAD_DIM
B_W = len(B_PATTERNS) * B_HEADS_PER_GROUP * HEAD_DIM
B_OUT_W = B_HEADS_PER_GROUP * HEAD_DIM
C_W = C_HEADS * HEAD_DIM
N_IN = A_Q_W + 2 * A_KV_W + 3 * B_W + 3 * C_W + 3 * D_MODEL
D_FF = ((8 * D_MODEL + 3 * 256 - 1) // (3 * 256)) * 256

OFF_QA = 0
OFF_KA = OFF_QA + A_Q_W
OFF_VA = OFF_KA + A_KV_W
OFF_QB = OFF_VA + A_KV_W
OFF_KB = OFF_QB + B_W
OFF_VB = OFF_KB + B_W
OFF_QC = OFF_VB + B_W
OFF_KC = OFF_QC + C_W
OFF_VC = OFF_KC + C_W
OFF_GATE = OFF_VC + C_W

TM = 1024
TN = 512
TN_DOWN = 256
ROW_CHUNK = 256
NORM_CHUNK = 64
ATT_Q = 256
C_TILE_ROWS = ATT_Q // GRID_W
MIB = 1024 * 1024


def _params(n_axes, vmem_mib):
    return pltpu.CompilerParams(
        dimension_semantics=("parallel",) * n_axes,
        vmem_limit_bytes=vmem_mib * MIB,
    )


def _rmsnorm_rows(x_ref, g_ref, h_ref):
    rows = x_ref.shape[0]

    def body(c, carry):
        r0 = pl.multiple_of(c * NORM_CHUNK, NORM_CHUNK)
        x = x_ref[pl.ds(r0, NORM_CHUNK), :]
        ms = jnp.mean(x * x, axis=-1, keepdims=True)
        h_ref[pl.ds(r0, NORM_CHUNK), :] = (x * lax.rsqrt(ms + NORM_EPS) * g_ref[...]).astype(h_ref.dtype)
        return carry

    lax.fori_loop(0, rows // NORM_CHUNK, body, 0)


def _softmax_pv(s, vcat, extra_logit=None):
    m = jnp.max(s, axis=-1, keepdims=True)
    if extra_logit is not None:
        m = jnp.maximum(m, extra_logit)
    p = jnp.exp(s - m)
    l = jnp.sum(p, axis=-1, keepdims=True)
    if extra_logit is not None:
        l = l + jnp.exp(extra_logit - m)
    o = jnp.dot(p.astype(BF16), vcat, preferred_element_type=F32) * (1.0 / l)
    return o, m, l


def _qk(q, k):
    return lax.dot_general(q, k, (((1,), (1,)), ((), ())), preferred_element_type=F32)


def _head_kind(head):
    c = head * HEAD_DIM
    if c < OFF_KA:
        return ("norm_rope", 0, True)
    if c < OFF_VA:
        return ("norm_rope", 1, False)
    if c < OFF_QB:
        return ("copy",)
    if c < OFF_KB:
        return ("norm_rope", 2, True)
    if c < OFF_VB:
        return ("norm_rope", 3, False)
    if c < OFF_QC:
        return ("copy",)
    if c < OFF_KC:
        return ("norm", 4, True)
    if c < OFF_VC:
        return ("norm", 5, False)
    if c < OFF_GATE:
        return ("copy",)
    return ("sigmoid",)


def _tile_groups():
    heads_per_tile = TN // HEAD_DIM
    groups = {}
    for j in range(N_IN // TN):
        kinds = tuple(_head_kind(j * heads_per_tile + h) for h in range(heads_per_tile))
        groups.setdefault(kinds, []).append(j)
    return list(groups.items())


def _in_tiles(j, tiles):
    runs = []
    for t in tiles:
        if runs and runs[-1][1] == t:
            runs[-1][1] = t + 1
        else:
            runs.append([t, t + 1])
    cond = None
    for lo, hi in runs:
        c = (j == lo) if hi == lo + 1 else ((j >= lo) & (j < hi))
        cond = c if cond is None else (cond | c)
    return cond


def _proj_epilogue(kinds, acc_ref, o_ref, cos_ref, sin_ref, qkg_ref):
    def body(c, carry):
        r0 = pl.multiple_of(c * ROW_CHUNK, ROW_CHUNK)
        rows = pl.ds(r0, ROW_CHUNK)
        for h, kind in enumerate(kinds):
            cols = slice(h * HEAD_DIM, (h + 1) * HEAD_DIM)
            a = acc_ref[rows, cols]
            if kind[0] in ("norm_rope", "norm"):
                g = qkg_ref[kind[1]:kind[1] + 1, :]
                if kind[2]:
                    g = g * SCALE
                ms = jnp.mean(a * a, axis=-1, keepdims=True)
                y = a * lax.rsqrt(ms + NORM_EPS) * g
                if kind[0] == "norm_rope":
                    y = y * cos_ref[rows, :] + pltpu.roll(y, HEAD_DIM // 2, 1) * sin_ref[rows, :]
            elif kind[0] == "sigmoid":
                y = 1.0 / (1.0 + jnp.exp(-a))
            else:
                y = a
            o_ref[rows, cols] = y.astype(o_ref.dtype)
        return carry

    lax.fori_loop(0, TM // ROW_CHUNK, body, 0)


def _in_proj_body(x_ref, g_ref, w_ref, cos_ref, sin_ref, qkg_ref, o_ref, h_ref, acc_ref):
    j = pl.program_id(1)

    @pl.when(j == 0)
    def _():
        _rmsnorm_rows(x_ref, g_ref, h_ref)

    acc_ref[...] = jnp.dot(h_ref[...], w_ref[...], preferred_element_type=F32)
    for kinds, tiles in _tile_groups():
        pl.when(_in_tiles(j, tiles))(
            functools.partial(_proj_epilogue, kinds, acc_ref, o_ref, cos_ref, sin_ref, qkg_ref))


def _in_proj(x, g, w_all, layer, cos, sin_signed, qkg, seq):
    t = x.shape[0]
    pos_tiles = seq // TM
    return pl.pallas_call(
        _in_proj_body,
        out_shape=jax.ShapeDtypeStruct((t, N_IN), BF16),
        grid=(t // TM, N_IN // TN),
        in_specs=[
            pl.BlockSpec((TM, D_MODEL), lambda i, j: (i, 0)),
            pl.BlockSpec((1, D_MODEL), lambda i, j: (0, 0)),
            pl.BlockSpec((None, D_MODEL, TN), lambda i, j: (layer, 0, j)),
            pl.BlockSpec((TM, HEAD_DIM), lambda i, j: (i % pos_tiles, 0)),
            pl.BlockSpec((TM, HEAD_DIM), lambda i, j: (i % pos_tiles, 0)),
            pl.BlockSpec((8, HEAD_DIM), lambda i, j: (0, 0)),
        ],
        out_specs=pl.BlockSpec((TM, TN), lambda i, j: (i, j)),
        scratch_shapes=[pltpu.VMEM((TM, D_MODEL), BF16), pltpu.VMEM((TM, TN), F32)],
        compiler_params=_params(2, 48),
        name="in_proj",
    )(x, g, w_all, cos, sin_signed, qkg)


def _band_bias(radius):
    r = np.arange(ATT_Q)[:, None]
    c = np.arange(ATT_Q + 2 * radius)[None, :]
    band = np.abs(c - radius - r) <= radius
    lo_ok = c >= radius
    hi_ok = c < ATT_Q + radius
    variants = [band & lo_ok, band, band & hi_ok, band & lo_ok & hi_ok]
    return np.stack([np.where(v, 0.0, NEG) for v in variants]).astype(np.float32)


def _band_variant(i, n_tiles):
    if n_tiles == 1:
        return 3
    return jnp.where(i == 0, 0, jnp.where(i == n_tiles - 1, 2, 1))


def _halo_specs(col_block, width, radius, n_rows, lead):
    per = ATT_Q // radius
    last = n_rows // radius - 1
    return [
        pl.BlockSpec((None, radius, width),
                     lambda *g: (lead(g)[0], jnp.maximum(lead(g)[1] * per - 1, 0), col_block(g))),
        pl.BlockSpec((None, ATT_Q, width),
                     lambda *g: (lead(g)[0], lead(g)[1], col_block(g))),
        pl.BlockSpec((None, radius, width),
                     lambda *g: (lead(g)[0], jnp.minimum(lead(g)[1] * per + per, last), col_block(g))),
    ]


def _attn_a_body(sink_ref, q_ref, kp_ref, kc_ref, kn_ref, vp_ref, vc_ref, vn_ref, bias_ref, o_ref):
    hk = pl.program_id(1)
    kcat = jnp.concatenate([kp_ref[...], kc_ref[...], kn_ref[...]], axis=0)
    vcat = jnp.concatenate([vp_ref[...], vc_ref[...], vn_ref[...]], axis=0)
    q = q_ref[...]
    qs = jnp.concatenate([q[:, g * HEAD_DIM:(g + 1) * HEAD_DIM] for g in range(A_GROUP)], axis=0)
    s = _qk(qs, kcat)
    bias = bias_ref[...]
    ps, inv = [], []
    for g in range(A_GROUP):
        sg = s[g * ATT_Q:(g + 1) * ATT_Q, :] + bias
        sink = sink_ref[hk * A_GROUP + g]
        m = jnp.maximum(jnp.max(sg, axis=-1, keepdims=True), sink)
        p = jnp.exp(sg - m)
        l = jnp.sum(p, axis=-1, keepdims=True) + jnp.exp(sink - m)
        ps.append(p.astype(BF16))
        inv.append(1.0 / l)
    o = jnp.dot(jnp.concatenate(ps, axis=0), vcat, preferred_element_type=F32)
    outs = [(o[g * ATT_Q:(g + 1) * ATT_Q, :] * inv[g]).astype(o_ref.dtype) for g in range(A_GROUP)]
    o_ref[...] = jnp.concatenate(outs, axis=1)


def _attn_a(proj3, sink, bias):
    bsz, seq, _ = proj3.shape
    n_tiles = seq // ATT_Q
    qw = A_GROUP * HEAD_DIM
    lead = lambda g: (g[0], g[2])
    k_specs = _halo_specs(lambda g: OFF_KA // HEAD_DIM + g[1], HEAD_DIM, A_RADIUS, seq, lead)
    v_specs = _halo_specs(lambda g: OFF_VA // HEAD_DIM + g[1], HEAD_DIM, A_RADIUS, seq, lead)
    return pl.pallas_call(
        _attn_a_body,
        out_shape=jax.ShapeDtypeStruct((bsz, seq, A_Q_W), BF16),
        grid=(bsz, A_KV_HEADS, n_tiles),
        in_specs=[
            pl.BlockSpec(memory_space=pltpu.SMEM),
            pl.BlockSpec((None, ATT_Q, qw), lambda b, h, i: (b, i, OFF_QA // qw + h)),
            *k_specs, *v_specs,
            pl.BlockSpec((None, ATT_Q, ATT_Q + 2 * A_RADIUS), lambda b, h, i: (_band_variant(i, n_tiles), 0, 0)),
        ],
        out_specs=pl.BlockSpec((None, ATT_Q, qw), lambda b, h, i: (b, i, h)),
        compiler_params=_params(3, 32),
        name="attn_a",
    )(sink, proj3, proj3, proj3, proj3, proj3, proj3, proj3, bias)


def _attn_b_body(*refs, has_prev, emit_lse):
    q_ref, kp_ref, kc_ref, kn_ref, vp_ref, vc_ref, vn_ref, bias_ref = refs[:8]
    rest = list(refs[8:])
    if has_prev:
        prev_o_ref, prev_lse_ref = rest[:2]
        rest = rest[2:]
    o_ref = rest[0]
    lse_ref = rest[1] if emit_lse else None
    bias = bias_ref[...]
    for h in range(B_HEADS_PER_GROUP):
        cols = slice(h * HEAD_DIM, (h + 1) * HEAD_DIM)
        kcat = jnp.concatenate([kp_ref[:, cols], kc_ref[:, cols], kn_ref[:, cols]], axis=0)
        vcat = jnp.concatenate([vp_ref[:, cols], vc_ref[:, cols], vn_ref[:, cols]], axis=0)
        s = _qk(q_ref[:, cols], kcat) + bias
        o, m, l = _softmax_pv(s, vcat)
        lse = jnp.broadcast_to(m + jnp.log(l), o.shape)
        if has_prev:
            lp = prev_lse_ref[:, cols]
            mx = jnp.maximum(lp, lse)
            wp = jnp.exp(lp - mx)
            wn = jnp.exp(lse - mx)
            tot = wp + wn
            o = (wp * prev_o_ref[:, cols].astype(F32) + wn * o) * (1.0 / tot)
            lse = mx + jnp.log(tot)
        o_ref[:, cols] = o.astype(o_ref.dtype)
        if emit_lse:
            lse_ref[:, cols] = lse


def _attn_b_group(proj, group, dil, bias, prev, emit_lse):
    bsz, seq, _ = proj.shape
    m = seq // dil
    n_tiles = m // ATT_Q
    pview = proj.reshape(bsz, m, dil * N_IN)
    per_class = N_IN // B_OUT_W
    lead = lambda g: (g[0], g[2])
    blk = lambda off: (lambda g: g[1] * per_class + off // B_OUT_W + group)
    q_spec = pl.BlockSpec((None, ATT_Q, B_OUT_W), lambda b, c, i: (b, i, blk(OFF_QB)((b, c, i))))
    k_specs = _halo_specs(blk(OFF_KB), B_OUT_W, B_RADIUS, m, lead)
    v_specs = _halo_specs(blk(OFF_VB), B_OUT_W, B_RADIUS, m, lead)
    io_spec = pl.BlockSpec((None, ATT_Q, B_OUT_W), lambda b, c, i: (b, i, c))
    in_specs = [q_spec, *k_specs, *v_specs,
                pl.BlockSpec((None, ATT_Q, ATT_Q + 2 * B_RADIUS), lambda b, c, i: (_band_variant(i, n_tiles), 0, 0))]
    args = [pview] * 7 + [bias]
    if prev is not None:
        in_specs += [io_spec, io_spec]
        args += [prev[0].reshape(bsz, m, dil * B_OUT_W), prev[1].reshape(bsz, m, dil * B_OUT_W)]
    out_shape = [jax.ShapeDtypeStruct((bsz, m, dil * B_OUT_W), BF16)]
    out_specs = [io_spec]
    if emit_lse:
        out_shape.append(jax.ShapeDtypeStruct((bsz, m, dil * B_OUT_W), F32))
        out_specs.append(io_spec)
    res = pl.pallas_call(
        functools.partial(_attn_b_body, has_prev=prev is not None, emit_lse=emit_lse),
        out_shape=out_shape,
        grid=(bsz, dil, n_tiles),
        in_specs=in_specs,
        out_specs=out_specs,
        compiler_params=_params(3, 32),
        name=f"attn_b{group}",
    )(*args)
    return [r.reshape(bsz, seq, B_OUT_W) for r in res]


def _attn_b(proj3, bias):
    prev = None
    for group, (_, dil) in enumerate(B_PATTERNS):
        last = group == len(B_PATTERNS) - 1
        prev = _attn_b_group(proj3, group, dil, bias, prev, emit_lse=not last)
    return prev[0]


N_DROW = 2 * C_WIN_ROWS - 1
N_DCOL = 2 * C_WIN_COLS - 1
C_KEY_TILES = 3 * C_TILE_ROWS


def _c_bias_body(rpb_ref, o_ref):
    h = pl.program_id(0)
    qc = lax.broadcasted_iota(jnp.int32, (GRID_W, GRID_W), 0)
    kc = lax.broadcasted_iota(jnp.int32, (GRID_W, GRID_W), 1)
    dcol = jnp.clip(kc - qc, -(C_WIN_COLS - 1), C_WIN_COLS - 1) + (C_WIN_COLS - 1)
    col_start = jnp.clip(qc - C_WIN_COLS // 2, 0, GRID_W - C_WIN_COLS)
    col_ok = (kc >= col_start) & (kc < col_start + C_WIN_COLS)
    tiles = []
    for dr in range(N_DROW):
        t = jnp.zeros((GRID_W, GRID_W), F32)
        for d in range(N_DCOL):
            t = jnp.where(dcol == d, rpb_ref[(h * N_DROW + dr) * N_DCOL + d], t)
        tiles.append(jnp.where(col_ok, t, NEG))
    masked = jnp.full((GRID_W, GRID_W), NEG, F32)
    half = C_WIN_ROWS // 2
    for var in range(3):
        for qi in range(C_TILE_ROWS):
            row = []
            for kj in range(C_KEY_TILES):
                if var == 0:
                    ok = C_TILE_ROWS <= kj < C_TILE_ROWS + C_WIN_ROWS
                elif var == 2:
                    ok = 2 * C_TILE_ROWS - C_WIN_ROWS <= kj < 2 * C_TILE_ROWS
                else:
                    ok = 0 <= kj - C_TILE_ROWS - qi + half < C_WIN_ROWS
                drow = kj - C_TILE_ROWS - qi
                row.append(tiles[drow + C_WIN_ROWS - 1] if ok else masked)
            o_ref[var, qi * GRID_W:(qi + 1) * GRID_W, :] = jnp.concatenate(row, axis=1)


def _c_bias(rpb_flat):
    return pl.pallas_call(
        _c_bias_body,
        out_shape=jax.ShapeDtypeStruct((3, C_HEADS, ATT_Q, C_KEY_TILES * GRID_W), F32),
        grid=(C_HEADS,),
        in_specs=[pl.BlockSpec(memory_space=pltpu.SMEM)],
        out_specs=pl.BlockSpec((3, None, ATT_Q, C_KEY_TILES * GRID_W), lambda h: (0, h, 0, 0)),
        compiler_params=_params(1, 32),
        name="c_bias",
    )(rpb_flat)


def _attn_c_body(q_ref, kp_ref, kc_ref, kn_ref, vp_ref, vc_ref, vn_ref, bias_ref, o_ref):
    for h in range(C_HEADS):
        cols = slice(h * HEAD_DIM, (h + 1) * HEAD_DIM)
        kcat = jnp.concatenate([kp_ref[:, cols], kc_ref[:, cols], kn_ref[:, cols]], axis=0)
        vcat = jnp.concatenate([vp_ref[:, cols], vc_ref[:, cols], vn_ref[:, cols]], axis=0)
        s = _qk(q_ref[:, cols], kcat) + bias_ref[h]
        o, _, _ = _softmax_pv(s, vcat)
        o_ref[:, cols] = o.astype(o_ref.dtype)


def _attn_c(proj3, bias):
    bsz, seq, _ = proj3.shape
    n_tiles = seq // ATT_Q
    lead = lambda g: (g[0], g[1])
    k_specs = _halo_specs(lambda g: OFF_KC // C_W, C_W, ATT_Q, seq, lead)
    v_specs = _halo_specs(lambda g: OFF_VC // C_W, C_W, ATT_Q, seq, lead)
    return pl.pallas_call(
        _attn_c_body,
        out_shape=jax.ShapeDtypeStruct((bsz, seq, C_W), BF16),
        grid=(bsz, n_tiles),
        in_specs=[
            pl.BlockSpec((None, ATT_Q, C_W), lambda b, i: (b, i, OFF_QC // C_W)),
            *k_specs, *v_specs,
            pl.BlockSpec((None, C_HEADS, ATT_Q, C_KEY_TILES * GRID_W),
                         lambda b, i: (jnp.where(i == 0, 0, jnp.where(i == n_tiles - 1, 2, 1)), 0, 0, 0)),
        ],
        out_specs=pl.BlockSpec((None, ATT_Q, C_W), lambda b, i: (b, i, 0)),
        compiler_params=_params(2, 40),
        name="attn_c",
    )(proj3, proj3, proj3, proj3, proj3, proj3, proj3, bias)


def _merge_body(oa_ref, ob_ref, oc_ref, ga_ref, gb_ref, gc_ref, wa_ref, wb_ref, wc_ref, o_ref):
    m = ga_ref[...].astype(F32) * jnp.dot(oa_ref[...], wa_ref[...], preferred_element_type=F32)
    m = m + gb_ref[...].astype(F32) * jnp.dot(ob_ref[...], wb_ref[...], preferred_element_type=F32)
    m = m + gc_ref[...].astype(F32) * jnp.dot(oc_ref[...], wc_ref[...], preferred_element_type=F32)
    o_ref[...] = m.astype(o_ref.dtype)


def _merge(oa, ob, oc, proj, wa, wb, wc, layer):
    t = oa.shape[0]
    gate = lambda k: pl.BlockSpec((TM, TN), lambda i, j: (i, (OFF_GATE + k * D_MODEL) // TN + j))
    wspec = lambda rows: pl.BlockSpec((None, rows, TN), lambda i, j: (layer, 0, j))
    return pl.pallas_call(
        _merge_body,
        out_shape=jax.ShapeDtypeStruct((t, D_MODEL), BF16),
        grid=(t // TM, D_MODEL // TN),
        in_specs=[
            pl.BlockSpec((TM, A_Q_W), lambda i, j: (i, 0)),
            pl.BlockSpec((TM, B_OUT_W), lambda i, j: (i, 0)),
            pl.BlockSpec((TM, C_W), lambda i, j: (i, 0)),
            gate(0), gate(1), gate(2),
            wspec(A_Q_W), wspec(B_OUT_W), wspec(C_W),
        ],
        out_specs=pl.BlockSpec((TM, TN), lambda i, j: (i, j)),
        compiler_params=_params(2, 48),
        name="merge",
    )(oa, ob, oc, proj, proj, proj, wa, wb, wc)


def _residual_matmul_body(a_ref, w_ref, x_ref, o_ref):
    o_ref[...] = x_ref[...] + jnp.dot(a_ref[...], w_ref[...], preferred_element_type=F32)


def _residual_matmul(a, w_all, layer, x, tn, name):
    t, k = a.shape
    return pl.pallas_call(
        _residual_matmul_body,
        out_shape=jax.ShapeDtypeStruct((t, D_MODEL), F32),
        grid=(t // TM, D_MODEL // tn),
        in_specs=[
            pl.BlockSpec((TM, k), lambda i, j: (i, 0)),
            pl.BlockSpec((None, k, tn), lambda i, j: (layer, 0, j)),
            pl.BlockSpec((TM, tn), lambda i, j: (i, j)),
        ],
        out_specs=pl.BlockSpec((TM, tn), lambda i, j: (i, j)),
        compiler_params=_params(2, 48),
        name=name,
    )(a, w_all, x)


def _ffn_up_body(x_ref, g_ref, wg_ref, wu_ref, o_ref, h_ref):
    @pl.when(pl.program_id(1) == 0)
    def _():
        _rmsnorm_rows(x_ref, g_ref, h_ref)

    h = h_ref[...]
    gt = jnp.dot(h, wg_ref[...], preferred_element_type=F32)
    up = jnp.dot(h, wu_ref[...], preferred_element_type=F32)
    o_ref[...] = (gt * (1.0 / (1.0 + jnp.exp(-gt))) * up).astype(o_ref.dtype)


def _ffn_up(x, g, w_all, layer):
    t = x.shape[0]
    n_tiles = D_FF // TN
    return pl.pallas_call(
        _ffn_up_body,
        out_shape=jax.ShapeDtypeStruct((t, D_FF), BF16),
        grid=(t // TM, n_tiles),
        in_specs=[
            pl.BlockSpec((TM, D_MODEL), lambda i, j: (i, 0)),
            pl.BlockSpec((1, D_MODEL), lambda i, j: (0, 0)),
            pl.BlockSpec((None, D_MODEL, TN), lambda i, j: (layer, 0, j)),
            pl.BlockSpec((None, D_MODEL, TN), lambda i, j: (layer, 0, n_tiles + j)),
        ],
        out_specs=pl.BlockSpec((TM, TN), lambda i, j: (i, j)),
        scratch_shapes=[pltpu.VMEM((TM, D_MODEL), BF16)],
        compiler_params=_params(2, 48),
        name="ffn_up",
    )(x, g, w_all, w_all)


def _rope_tables(n):
    half = HEAD_DIM // 2
    inv_freq = ROPE_THETA ** (-jnp.arange(half, dtype=F32) * 2.0 / HEAD_DIM)
    ang = jnp.arange(n, dtype=F32)[:, None] * inv_freq[None, :]
    cos, sin = jnp.cos(ang), jnp.sin(ang)
    return jnp.concatenate([cos, cos], axis=-1), jnp.concatenate([-sin, sin], axis=-1)


def kernel(x, norm1_g, w_in, qk_norm_g, sink_a, rpb_c, w_br_a, w_br_b, w_br_c, w_o,
           norm2_g, w_gate_up, w_down):
    bsz, seq, d = x.shape
    assert d == D_MODEL and seq % TM == 0 and seq // ATT_Q >= 2
    assert seq // GRID_W >= C_WIN_ROWS and (seq // B_PATTERNS[-1][1]) % ATT_Q == 0
    t = bsz * seq
    cos, sin_signed = _rope_tables(seq)
    bias_a = jnp.asarray(_band_bias(A_RADIUS))
    bias_b = jnp.asarray(_band_bias(B_RADIUS))
    w_in, w_br_a, w_br_b, w_br_c, w_o, w_gate_up, w_down = (
        w.astype(BF16) for w in (w_in, w_br_a, w_br_b, w_br_c, w_o, w_gate_up, w_down))
    qkg = jnp.pad(qk_norm_g, ((0, 0), (0, 2), (0, 0)))
    xf = x.reshape(t, d)
    for layer in range(DEPTH):
        proj = _in_proj(xf, norm1_g[layer][None], w_in, layer, cos, sin_signed, qkg[layer], seq)
        proj3 = proj.reshape(bsz, seq, N_IN)
        oa = _attn_a(proj3, sink_a[layer], bias_a)
        ob = _attn_b(proj3, bias_b)
        oc = _attn_c(proj3, _c_bias(rpb_c[layer].reshape(-1)))
        merged = _merge(oa.reshape(t, A_Q_W), ob.reshape(t, B_OUT_W), oc.reshape(t, C_W), proj,
                        w_br_a, w_br_b, w_br_c, layer)
        xf = _residual_matmul(merged, w_o, layer, xf, TN, "out_proj")
        act = _ffn_up(xf, norm2_g[layer][None], w_gate_up, layer)
        xf = _residual_matmul(act, w_down, layer, xf, TN_DOWN, "ffn_down")
    return xf.reshape(bsz, seq, d)
```

```python
import functools

import numpy as np
import jax
import jax.numpy as jnp
from jax import lax
from jax.experimental import pallas as pl
from jax.experimental.pallas import tpu as pltpu

F32 = jnp.float32
BF16 = jnp.bfloat16

D_MODEL = 2048
DEPTH = 4
HEAD_DIM = 128
ROPE_THETA = 10000.0
NORM_EPS = 1e-6
SCALE = HEAD_DIM ** -0.5
NEG = -1e30

A_Q_HEADS = 8
A_KV_HEADS = 2
A_GROUP = A_Q_HEADS // A_KV_HEADS
A_RADIUS = 128
B_PATTERNS = ((128, 1), (512, 4), (2048, 16))
B_GROUPS = len(B_PATTERNS)
B_HEADS_PER_GROUP = 4
B_RADIUS = 64
C_HEADS = 8
GRID_W = 64
C_WIN_ROWS = 8
C_WIN_COLS = 16

A_Q_W = A_Q_HEADS * HEAD_DIM
A_KV_W = A_KV_HEADS * HEAD_DIM
B_W = B_GROUPS * B_HEADS_PER_GROUP * HEAD_DIM
B_OUT_W = B_HEADS_PER_GROUP * HEAD_DIM
C_W = C_HEADS * HEAD_DIM
N_IN = A_Q_W + 2 * A_KV_W + 3 * B_W + 3 * C_W + 3 * D_MODEL
D_FF = ((8 * D_MODEL + 3 * 256 - 1) // (3 * 256)) * 256

OFF_QA = 0
OFF_KA = OFF_QA + A_Q_W
OFF_VA = OFF_KA + A_KV_W
OFF_QB = OFF_VA + A_KV_W
OFF_KB = OFF_QB + B_W
OFF_VB = OFF_KB + B_W
OFF_QC = OFF_VB + B_W
OFF_KC = OFF_QC + C_W
OFF_VC = OFF_KC + C_W
OFF_GATE = OFF_VC + C_W

TM = 1024
TN = 512
TN_DOWN = 256
HEADS_PER_TILE = TN // HEAD_DIM
EPI_CHUNK = 64
NORM_CHUNK = 64
ATT_Q = 256
C_TILE_ROWS = ATT_Q // GRID_W
COMBINE_CHUNK = 128
MIB = 1024 * 1024


def _params(n_axes, vmem_mib):
    return pltpu.CompilerParams(
        dimension_semantics=("parallel",) * n_axes,
        vmem_limit_bytes=vmem_mib * MIB,
    )


def _rmsnorm_rows(x_ref, g_ref, h_ref):
    rows = x_ref.shape[0]

    def body(c, carry):
        r0 = pl.multiple_of(c * NORM_CHUNK, NORM_CHUNK)
        x = x_ref[pl.ds(r0, NORM_CHUNK), :]
        ms = jnp.mean(x * x, axis=-1, keepdims=True)
        h_ref[pl.ds(r0, NORM_CHUNK), :] = (x * lax.rsqrt(ms + NORM_EPS) * g_ref[...]).astype(h_ref.dtype)
        return carry

    lax.fori_loop(0, rows // NORM_CHUNK, body, 0)


def _softmax_pv(s, vcat):
    m = jnp.max(s, axis=-1, keepdims=True)
    p = jnp.exp(s - m)
    l = jnp.sum(p, axis=-1, keepdims=True)
    o = jnp.dot(p.astype(BF16), vcat, preferred_element_type=F32) * (1.0 / l)
    return o, m, l


def _qk(q, k):
    return lax.dot_general(q, k, (((1,), (1,)), ((), ())), preferred_element_type=F32)


def _norm1_body(x_ref, g_ref, o_ref):
    _rmsnorm_rows(x_ref, g_ref, o_ref)


def _norm1(x, g):
    t = x.shape[0]
    return pl.pallas_call(
        _norm1_body,
        out_shape=jax.ShapeDtypeStruct((t, D_MODEL), BF16),
        grid=(t // TM,),
        in_specs=[pl.BlockSpec((TM, D_MODEL), lambda i: (i, 0)),
                  pl.BlockSpec((1, D_MODEL), lambda i: (0, 0))],
        out_specs=pl.BlockSpec((TM, D_MODEL), lambda i: (i, 0)),
        compiler_params=_params(1, 40),
        name="norm1",
    )(x, g)


def _head_kind(head):
    c = head * HEAD_DIM
    if c < OFF_KA:
        return ("norm_rope", 0, True)
    if c < OFF_VA:
        return ("norm_rope", 1, False)
    if c < OFF_QB:
        return ("copy",)
    if c < OFF_KB:
        return ("norm_rope", 2, True)
    if c < OFF_VB:
        return ("norm_rope", 3, False)
    if c < OFF_QC:
        return ("copy",)
    if c < OFF_KC:
        return ("norm", 4, True)
    if c < OFF_VC:
        return ("norm", 5, False)
    if c < OFF_GATE:
        return ("copy",)
    return ("sigmoid",)


def _tile_groups(w_tiles):
    groups = {}
    for j, wt in enumerate(w_tiles):
        kinds = tuple(_head_kind(wt * HEADS_PER_TILE + h) for h in range(HEADS_PER_TILE))
        groups.setdefault(kinds, []).append(j)
    return list(groups.items())


def _in_tiles(j, tiles):
    runs = []
    for t in tiles:
        if runs and runs[-1][1] == t:
            runs[-1][1] = t + 1
        else:
            runs.append([t, t + 1])
    cond = None
    for lo, hi in runs:
        c = (j == lo) if hi == lo + 1 else ((j >= lo) & (j < hi))
        cond = c if cond is None else (cond | c)
    return cond


def _proj_epilogue(kinds, acc_ref, rot_ref, scl_ref, o_ref, cos_ref, sin_ref, qkg_ref, dil):
    rows_per_class = TM // dil
    blocks = []
    for c in range(dil):
        for m0 in range(0, rows_per_class, EPI_CHUNK):
            if dil == 1:
                blocks.append((pl.ds(m0, EPI_CHUNK), c, m0))
            else:
                blocks.append((pl.ds(c + dil * m0, EPI_CHUNK, stride=dil), c, m0))
    half = HEAD_DIM // 2
    for h, kind in enumerate(kinds):
        if kind[0] not in ("norm_rope", "norm"):
            continue
        for src, c, m0 in blocks:
            tmp = pl.ds(c * rows_per_class + m0, EPI_CHUNK)
            a = acc_ref.at[h][src, :]
            ms = jnp.mean(a * a, axis=-1, keepdims=True)
            scl_ref[h, tmp, :] = jnp.broadcast_to(lax.rsqrt(ms + NORM_EPS), a.shape)
            if kind[0] == "norm_rope":
                rot_ref[h, tmp, :] = pltpu.roll(a, half, 1)
    for h, kind in enumerate(kinds):
        cols = slice(h * HEAD_DIM, (h + 1) * HEAD_DIM)
        if kind[0] in ("norm_rope", "norm"):
            g = qkg_ref[kind[1]:kind[1] + 1, :]
            if kind[2]:
                g = g * SCALE
            g_rot = pltpu.roll(g, half, 1)
        for src, c, m0 in blocks:
            tmp = pl.ds(c * rows_per_class + m0, EPI_CHUNK)
            a = acc_ref.at[h][src, :]
            if kind[0] == "norm_rope":
                y = scl_ref[h, tmp, :] * (a * (g * cos_ref[src, :]) + rot_ref[h, tmp, :] * (g_rot * sin_ref[src, :]))
            elif kind[0] == "norm":
                y = a * scl_ref[h, tmp, :] * g
            elif kind[0] == "sigmoid":
                y = 1.0 / (1.0 + jnp.exp(-a))
            else:
                y = a
            if dil == 1:
                o_ref[pl.ds(m0, EPI_CHUNK), cols] = y.astype(o_ref.dtype)
            else:
                o_ref[c, pl.ds(m0, EPI_CHUNK), cols] = y.astype(o_ref.dtype)


def _proj_body(h_ref, w_ref, cos_ref, sin_ref, qkg_ref, o_ref, acc_ref, rot_ref, scl_ref, *, n_cols, groups, dil):
    s = pl.program_id(0)
    je = jnp.maximum(s - 1, 0) % n_cols
    parity = s % 2

    @pl.when(s == 0)
    def _():
        acc_ref[1] = jnp.zeros(acc_ref.shape[1:], F32)

    def step(kinds, p):
        _proj_epilogue(kinds, acc_ref.at[1 - p], rot_ref, scl_ref, o_ref, cos_ref, sin_ref, qkg_ref, dil)
        res = jnp.dot(h_ref[...], w_ref[...], preferred_element_type=F32)
        for h in range(HEADS_PER_TILE):
            acc_ref[p, h] = res[:, h * HEAD_DIM:(h + 1) * HEAD_DIM]

    for kinds, tiles in groups:
        for p in range(2):
            cond = parity == p
            if len(groups) > 1:
                cond = cond & _in_tiles(je, tiles)
            pl.when(cond)(functools.partial(step, kinds, p))


def _proj(h, w_all, layer, w_tiles, cos, sin_signed, qkg, bsz, seq, dil, name):
    t = h.shape[0]
    n_cols = len(w_tiles)
    w_base = w_tiles[0]
    w_step = w_tiles[1] - w_tiles[0] if n_cols > 1 else 0
    assert list(w_tiles) == [w_base + w_step * j for j in range(n_cols)]
    pos_tiles = seq // TM
    n_tiles = (t // TM) * n_cols
    groups = _tile_groups(w_tiles)

    dot_tile = lambda s: jnp.minimum(s, n_tiles - 1)
    epi_tile = lambda s: jnp.maximum(s - 1, 0)
    if dil == 1:
        out_shape = jax.ShapeDtypeStruct((t, n_cols * TN), BF16)
        out_spec = pl.BlockSpec((TM, TN), lambda s: (epi_tile(s) // n_cols, epi_tile(s) % n_cols))
    else:
        out_shape = jax.ShapeDtypeStruct((bsz, dil, seq // dil, n_cols * TN), BF16)
        out_spec = pl.BlockSpec(
            (None, dil, TM // dil, TN),
            lambda s: ((epi_tile(s) // n_cols) // pos_tiles, 0, (epi_tile(s) // n_cols) % pos_tiles,
                       epi_tile(s) % n_cols))
    table_spec = pl.BlockSpec((TM, HEAD_DIM), lambda s: ((epi_tile(s) // n_cols) % pos_tiles, 0))
    return pl.pallas_call(
        functools.partial(_proj_body, n_cols=n_cols, groups=groups, dil=dil),
        out_shape=out_shape,
        grid=(n_tiles + 1,),
        in_specs=[
            pl.BlockSpec((TM, D_MODEL), lambda s: (dot_tile(s) // n_cols, 0)),
            pl.BlockSpec((None, D_MODEL, TN), lambda s: (layer, 0, w_base + w_step * (dot_tile(s) % n_cols))),
            table_spec, table_spec,
            pl.BlockSpec((8, HEAD_DIM), lambda s: (0, 0)),
        ],
        out_specs=out_spec,
        scratch_shapes=[pltpu.VMEM((2, HEADS_PER_TILE, TM, HEAD_DIM), F32),
                        pltpu.VMEM((HEADS_PER_TILE, TM, HEAD_DIM), F32),
                        pltpu.VMEM((HEADS_PER_TILE, TM, HEAD_DIM), F32)],
        compiler_params=pltpu.CompilerParams(dimension_semantics=("arbitrary",), vmem_limit_bytes=40 * MIB),
        name=name,
    )(h, w_all, cos, sin_signed, qkg)


def _band_bias(radius):
    r = np.arange(ATT_Q)[:, None]
    c = np.arange(ATT_Q + 2 * radius)[None, :]
    band = np.abs(c - radius - r) <= radius
    lo_ok = c >= radius
    hi_ok = c < ATT_Q + radius
    variants = [band & lo_ok, band, band & hi_ok, band & lo_ok & hi_ok]
    return np.stack([np.where(v, 0.0, NEG) for v in variants]).astype(np.float32)


def _band_variant(i, n_tiles):
    if n_tiles == 1:
        return 3
    return jnp.where(i == 0, 0, jnp.where(i == n_tiles - 1, 2, 1))


def _halo_specs(block, index, radius, n_rows):
    per = ATT_Q // radius
    last = n_rows // radius - 1
    return [
        pl.BlockSpec(block(radius), lambda *g: index(g, jnp.maximum(g[-1] * per - 1, 0))),
        pl.BlockSpec(block(ATT_Q), lambda *g: index(g, g[-1])),
        pl.BlockSpec(block(radius), lambda *g: index(g, jnp.minimum(g[-1] * per + per, last))),
    ]


def _attn_a_body(sink_ref, q_ref, kp_ref, kc_ref, kn_ref, vp_ref, vc_ref, vn_ref, bias_ref, o_ref):
    hk = pl.program_id(1)
    kcat = jnp.concatenate([kp_ref[...], kc_ref[...], kn_ref[...]], axis=0)
    vcat = jnp.concatenate([vp_ref[...], vc_ref[...], vn_ref[...]], axis=0)
    q = q_ref[...]
    qs = jnp.concatenate([q[:, g * HEAD_DIM:(g + 1) * HEAD_DIM] for g in range(A_GROUP)], axis=0)
    s = _qk(qs, kcat)
    bias = bias_ref[...]
    ps, inv = [], []
    for g in range(A_GROUP):
        sg = s[g * ATT_Q:(g + 1) * ATT_Q, :] + bias
        sink = sink_ref[hk * A_GROUP + g]
        m = jnp.maximum(jnp.max(sg, axis=-1, keepdims=True), sink)
        p = jnp.exp(sg - m)
        l = jnp.sum(p, axis=-1, keepdims=True) + jnp.exp(sink - m)
        ps.append(p.astype(BF16))
        inv.append(1.0 / l)
    o = jnp.dot(jnp.concatenate(ps, axis=0), vcat, preferred_element_type=F32)
    outs = [(o[g * ATT_Q:(g + 1) * ATT_Q, :] * inv[g]).astype(o_ref.dtype) for g in range(A_GROUP)]
    o_ref[...] = jnp.concatenate(outs, axis=1)


def _attn_a(qkv, sink, bias):
    bsz, seq, _ = qkv.shape
    n_tiles = seq // ATT_Q
    qw = A_GROUP * HEAD_DIM
    block = lambda r: (None, r, HEAD_DIM)
    k_specs = _halo_specs(block, lambda g, rb: (g[0], rb, OFF_KA // HEAD_DIM + g[1]), A_RADIUS, seq)
    v_specs = _halo_specs(block, lambda g, rb: (g[0], rb, OFF_VA // HEAD_DIM + g[1]), A_RADIUS, seq)
    return pl.pallas_call(
        _attn_a_body,
        out_shape=jax.ShapeDtypeStruct((bsz, seq, A_Q_W), BF16),
        grid=(bsz, A_KV_HEADS, n_tiles),
        in_specs=[
            pl.BlockSpec(memory_space=pltpu.SMEM),
            pl.BlockSpec((None, ATT_Q, qw), lambda b, h, i: (b, i, h)),
            *k_specs, *v_specs,
            pl.BlockSpec((None, ATT_Q, ATT_Q + 2 * A_RADIUS), lambda b, h, i: (_band_variant(i, n_tiles), 0, 0)),
        ],
        out_specs=pl.BlockSpec((None, ATT_Q, qw), lambda b, h, i: (b, i, h)),
        compiler_params=_params(3, 32),
        name="attn_a",
    )(sink, qkv, qkv, qkv, qkv, qkv, qkv, qkv, bias)


def _attn_b_body(q_ref, kp_ref, kc_ref, kn_ref, vp_ref, vc_ref, vn_ref, bias_ref, o_ref, lse_ref):
    bias = bias_ref[...]
    for h in range(B_HEADS_PER_GROUP):
        cols = slice(h * HEAD_DIM, (h + 1) * HEAD_DIM)
        kcat = jnp.concatenate([kp_ref[:, cols], kc_ref[:, cols], kn_ref[:, cols]], axis=0)
        vcat = jnp.concatenate([vp_ref[:, cols], vc_ref[:, cols], vn_ref[:, cols]], axis=0)
        s = _qk(q_ref[:, cols], kcat) + bias
        o, m, l = _softmax_pv(s, vcat)
        o_ref[:, cols] = o.astype(o_ref.dtype)
        lse_ref[:, cols] = jnp.broadcast_to(m + jnp.log(l), o.shape)


def _attn_b_group(qkv, bias, group):
    bsz, dil, m, _ = qkv.shape
    n_tiles = m // ATT_Q
    block = lambda r: (None, None, r, B_OUT_W)
    k_specs = _halo_specs(block, lambda g, rb: (g[0], g[1], rb, 1), B_RADIUS, m)
    v_specs = _halo_specs(block, lambda g, rb: (g[0], g[1], rb, 2), B_RADIUS, m)
    io_spec = pl.BlockSpec(block(ATT_Q), lambda b, c, i: (b, c, i, 0))
    return pl.pallas_call(
        _attn_b_body,
        out_shape=[jax.ShapeDtypeStruct((bsz, dil, m, B_OUT_W), BF16),
                   jax.ShapeDtypeStruct((bsz, dil, m, B_OUT_W), F32)],
        grid=(bsz, dil, n_tiles),
        in_specs=[io_spec, *k_specs, *v_specs,
                  pl.BlockSpec((None, ATT_Q, ATT_Q + 2 * B_RADIUS),
                               lambda b, c, i: (_band_variant(i, n_tiles), 0, 0))],
        out_specs=[io_spec, io_spec],
        compiler_params=_params(3, 32),
        name=f"attn_b{group}",
    )(qkv, qkv, qkv, qkv, qkv, qkv, qkv, bias)


def _b_combine_body(*refs):
    in_refs = refs[:2 * B_GROUPS]
    out_ref, nat_o, nat_l = refs[2 * B_GROUPS:]
    o0_ref, l0_ref = in_refs[0], in_refs[1]
    for gi in range(1, B_GROUPS):
        o_ref, l_ref = in_refs[2 * gi], in_refs[2 * gi + 1]
        dil = B_PATTERNS[gi][1]
        for c in range(dil):
            dst = pl.ds(c, TM // dil, stride=dil)
            for h in range(B_HEADS_PER_GROUP):
                cols = slice(h * HEAD_DIM, (h + 1) * HEAD_DIM)
                nat_o.at[gi - 1, h][dst, :] = o_ref[c, :, cols].astype(F32)
                nat_l.at[gi - 1, h][dst, :] = l_ref[c, :, cols]

    def body(ci, carry):
        rows = pl.ds(pl.multiple_of(ci * COMBINE_CHUNK, COMBINE_CHUNK), COMBINE_CHUNK)
        for h in range(B_HEADS_PER_GROUP):
            cols = slice(h * HEAD_DIM, (h + 1) * HEAD_DIM)
            lses = [l0_ref[0, rows, cols]] + [nat_l[gi, h, rows, :] for gi in range(B_GROUPS - 1)]
            outs = [o0_ref[0, rows, cols].astype(F32)] + [nat_o[gi, h, rows, :] for gi in range(B_GROUPS - 1)]
            mx = functools.reduce(jnp.maximum, lses)
            ws = [jnp.exp(l - mx) for l in lses]
            num = functools.reduce(lambda a, b: a + b, [w * o for w, o in zip(ws, outs)])
            den = functools.reduce(lambda a, b: a + b, ws)
            out_ref[rows, cols] = (num * (1.0 / den)).astype(out_ref.dtype)
        return carry

    lax.fori_loop(0, TM // COMBINE_CHUNK, body, 0)


def _b_combine(parts, seq):
    assert B_PATTERNS[0][1] == 1
    bsz = parts[0][0].shape[0]
    in_specs, args = [], []
    for (o, lse), (_, dil) in zip(parts, B_PATTERNS):
        spec = pl.BlockSpec((None, dil, TM // dil, B_OUT_W), lambda b, i: (b, 0, i, 0))
        in_specs += [spec, spec]
        args += [o, lse]
    nat_shape = (B_GROUPS - 1, B_HEADS_PER_GROUP, TM, HEAD_DIM)
    return pl.pallas_call(
        _b_combine_body,
        out_shape=jax.ShapeDtypeStruct((bsz, seq, B_OUT_W), BF16),
        grid=(bsz, seq // TM),
        in_specs=in_specs,
        out_specs=pl.BlockSpec((None, TM, B_OUT_W), lambda b, i: (b, i, 0)),
        scratch_shapes=[pltpu.VMEM(nat_shape, F32), pltpu.VMEM(nat_shape, F32)],
        compiler_params=_params(2, 40),
        name="b_combine",
    )(*args)


N_DROW = 2 * C_WIN_ROWS - 1
N_DCOL = 2 * C_WIN_COLS - 1
C_KEY_TILES = 3 * C_TILE_ROWS


def _c_bias_body(rpb_ref, o_ref):
    h = pl.program_id(0)
    qc = lax.broadcasted_iota(jnp.int32, (GRID_W, GRID_W), 0)
    kc = lax.broadcasted_iota(jnp.int32, (GRID_W, GRID_W), 1)
    dcol = jnp.clip(kc - qc, -(C_WIN_COLS - 1), C_WIN_COLS - 1) + (C_WIN_COLS - 1)
    col_start = jnp.clip(qc - C_WIN_COLS // 2, 0, GRID_W - C_WIN_COLS)
    col_ok = (kc >= col_start) & (kc < col_start + C_WIN_COLS)
    tiles = []
    for dr in range(N_DROW):
        t = jnp.zeros((GRID_W, GRID_W), F32)
        for d in range(N_DCOL):
            t = jnp.where(dcol == d, rpb_ref[(h * N_DROW + dr) * N_DCOL + d], t)
        tiles.append(jnp.where(col_ok, t, NEG))
    masked = jnp.full((GRID_W, GRID_W), NEG, F32)
    half = C_WIN_ROWS // 2
    for var in range(3):
        for qi in range(C_TILE_ROWS):
            row = []
            for kj in range(C_KEY_TILES):
                if var == 0:
                    ok = C_TILE_ROWS <= kj < C_TILE_ROWS + C_WIN_ROWS
                elif var == 2:
                    ok = 2 * C_TILE_ROWS - C_WIN_ROWS <= kj < 2 * C_TILE_ROWS
                else:
                    ok = 0 <= kj - C_TILE_ROWS - qi + half < C_WIN_ROWS
                drow = kj - C_TILE_ROWS - qi
                row.append(tiles[drow + C_WIN_ROWS - 1] if ok else masked)
            o_ref[var, qi * GRID_W:(qi + 1) * GRID_W, :] = jnp.concatenate(row, axis=1)


def _c_bias(rpb_flat):
    return pl.pallas_call(
        _c_bias_body,
        out_shape=jax.ShapeDtypeStruct((3, C_HEADS, ATT_Q, C_KEY_TILES * GRID_W), F32),
        grid=(C_HEADS,),
        in_specs=[pl.BlockSpec(memory_space=pltpu.SMEM)],
        out_specs=pl.BlockSpec((3, None, ATT_Q, C_KEY_TILES * GRID_W), lambda h: (0, h, 0, 0)),
        compiler_params=_params(1, 32),
        name="c_bias",
    )(rpb_flat)


def _attn_c_body(q_ref, kp_ref, kc_ref, kn_ref, vp_ref, vc_ref, vn_ref, bias_ref, o_ref):
    for h in range(C_HEADS):
        cols = slice(h * HEAD_DIM, (h + 1) * HEAD_DIM)
        kcat = jnp.concatenate([kp_ref[:, cols], kc_ref[:, cols], kn_ref[:, cols]], axis=0)
        vcat = jnp.concatenate([vp_ref[:, cols], vc_ref[:, cols], vn_ref[:, cols]], axis=0)
        s = _qk(q_ref[:, cols], kcat) + bias_ref[h]
        o, _, _ = _softmax_pv(s, vcat)
        o_ref[:, cols] = o.astype(o_ref.dtype)


def _attn_c(qkv, bias):
    bsz, seq, _ = qkv.shape
    n_tiles = seq // ATT_Q
    block = lambda r: (None, r, C_W)
    k_specs = _halo_specs(block, lambda g, rb: (g[0], rb, 1), ATT_Q, seq)
    v_specs = _halo_specs(block, lambda g, rb: (g[0], rb, 2), ATT_Q, seq)
    return pl.pallas_call(
        _attn_c_body,
        out_shape=jax.ShapeDtypeStruct((bsz, seq, C_W), BF16),
        grid=(bsz, n_tiles),
        in_specs=[
            pl.BlockSpec(block(ATT_Q), lambda b, i: (b, i, 0)),
            *k_specs, *v_specs,
            pl.BlockSpec((None, C_HEADS, ATT_Q, C_KEY_TILES * GRID_W),
                         lambda b, i: (jnp.where(i == 0, 0, jnp.where(i == n_tiles - 1, 2, 1)), 0, 0, 0)),
        ],
        out_specs=pl.BlockSpec(block(ATT_Q), lambda b, i: (b, i, 0)),
        compiler_params=_params(2, 40),
        name="attn_c",
    )(qkv, qkv, qkv, qkv, qkv, qkv, qkv, bias)


def _merge_body(oa_ref, ob_ref, oc_ref, ga_ref, gb_ref, gc_ref, wa_ref, wb_ref, wc_ref, o_ref):
    m = ga_ref[...].astype(F32) * jnp.dot(oa_ref[...], wa_ref[...], preferred_element_type=F32)
    m = m + gb_ref[...].astype(F32) * jnp.dot(ob_ref[...], wb_ref[...], preferred_element_type=F32)
    m = m + gc_ref[...].astype(F32) * jnp.dot(oc_ref[...], wc_ref[...], preferred_element_type=F32)
    o_ref[...] = m.astype(o_ref.dtype)


def _merge(oa, ob, oc, gates, wa, wb, wc, layer):
    t = oa.shape[0]
    gate = lambda k: pl.BlockSpec((TM, TN), lambda i, j: (i, k * (D_MODEL // TN) + j))
    wspec = lambda rows: pl.BlockSpec((None, rows, TN), lambda i, j: (layer, 0, j))
    return pl.pallas_call(
        _merge_body,
        out_shape=jax.ShapeDtypeStruct((t, D_MODEL), BF16),
        grid=(t // TM, D_MODEL // TN),
        in_specs=[
            pl.BlockSpec((TM, A_Q_W), lambda i, j: (i, 0)),
            pl.BlockSpec((TM, B_OUT_W), lambda i, j: (i, 0)),
            pl.BlockSpec((TM, C_W), lambda i, j: (i, 0)),
            gate(0), gate(1), gate(2),
            wspec(A_Q_W), wspec(B_OUT_W), wspec(C_W),
        ],
        out_specs=pl.BlockSpec((TM, TN), lambda i, j: (i, j)),
        compiler_params=_params(2, 48),
        name="merge",
    )(oa, ob, oc, gates, gates, gates, wa, wb, wc)


def _residual_matmul_body(a_ref, w_ref, x_ref, o_ref):
    o_ref[...] = x_ref[...] + jnp.dot(a_ref[...], w_ref[...], preferred_element_type=F32)


def _residual_matmul(a, w_all, layer, x, tn, name):
    t, k = a.shape
    return pl.pallas_call(
        _residual_matmul_body,
        out_shape=jax.ShapeDtypeStruct((t, D_MODEL), F32),
        grid=(t // TM, D_MODEL // tn),
        in_specs=[
            pl.BlockSpec((TM, k), lambda i, j: (i, 0)),
            pl.BlockSpec((None, k, tn), lambda i, j: (layer, 0, j)),
            pl.BlockSpec((TM, tn), lambda i, j: (i, j)),
        ],
        out_specs=pl.BlockSpec((TM, tn), lambda i, j: (i, j)),
        compiler_params=_params(2, 48),
        name=name,
    )(a, w_all, x)


def _ffn_up_body(x_ref, g_ref, wg_ref, wu_ref, o_ref, h_ref):
    @pl.when(pl.program_id(1) == 0)
    def _():
        _rmsnorm_rows(x_ref, g_ref, h_ref)

    h = h_ref[...]
    gt = jnp.dot(h, wg_ref[...], preferred_element_type=F32)
    up = jnp.dot(h, wu_ref[...], preferred_element_type=F32)
    o_ref[...] = (gt * (1.0 / (1.0 + jnp.exp(-gt))) * up).astype(o_ref.dtype)


def _ffn_up(x, g, w_all, layer):
    t = x.shape[0]
    n_tiles = D_FF // TN
    return pl.pallas_call(
        _ffn_up_body,
        out_shape=jax.ShapeDtypeStruct((t, D_FF), BF16),
        grid=(t // TM, n_tiles),
        in_specs=[
            pl.BlockSpec((TM, D_MODEL), lambda i, j: (i, 0)),
            pl.BlockSpec((1, D_MODEL), lambda i, j: (0, 0)),
            pl.BlockSpec((None, D_MODEL, TN), lambda i, j: (layer, 0, j)),
            pl.BlockSpec((None, D_MODEL, TN), lambda i, j: (layer, 0, n_tiles + j)),
        ],
        out_specs=pl.BlockSpec((TM, TN), lambda i, j: (i, j)),
        scratch_shapes=[pltpu.VMEM((TM, D_MODEL), BF16)],
        compiler_params=_params(2, 48),
        name="ffn_up",
    )(x, g, w_all, w_all)


def _rope_tables(n):
    half = HEAD_DIM // 2
    inv_freq = ROPE_THETA ** (-jnp.arange(half, dtype=F32) * 2.0 / HEAD_DIM)
    ang = jnp.arange(n, dtype=F32)[:, None] * inv_freq[None, :]
    cos, sin = jnp.cos(ang), jnp.sin(ang)
    return jnp.concatenate([cos, cos], axis=-1), jnp.concatenate([-sin, sin], axis=-1)


def kernel(x, norm1_g, w_in, qk_norm_g, sink_a, rpb_c, w_br_a, w_br_b, w_br_c, w_o,
           norm2_g, w_gate_up, w_down):
    bsz, seq, d = x.shape
    assert d == D_MODEL and seq % TM == 0 and seq // ATT_Q >= 2
    assert seq // GRID_W >= C_WIN_ROWS and (seq // B_PATTERNS[-1][1]) % ATT_Q == 0
    t = bsz * seq
    cos, sin_signed = _rope_tables(seq)
    bias_a = jnp.asarray(_band_bias(A_RADIUS))
    bias_b = jnp.asarray(_band_bias(B_RADIUS))
    w_in, w_br_a, w_br_b, w_br_c, w_o, w_gate_up, w_down = (
        w.astype(BF16) for w in (w_in, w_br_a, w_br_b, w_br_c, w_o, w_gate_up, w_down))
    qkg = jnp.pad(qk_norm_g, ((0, 0), (0, 2), (0, 0)))
    tile = lambda off: off // TN
    xf = x.reshape(t, d)
    for layer in range(DEPTH):
        h = _norm1(xf, norm1_g[layer][None])
        proj = functools.partial(_proj, h, w_in, layer, cos=cos, sin_signed=sin_signed, qkg=qkg[layer],
                                 bsz=bsz, seq=seq)
        qkv_a = proj(range(tile(OFF_QA), tile(OFF_QB)), dil=1, name="proj_a")
        oa = _attn_a(qkv_a.reshape(bsz, seq, -1), sink_a[layer], bias_a)
        parts = []
        for group, (_, dil) in enumerate(B_PATTERNS):
            tiles = [tile(off) + group for off in (OFF_QB, OFF_KB, OFF_VB)]
            qkv_b = proj(tiles, dil=dil, name=f"proj_b{group}")
            if dil == 1:
                qkv_b = qkv_b.reshape(bsz, 1, seq, -1)
            parts.append(_attn_b_group(qkv_b, bias_b, group))
        ob = _b_combine(parts, seq)
        qkv_c = proj(range(tile(OFF_QC), tile(OFF_GATE)), dil=1, name="proj_c")
        oc = _attn_c(qkv_c.reshape(bsz, seq, -1), _c_bias(rpb_c[layer].reshape(-1)))
        gates = proj(range(tile(OFF_GATE), tile(N_IN)), dil=1, name="proj_g")
        merged = _merge(oa.reshape(t, A_Q_W), ob.reshape(t, B_OUT_W), oc.reshape(t, C_W), gates,
                        w_br_a, w_br_b, w_br_c, layer)
        xf = _residual_matmul(merged, w_o, layer, xf, TN, "out_proj")
        act = _ffn_up(xf, norm2_g[layer][None], w_gate_up, layer)
        xf = _residual_matmul(act, w_down, layer, xf, TN_DOWN, "ffn_down")
    return xf.reshape(bsz, seq, d)
```

```python
import functools

import numpy as np
import jax
import jax.numpy as jnp
from jax import lax
from jax.experimental import pallas as pl
from jax.experimental.pallas import tpu as pltpu

F32 = jnp.float32
BF16 = jnp.bfloat16

D_MODEL = 2048
DEPTH = 4
HEAD_DIM = 128
ROPE_THETA = 10000.0
NORM_EPS = 1e-6
SCALE = HEAD_DIM ** -0.5
LOG2E = 1.4426950408889634
NEG = -1e30

A_Q_HEADS = 8
A_KV_HEADS = 2
A_GROUP = A_Q_HEADS // A_KV_HEADS
A_RADIUS = 128
B_PATTERNS = ((128, 1), (512, 4), (2048, 16))
B_GROUPS = len(B_PATTERNS)
B_HEADS_PER_GROUP = 4
B_RADIUS = 64
C_HEADS = 8
GRID_W = 64
C_WIN_ROWS = 8
C_WIN_COLS = 16

A_Q_W = A_Q_HEADS * HEAD_DIM
A_KV_W = A_KV_HEADS * HEAD_DIM
B_W = B_GROUPS * B_HEADS_PER_GROUP * HEAD_DIM
B_OUT_W = B_HEADS_PER_GROUP * HEAD_DIM
C_W = C_HEADS * HEAD_DIM
N_IN = A_Q_W + 2 * A_KV_W + 3 * B_W + 3 * C_W + 3 * D_MODEL
D_FF = ((8 * D_MODEL + 3 * 256 - 1) // (3 * 256)) * 256

OFF_QA = 0
OFF_KA = OFF_QA + A_Q_W
OFF_VA = OFF_KA + A_KV_W
OFF_QB = OFF_VA + A_KV_W
OFF_KB = OFF_QB + B_W
OFF_VB = OFF_KB + B_W
OFF_QC = OFF_VB + B_W
OFF_KC = OFF_QC + C_W
OFF_VC = OFF_KC + C_W
OFF_GATE = OFF_VC + C_W

TM = 1024
TN = 512
TN_DOWN = 256
HEADS_PER_TILE = TN // HEAD_DIM
EPI_CHUNK = 64
NORM_CHUNK = 64
ATT_Q = 256
C_TILE_ROWS = ATT_Q // GRID_W
COMBINE_CHUNK = 128
MIB = 1024 * 1024


def _params(n_axes, vmem_mib):
    return pltpu.CompilerParams(
        dimension_semantics=("parallel",) * n_axes,
        vmem_limit_bytes=vmem_mib * MIB,
    )


def _rmsnorm_rows(x_ref, g_ref, h_ref):
    rows = x_ref.shape[0]

    def body(c, carry):
        r0 = pl.multiple_of(c * NORM_CHUNK, NORM_CHUNK)
        x = x_ref[pl.ds(r0, NORM_CHUNK), :]
        ms = jnp.mean(x * x, axis=-1, keepdims=True)
        h_ref[pl.ds(r0, NORM_CHUNK), :] = (x * lax.rsqrt(ms + NORM_EPS) * g_ref[...]).astype(h_ref.dtype)
        return carry

    lax.fori_loop(0, rows // NORM_CHUNK, body, 0)


def _pipelined_attention(n_units, scores, values, finish, extra_logit=None):
    s_next = scores(0)
    prev = None
    for u in range(n_units + 1):
        s = s_next
        if u + 1 < n_units:
            s_next = scores(u + 1)
        cur = None
        if u < n_units:
            m = jnp.max(s, axis=-1, keepdims=True)
            if extra_logit is not None:
                m = jnp.maximum(m, extra_logit(u))
            p = jnp.exp2(s - m)
            l = jnp.sum(p, axis=-1, keepdims=True)
            if extra_logit is not None:
                l = l + jnp.exp2(extra_logit(u) - m)
            cur = (p.astype(BF16), m, l)
        if prev is not None:
            p_prev, m_prev, l_prev = prev
            o = jnp.dot(p_prev, values(u - 1), preferred_element_type=F32) * (1.0 / l_prev)
            finish(u - 1, o, m_prev, l_prev)
        prev = cur


def _qk(q, k):
    return lax.dot_general(q, k, (((1,), (1,)), ((), ())), preferred_element_type=F32)


def _norm1_body(x_ref, g_ref, o_ref):
    _rmsnorm_rows(x_ref, g_ref, o_ref)


def _norm1(x, g):
    t = x.shape[0]
    return pl.pallas_call(
        _norm1_body,
        out_shape=jax.ShapeDtypeStruct((t, D_MODEL), BF16),
        grid=(t // TM,),
        in_specs=[pl.BlockSpec((TM, D_MODEL), lambda i: (i, 0)),
                  pl.BlockSpec((1, D_MODEL), lambda i: (0, 0))],
        out_specs=pl.BlockSpec((TM, D_MODEL), lambda i: (i, 0)),
        compiler_params=_params(1, 40),
        name="norm1",
    )(x, g)


def _head_kind(head):
    c = head * HEAD_DIM
    if c < OFF_KA:
        return ("norm_rope", 0, True)
    if c < OFF_VA:
        return ("norm_rope", 1, False)
    if c < OFF_QB:
        return ("copy",)
    if c < OFF_KB:
        return ("norm_rope", 2, True)
    if c < OFF_VB:
        return ("norm_rope", 3, False)
    if c < OFF_QC:
        return ("copy",)
    if c < OFF_KC:
        return ("norm", 4, True)
    if c < OFF_VC:
        return ("norm", 5, False)
    if c < OFF_GATE:
        return ("copy",)
    return ("sigmoid",)


def _tile_groups(w_tiles):
    groups = {}
    for j, wt in enumerate(w_tiles):
        kinds = tuple(_head_kind(wt * HEADS_PER_TILE + h) for h in range(HEADS_PER_TILE))
        groups.setdefault(kinds, []).append(j)
    return list(groups.items())


def _in_tiles(j, tiles):
    runs = []
    for t in tiles:
        if runs and runs[-1][1] == t:
            runs[-1][1] = t + 1
        else:
            runs.append([t, t + 1])
    cond = None
    for lo, hi in runs:
        c = (j == lo) if hi == lo + 1 else ((j >= lo) & (j < hi))
        cond = c if cond is None else (cond | c)
    return cond


def _proj_epilogue(kinds, acc_ref, rot_ref, scl_ref, o_ref, cos_ref, sin_ref, qkg_ref, dil):
    rows_per_class = TM // dil
    blocks = []
    for c in range(dil):
        for m0 in range(0, rows_per_class, EPI_CHUNK):
            if dil == 1:
                blocks.append((pl.ds(m0, EPI_CHUNK), c, m0))
            else:
                blocks.append((pl.ds(c + dil * m0, EPI_CHUNK, stride=dil), c, m0))
    half = HEAD_DIM // 2

    def pass1(h, kind):
        if kind[0] not in ("norm_rope", "norm"):
            return
        for src, c, m0 in blocks:
            tmp = pl.ds(c * rows_per_class + m0, EPI_CHUNK)
            a = acc_ref.at[h][src, :]
            ms = jnp.mean(a * a, axis=-1, keepdims=True)
            scl_ref[h, tmp, :] = jnp.broadcast_to(lax.rsqrt(ms + NORM_EPS), a.shape)
            if kind[0] == "norm_rope":
                rot_ref[h, tmp, :] = pltpu.roll(a, half, 1)

    def pass2():
        gains = {}
        for kind in kinds:
            if kind[0] in ("norm_rope", "norm") and kind[1:] not in gains:
                g = qkg_ref[kind[1]:kind[1] + 1, :]
                if kind[2]:
                    g = g * (SCALE * LOG2E)
                gains[kind[1:]] = (g, pltpu.roll(g, half, 1))
        for src, c, m0 in blocks:
            tmp = pl.ds(c * rows_per_class + m0, EPI_CHUNK)
            tables = {}
            for h, kind in enumerate(kinds):
                cols = slice(h * HEAD_DIM, (h + 1) * HEAD_DIM)
                a = acc_ref.at[h][src, :]
                if kind[0] == "norm_rope":
                    if kind[1:] not in tables:
                        g, g_rot = gains[kind[1:]]
                        tables[kind[1:]] = (g * cos_ref[src, :], g_rot * sin_ref[src, :])
                    t_cos, t_sin = tables[kind[1:]]
                    y = scl_ref[h, tmp, :] * (a * t_cos + rot_ref[h, tmp, :] * t_sin)
                elif kind[0] == "norm":
                    y = a * scl_ref[h, tmp, :] * gains[kind[1:]][0]
                elif kind[0] == "sigmoid":
                    y = 1.0 / (1.0 + jnp.exp(-a))
                else:
                    y = a
                if dil == 1:
                    o_ref[pl.ds(m0, EPI_CHUNK), cols] = y.astype(o_ref.dtype)
                else:
                    o_ref[c, pl.ds(m0, EPI_CHUNK), cols] = y.astype(o_ref.dtype)

    for h, kind in enumerate(kinds):
        pass1(h, kind)
    pass2()


def _proj_body(h_ref, w_ref, cos_ref, sin_ref, qkg_ref, o_ref, acc_ref, rot_ref, scl_ref, *, n_cols, groups, dil):
    s = pl.program_id(0)
    je = jnp.maximum(s - 1, 0) % n_cols
    parity = s % 2

    @pl.when(s == 0)
    def _():
        acc_ref[1] = jnp.zeros(acc_ref.shape[1:], F32)

    def step(kinds, p):
        _proj_epilogue(kinds, acc_ref.at[1 - p], rot_ref, scl_ref, o_ref, cos_ref, sin_ref, qkg_ref, dil)
        res = jnp.dot(h_ref[...], w_ref[...], preferred_element_type=F32)
        for h in range(HEADS_PER_TILE):
            acc_ref[p, h] = res[:, h * HEAD_DIM:(h + 1) * HEAD_DIM]

    for kinds, tiles in groups:
        for p in range(2):
            cond = parity == p
            if len(groups) > 1:
                cond = cond & _in_tiles(je, tiles)
            pl.when(cond)(functools.partial(step, kinds, p))


def _proj(h, w_all, layer, w_tiles, cos, sin_signed, qkg, bsz, seq, dil, name):
    t = h.shape[0]
    n_cols = len(w_tiles)
    w_base = w_tiles[0]
    w_step = w_tiles[1] - w_tiles[0] if n_cols > 1 else 0
    assert list(w_tiles) == [w_base + w_step * j for j in range(n_cols)]
    pos_tiles = seq // TM
    n_tiles = (t // TM) * n_cols
    groups = _tile_groups(w_tiles)

    dot_tile = lambda s: jnp.minimum(s, n_tiles - 1)
    epi_tile = lambda s: jnp.maximum(s - 1, 0)
    if dil == 1:
        out_shape = jax.ShapeDtypeStruct((t, n_cols * TN), BF16)
        out_spec = pl.BlockSpec((TM, TN), lambda s: (epi_tile(s) // n_cols, epi_tile(s) % n_cols))
    else:
        out_shape = jax.ShapeDtypeStruct((bsz, dil, seq // dil, n_cols * TN), BF16)
        out_spec = pl.BlockSpec(
            (None, dil, TM // dil, TN),
            lambda s: ((epi_tile(s) // n_cols) // pos_tiles, 0, (epi_tile(s) // n_cols) % pos_tiles,
                       epi_tile(s) % n_cols))
    table_spec = pl.BlockSpec((TM, HEAD_DIM), lambda s: ((epi_tile(s) // n_cols) % pos_tiles, 0))
    return pl.pallas_call(
        functools.partial(_proj_body, n_cols=n_cols, groups=groups, dil=dil),
        out_shape=out_shape,
        grid=(n_tiles + 1,),
        in_specs=[
            pl.BlockSpec((TM, D_MODEL), lambda s: (dot_tile(s) // n_cols, 0)),
            pl.BlockSpec((None, D_MODEL, TN), lambda s: (layer, 0, w_base + w_step * (dot_tile(s) % n_cols))),
            table_spec, table_spec,
            pl.BlockSpec((8, HEAD_DIM), lambda s: (0, 0)),
        ],
        out_specs=out_spec,
        scratch_shapes=[pltpu.VMEM((2, HEADS_PER_TILE, TM, HEAD_DIM), F32),
                        pltpu.VMEM((HEADS_PER_TILE, TM, HEAD_DIM), F32),
                        pltpu.VMEM((HEADS_PER_TILE, TM, HEAD_DIM), F32)],
        compiler_params=pltpu.CompilerParams(dimension_semantics=("arbitrary",), vmem_limit_bytes=40 * MIB),
        name=name,
    )(h, w_all, cos, sin_signed, qkg)


def _band_bias(radius):
    r = np.arange(ATT_Q)[:, None]
    c = np.arange(ATT_Q + 2 * radius)[None, :]
    band = np.abs(c - radius - r) <= radius
    lo_ok = c >= radius
    hi_ok = c < ATT_Q + radius
    variants = [band & lo_ok, band, band & hi_ok, band & lo_ok & hi_ok]
    return np.stack([np.where(v, 0.0, NEG) for v in variants]).astype(np.float32)


def _band_variant(i, n_tiles):
    if n_tiles == 1:
        return 3
    return jnp.where(i == 0, 0, jnp.where(i == n_tiles - 1, 2, 1))


def _halo_specs(block, index, radius, n_rows):
    per = ATT_Q // radius
    last = n_rows // radius - 1
    return [
        pl.BlockSpec(block(radius), lambda *g: index(g, jnp.maximum(g[-1] * per - 1, 0))),
        pl.BlockSpec(block(ATT_Q), lambda *g: index(g, g[-1])),
        pl.BlockSpec(block(radius), lambda *g: index(g, jnp.minimum(g[-1] * per + per, last))),
    ]


def _attn_a_body(sink_ref, q_ref, kp_ref, kc_ref, kn_ref, vp_ref, vc_ref, vn_ref, bias_ref, o_ref):
    head_cols = lambda h: slice(h * HEAD_DIM, (h + 1) * HEAD_DIM)
    kcat = [jnp.concatenate([r[:, head_cols(kv)] for r in (kp_ref, kc_ref, kn_ref)], axis=0)
            for kv in range(A_KV_HEADS)]
    vcat = [jnp.concatenate([r[:, head_cols(kv)] for r in (vp_ref, vc_ref, vn_ref)], axis=0)
            for kv in range(A_KV_HEADS)]

    def finish(h, o, m, l):
        o_ref[:, head_cols(h)] = o.astype(o_ref.dtype)

    _pipelined_attention(
        A_Q_HEADS,
        scores=lambda h: _qk(q_ref[:, head_cols(h)], kcat[h // A_GROUP]) + bias_ref[...],
        values=lambda h: vcat[h // A_GROUP],
        finish=finish,
        extra_logit=lambda h: sink_ref[h] * LOG2E)


def _attn_a(qkv, sink, bias):
    bsz, seq, _ = qkv.shape
    n_tiles = seq // ATT_Q
    block = lambda r: (None, r, A_KV_W)
    k_specs = _halo_specs(block, lambda g, rb: (g[0], rb, OFF_KA // A_KV_W), A_RADIUS, seq)
    v_specs = _halo_specs(block, lambda g, rb: (g[0], rb, OFF_VA // A_KV_W), A_RADIUS, seq)
    return pl.pallas_call(
        _attn_a_body,
        out_shape=jax.ShapeDtypeStruct((bsz, seq, A_Q_W), BF16),
        grid=(bsz, n_tiles),
        in_specs=[
            pl.BlockSpec(memory_space=pltpu.SMEM),
            pl.BlockSpec((None, ATT_Q, A_Q_W), lambda b, i: (b, i, 0)),
            *k_specs, *v_specs,
            pl.BlockSpec((None, ATT_Q, ATT_Q + 2 * A_RADIUS), lambda b, i: (_band_variant(i, n_tiles), 0, 0)),
        ],
        out_specs=pl.BlockSpec((None, ATT_Q, A_Q_W), lambda b, i: (b, i, 0)),
        compiler_params=_params(2, 32),
        name="attn_a",
    )(sink, qkv, qkv, qkv, qkv, qkv, qkv, qkv, bias)


def _attn_b_body(q_ref, kp_ref, kc_ref, kn_ref, vp_ref, vc_ref, vn_ref, bias_ref, o_ref, lse_ref):
    head_cols = lambda h: slice(h * HEAD_DIM, (h + 1) * HEAD_DIM)
    cat = lambda refs, h: jnp.concatenate([r[:, head_cols(h)] for r in refs], axis=0)

    def finish(h, o, m, l):
        o_ref[:, head_cols(h)] = o.astype(o_ref.dtype)
        lse_ref[:, head_cols(h)] = jnp.broadcast_to(m + jnp.log2(l), o.shape)

    _pipelined_attention(
        B_HEADS_PER_GROUP,
        scores=lambda h: _qk(q_ref[:, head_cols(h)], cat((kp_ref, kc_ref, kn_ref), h)) + bias_ref[...],
        values=lambda h: cat((vp_ref, vc_ref, vn_ref), h),
        finish=finish)


def _attn_b_group(qkv, bias, group):
    bsz, dil, m, _ = qkv.shape
    n_tiles = m // ATT_Q
    block = lambda r: (None, None, r, B_OUT_W)
    k_specs = _halo_specs(block, lambda g, rb: (g[0], g[1], rb, 1), B_RADIUS, m)
    v_specs = _halo_specs(block, lambda g, rb: (g[0], g[1], rb, 2), B_RADIUS, m)
    io_spec = pl.BlockSpec(block(ATT_Q), lambda b, c, i: (b, c, i, 0))
    return pl.pallas_call(
        _attn_b_body,
        out_shape=[jax.ShapeDtypeStruct((bsz, dil, m, B_OUT_W), BF16),
                   jax.ShapeDtypeStruct((bsz, dil, m, B_OUT_W), F32)],
        grid=(bsz, dil, n_tiles),
        in_specs=[io_spec, *k_specs, *v_specs,
                  pl.BlockSpec((None, ATT_Q, ATT_Q + 2 * B_RADIUS),
                               lambda b, c, i: (_band_variant(i, n_tiles), 0, 0))],
        out_specs=[io_spec, io_spec],
        compiler_params=_params(3, 32),
        name=f"attn_b{group}",
    )(qkv, qkv, qkv, qkv, qkv, qkv, qkv, bias)


def _b_combine_body(*refs):
    in_refs = refs[:2 * B_GROUPS]
    out_ref, nat_o, nat_l = refs[2 * B_GROUPS:]
    o0_ref, l0_ref = in_refs[0], in_refs[1]
    for gi in range(1, B_GROUPS):
        o_ref, l_ref = in_refs[2 * gi], in_refs[2 * gi + 1]
        dil = B_PATTERNS[gi][1]
        for c in range(dil):
            dst = pl.ds(c, TM // dil, stride=dil)
            for h in range(B_HEADS_PER_GROUP):
                cols = slice(h * HEAD_DIM, (h + 1) * HEAD_DIM)
                nat_o.at[gi - 1, h][dst, :] = o_ref[c, :, cols].astype(F32)
                nat_l.at[gi - 1, h][dst, :] = l_ref[c, :, cols]

    def body(ci, carry):
        rows = pl.ds(pl.multiple_of(ci * COMBINE_CHUNK, COMBINE_CHUNK), COMBINE_CHUNK)
        for h in range(B_HEADS_PER_GROUP):
            cols = slice(h * HEAD_DIM, (h + 1) * HEAD_DIM)
            lses = [l0_ref[0, rows, cols]] + [nat_l[gi, h, rows, :] for gi in range(B_GROUPS - 1)]
            outs = [o0_ref[0, rows, cols].astype(F32)] + [nat_o[gi, h, rows, :] for gi in range(B_GROUPS - 1)]
            mx = functools.reduce(jnp.maximum, lses)
            ws = [jnp.exp2(l - mx) for l in lses]
            num = functools.reduce(lambda a, b: a + b, [w * o for w, o in zip(ws, outs)])
            den = functools.reduce(lambda a, b: a + b, ws)
            out_ref[rows, cols] = (num * (1.0 / den)).astype(out_ref.dtype)
        return carry

    lax.fori_loop(0, TM // COMBINE_CHUNK, body, 0)


def _b_combine(parts, seq):
    assert B_PATTERNS[0][1] == 1
    bsz = parts[0][0].shape[0]
    in_specs, args = [], []
    for (o, lse), (_, dil) in zip(parts, B_PATTERNS):
        spec = pl.BlockSpec((None, dil, TM // dil, B_OUT_W), lambda b, i: (b, 0, i, 0))
        in_specs += [spec, spec]
        args += [o, lse]
    nat_shape = (B_GROUPS - 1, B_HEADS_PER_GROUP, TM, HEAD_DIM)
    return pl.pallas_call(
        _b_combine_body,
        out_shape=jax.ShapeDtypeStruct((bsz, seq, B_OUT_W), BF16),
        grid=(bsz, seq // TM),
        in_specs=in_specs,
        out_specs=pl.BlockSpec((None, TM, B_OUT_W), lambda b, i: (b, i, 0)),
        scratch_shapes=[pltpu.VMEM(nat_shape, F32), pltpu.VMEM(nat_shape, F32)],
        compiler_params=_params(2, 40),
        name="b_combine",
    )(*args)


N_DROW = 2 * C_WIN_ROWS - 1
N_DCOL = 2 * C_WIN_COLS - 1
C_KEY_TILES = 3 * C_TILE_ROWS


def _c_bias_body(rpb_ref, o_ref):
    h = pl.program_id(0)
    qc = lax.broadcasted_iota(jnp.int32, (GRID_W, GRID_W), 0)
    kc = lax.broadcasted_iota(jnp.int32, (GRID_W, GRID_W), 1)
    dcol = jnp.clip(kc - qc, -(C_WIN_COLS - 1), C_WIN_COLS - 1) + (C_WIN_COLS - 1)
    col_start = jnp.clip(qc - C_WIN_COLS // 2, 0, GRID_W - C_WIN_COLS)
    col_ok = (kc >= col_start) & (kc < col_start + C_WIN_COLS)
    tiles = []
    for dr in range(N_DROW):
        t = jnp.zeros((GRID_W, GRID_W), F32)
        for d in range(N_DCOL):
            t = jnp.where(dcol == d, rpb_ref[(h * N_DROW + dr) * N_DCOL + d] * LOG2E, t)
        tiles.append(jnp.where(col_ok, t, NEG))
    masked = jnp.full((GRID_W, GRID_W), NEG, F32)
    half = C_WIN_ROWS // 2
    for var in range(3):
        for qi in range(C_TILE_ROWS):
            row = []
            for kj in range(C_KEY_TILES):
                if var == 0:
                    ok = C_TILE_ROWS <= kj < C_TILE_ROWS + C_WIN_ROWS
                elif var == 2:
                    ok = 2 * C_TILE_ROWS - C_WIN_ROWS <= kj < 2 * C_TILE_ROWS
                else:
                    ok = 0 <= kj - C_TILE_ROWS - qi + half < C_WIN_ROWS
                drow = kj - C_TILE_ROWS - qi
                row.append(tiles[drow + C_WIN_ROWS - 1] if ok else masked)
            o_ref[var, qi * GRID_W:(qi + 1) * GRID_W, :] = jnp.concatenate(row, axis=1)


def _c_bias(rpb_flat):
    return pl.pallas_call(
        _c_bias_body,
        out_shape=jax.ShapeDtypeStruct((3, C_HEADS, ATT_Q, C_KEY_TILES * GRID_W), F32),
        grid=(C_HEADS,),
        in_specs=[pl.BlockSpec(memory_space=pltpu.SMEM)],
        out_specs=pl.BlockSpec((3, None, ATT_Q, C_KEY_TILES * GRID_W), lambda h: (0, h, 0, 0)),
        compiler_params=_params(1, 32),
        name="c_bias",
    )(rpb_flat)


def _attn_c_body(q_ref, kp_ref, kc_ref, kn_ref, vp_ref, vc_ref, vn_ref, bias_ref, o_ref):
    head_cols = lambda h: slice(h * HEAD_DIM, (h + 1) * HEAD_DIM)
    cat = lambda refs, h: jnp.concatenate([r[:, head_cols(h)] for r in refs], axis=0)

    def finish(h, o, m, l):
        o_ref[:, head_cols(h)] = o.astype(o_ref.dtype)

    _pipelined_attention(
        C_HEADS,
        scores=lambda h: _qk(q_ref[:, head_cols(h)], cat((kp_ref, kc_ref, kn_ref), h)) + bias_ref[h],
        values=lambda h: cat((vp_ref, vc_ref, vn_ref), h),
        finish=finish)


def _attn_c(qkv, bias):
    bsz, seq, _ = qkv.shape
    n_tiles = seq // ATT_Q
    block = lambda r: (None, r, C_W)
    k_specs = _halo_specs(block, lambda g, rb: (g[0], rb, 1), ATT_Q, seq)
    v_specs = _halo_specs(block, lambda g, rb: (g[0], rb, 2), ATT_Q, seq)
    return pl.pallas_call(
        _attn_c_body,
        out_shape=jax.ShapeDtypeStruct((bsz, seq, C_W), BF16),
        grid=(bsz, n_tiles),
        in_specs=[
            pl.BlockSpec(block(ATT_Q), lambda b, i: (b, i, 0)),
            *k_specs, *v_specs,
            pl.BlockSpec((None, C_HEADS, ATT_Q, C_KEY_TILES * GRID_W),
                         lambda b, i: (jnp.where(i == 0, 0, jnp.where(i == n_tiles - 1, 2, 1)), 0, 0, 0)),
        ],
        out_specs=pl.BlockSpec(block(ATT_Q), lambda b, i: (b, i, 0)),
        compiler_params=_params(2, 40),
        name="attn_c",
    )(qkv, qkv, qkv, qkv, qkv, qkv, qkv, bias)


def _merge_body(oa_ref, ob_ref, oc_ref, ga_ref, gb_ref, gc_ref, wa_ref, wb_ref, wc_ref, o_ref):
    m = ga_ref[...].astype(F32) * jnp.dot(oa_ref[...], wa_ref[...], preferred_element_type=F32)
    m = m + gb_ref[...].astype(F32) * jnp.dot(ob_ref[...], wb_ref[...], preferred_element_type=F32)
    m = m + gc_ref[...].astype(F32) * jnp.dot(oc_ref[...], wc_ref[...], preferred_element_type=F32)
    o_ref[...] = m.astype(o_ref.dtype)


def _merge(oa, ob, oc, gates, wa, wb, wc, layer):
    t = oa.shape[0]
    gate = lambda k: pl.BlockSpec((TM, TN), lambda i, j: (i, k * (D_MODEL // TN) + j))
    wspec = lambda rows: pl.BlockSpec((None, rows, TN), lambda i, j: (layer, 0, j))
    return pl.pallas_call(
        _merge_body,
        out_shape=jax.ShapeDtypeStruct((t, D_MODEL), BF16),
        grid=(t // TM, D_MODEL // TN),
        in_specs=[
            pl.BlockSpec((TM, A_Q_W), lambda i, j: (i, 0)),
            pl.BlockSpec((TM, B_OUT_W), lambda i, j: (i, 0)),
            pl.BlockSpec((TM, C_W), lambda i, j: (i, 0)),
            gate(0), gate(1), gate(2),
            wspec(A_Q_W), wspec(B_OUT_W), wspec(C_W),
        ],
        out_specs=pl.BlockSpec((TM, TN), lambda i, j: (i, j)),
        compiler_params=_params(2, 48),
        name="merge",
    )(oa, ob, oc, gates, gates, gates, wa, wb, wc)


def _residual_matmul_body(a_ref, w_ref, x_ref, o_ref):
    o_ref[...] = x_ref[...] + jnp.dot(a_ref[...], w_ref[...], preferred_element_type=F32)


def _residual_matmul(a, w_all, layer, x, tn, name):
    t, k = a.shape
    return pl.pallas_call(
        _residual_matmul_body,
        out_shape=jax.ShapeDtypeStruct((t, D_MODEL), F32),
        grid=(t // TM, D_MODEL // tn),
        in_specs=[
            pl.BlockSpec((TM, k), lambda i, j: (i, 0)),
            pl.BlockSpec((None, k, tn), lambda i, j: (layer, 0, j)),
            pl.BlockSpec((TM, tn), lambda i, j: (i, j)),
        ],
        out_specs=pl.BlockSpec((TM, tn), lambda i, j: (i, j)),
        compiler_params=_params(2, 48),
        name=name,
    )(a, w_all, x)


def _ffn_up_body(x_ref, g_ref, wg_ref, wu_ref, o_ref, h_ref):
    @pl.when(pl.program_id(1) == 0)
    def _():
        _rmsnorm_rows(x_ref, g_ref, h_ref)

    h = h_ref[...]
    gt = jnp.dot(h, wg_ref[...], preferred_element_type=F32)
    up = jnp.dot(h, wu_ref[...], preferred_element_type=F32)
    o_ref[...] = (gt * (1.0 / (1.0 + jnp.exp(-gt))) * up).astype(o_ref.dtype)


def _ffn_up(x, g, w_all, layer):
    t = x.shape[0]
    n_tiles = D_FF // TN
    return pl.pallas_call(
        _ffn_up_body,
        out_shape=jax.ShapeDtypeStruct((t, D_FF), BF16),
        grid=(t // TM, n_tiles),
        in_specs=[
            pl.BlockSpec((TM, D_MODEL), lambda i, j: (i, 0)),
            pl.BlockSpec((1, D_MODEL), lambda i, j: (0, 0)),
            pl.BlockSpec((None, D_MODEL, TN), lambda i, j: (layer, 0, j)),
            pl.BlockSpec((None, D_MODEL, TN), lambda i, j: (layer, 0, n_tiles + j)),
        ],
        out_specs=pl.BlockSpec((TM, TN), lambda i, j: (i, j)),
        scratch_shapes=[pltpu.VMEM((TM, D_MODEL), BF16)],
        compiler_params=_params(2, 48),
        name="ffn_up",
    )(x, g, w_all, w_all)


def _rope_tables(n):
    half = HEAD_DIM // 2
    inv_freq = ROPE_THETA ** (-jnp.arange(half, dtype=F32) * 2.0 / HEAD_DIM)
    ang = jnp.arange(n, dtype=F32)[:, None] * inv_freq[None, :]
    cos, sin = jnp.cos(ang), jnp.sin(ang)
    return jnp.concatenate([cos, cos], axis=-1), jnp.concatenate([-sin, sin], axis=-1)


def kernel(x, norm1_g, w_in, qk_norm_g, sink_a, rpb_c, w_br_a, w_br_b, w_br_c, w_o,
           norm2_g, w_gate_up, w_down):
    bsz, seq, d = x.shape
    assert d == D_MODEL and seq % TM == 0 and seq // ATT_Q >= 2
    assert seq // GRID_W >= C_WIN_ROWS and (seq // B_PATTERNS[-1][1]) % ATT_Q == 0
    t = bsz * seq
    cos, sin_signed = _rope_tables(seq)
    bias_a = jnp.asarray(_band_bias(A_RADIUS))
    bias_b = jnp.asarray(_band_bias(B_RADIUS))
    w_in, w_br_a, w_br_b, w_br_c, w_o, w_gate_up, w_down = (
        w.astype(BF16) for w in (w_in, w_br_a, w_br_b, w_br_c, w_o, w_gate_up, w_down))
    qkg = jnp.pad(qk_norm_g, ((0, 0), (0, 2), (0, 0)))
    tile = lambda off: off // TN
    xf = x.reshape(t, d)
    for layer in range(DEPTH):
        h = _norm1(xf, norm1_g[layer][None])
        proj = functools.partial(_proj, h, w_in, layer, cos=cos, sin_signed=sin_signed, qkg=qkg[layer],
                                 bsz=bsz, seq=seq)
        qkv_a = proj(range(tile(OFF_QA), tile(OFF_QB)), dil=1, name="proj_a")
        oa = _attn_a(qkv_a.reshape(bsz, seq, -1), sink_a[layer], bias_a)
        parts = []
        for group, (_, dil) in enumerate(B_PATTERNS):
            tiles = [tile(off) + group for off in (OFF_QB, OFF_KB, OFF_VB)]
            qkv_b = proj(tiles, dil=dil, name=f"proj_b{group}")
            if dil == 1:
                qkv_b = qkv_b.reshape(bsz, 1, seq, -1)
            parts.append(_attn_b_group(qkv_b, bias_b, group))
        ob = _b_combine(parts, seq)
        qkv_c = proj(range(tile(OFF_QC), tile(OFF_GATE)), dil=1, name="proj_c")
        oc = _attn_c(qkv_c.reshape(bsz, seq, -1), _c_bias(rpb_c[layer].reshape(-1)))
        gates = proj(range(tile(OFF_GATE), tile(N_IN)), dil=1, name="proj_g")
        merged = _merge(oa.reshape(t, A_Q_W), ob.reshape(t, B_OUT_W), oc.reshape(t, C_W), gates,
                        w_br_a, w_br_b, w_br_c, layer)
        xf = _residual_matmul(merged, w_o, layer, xf, TN, "out_proj")
        act = _ffn_up(xf, norm2_g[layer][None], w_gate_up, layer)
        xf = _residual_matmul(act, w_down, layer, xf, TN_DOWN, "ffn_down")
    return xf.reshape(bsz, seq, d)
```

```python
import functools
import math

import numpy as np
import jax
import jax.numpy as jnp
from jax import lax
from jax.experimental import pallas as pl
from jax.experimental.pallas import tpu as pltpu

F32 = jnp.float32
BF16 = jnp.bfloat16

D_MODEL = 2048
DEPTH = 4
HEAD_DIM = 128
ROPE_THETA = 10000.0
NORM_EPS = 1e-6
SCALE = HEAD_DIM ** -0.5
LOG2E = 1.4426950408889634
NEG = -1e30

A_Q_HEADS = 8
A_KV_HEADS = 2
A_GROUP = A_Q_HEADS // A_KV_HEADS
A_RADIUS = 128
B_PATTERNS = ((128, 1), (512, 4), (2048, 16))
B_GROUPS = len(B_PATTERNS)
B_HEADS_PER_GROUP = 4
B_RADIUS = 64
C_HEADS = 8
GRID_W = 64
C_WIN_ROWS = 8
C_WIN_COLS = 16

A_Q_W = A_Q_HEADS * HEAD_DIM
A_KV_W = A_KV_HEADS * HEAD_DIM
B_W = B_GROUPS * B_HEADS_PER_GROUP * HEAD_DIM
B_OUT_W = B_HEADS_PER_GROUP * HEAD_DIM
C_W = C_HEADS * HEAD_DIM
N_IN = A_Q_W + 2 * A_KV_W + 3 * B_W + 3 * C_W + 3 * D_MODEL
D_FF = ((8 * D_MODEL + 3 * 256 - 1) // (3 * 256)) * 256

OFF_QA = 0
OFF_KA = OFF_QA + A_Q_W
OFF_VA = OFF_KA + A_KV_W
OFF_QB = OFF_VA + A_KV_W
OFF_KB = OFF_QB + B_W
OFF_VB = OFF_KB + B_W
OFF_QC = OFF_VB + B_W
OFF_KC = OFF_QC + C_W
OFF_VC = OFF_KC + C_W
OFF_GATE = OFF_VC + C_W

TM = 1024
TN = 512
TN_WIDE = 1024
TN_DOWN = 256
TM_OUT = 512
TM_FFN = 2048
EPI_CHUNK = 64
NORM_CHUNK = 64
ATT_Q = 256
C_TILE_ROWS = ATT_Q // GRID_W
COMBINE_CHUNK = 128
MIB = 1024 * 1024


def _params(n_axes, vmem_mib):
    return pltpu.CompilerParams(
        dimension_semantics=("parallel",) * n_axes,
        vmem_limit_bytes=vmem_mib * MIB,
    )


def _rmsnorm_rows(x_ref, g_ref, h_ref):
    rows = x_ref.shape[0]

    def body(c, carry):
        r0 = pl.multiple_of(c * NORM_CHUNK, NORM_CHUNK)
        x = x_ref[pl.ds(r0, NORM_CHUNK), :]
        ms = jnp.mean(x * x, axis=-1, keepdims=True)
        h_ref[pl.ds(r0, NORM_CHUNK), :] = (x * lax.rsqrt(ms + NORM_EPS) * g_ref[...]).astype(h_ref.dtype)
        return carry

    lax.fori_loop(0, rows // NORM_CHUNK, body, 0)


def _pipelined_attention(n_units, scores, values, finish, extra_logit=None):
    s_next = scores(0)
    prev = None
    for u in range(n_units + 1):
        s = s_next
        if u + 1 < n_units:
            s_next = scores(u + 1)
        cur = None
        if u < n_units:
            m = jnp.max(s, axis=-1, keepdims=True)
            if extra_logit is not None:
                m = jnp.maximum(m, extra_logit(u))
            p = jnp.exp2(s - m)
            l = jnp.sum(p, axis=-1, keepdims=True)
            if extra_logit is not None:
                l = l + jnp.exp2(extra_logit(u) - m)
            cur = (p.astype(BF16), m, l)
        if prev is not None:
            p_prev, m_prev, l_prev = prev
            o = jnp.dot(p_prev, values(u - 1), preferred_element_type=F32) * (1.0 / l_prev)
            finish(u - 1, o, m_prev, l_prev)
        prev = cur


def _qk(q, k):
    return lax.dot_general(q, k, (((1,), (1,)), ((), ())), preferred_element_type=F32)


def _norm1_body(x_ref, g_ref, o_ref):
    _rmsnorm_rows(x_ref, g_ref, o_ref)


def _norm1(x, g):
    t = x.shape[0]
    return pl.pallas_call(
        _norm1_body,
        out_shape=jax.ShapeDtypeStruct((t, D_MODEL), BF16),
        grid=(t // TM,),
        in_specs=[pl.BlockSpec((TM, D_MODEL), lambda i: (i, 0)),
                  pl.BlockSpec((1, D_MODEL), lambda i: (0, 0))],
        out_specs=pl.BlockSpec((TM, D_MODEL), lambda i: (i, 0)),
        compiler_params=_params(1, 40),
        name="norm1",
    )(x, g)


def _head_kind(head):
    c = head * HEAD_DIM
    if c < OFF_KA:
        return ("norm_rope", 0, True)
    if c < OFF_VA:
        return ("norm_rope", 1, False)
    if c < OFF_QB:
        return ("copy",)
    if c < OFF_KB:
        return ("norm_rope", 2, True)
    if c < OFF_VB:
        return ("norm_rope", 3, False)
    if c < OFF_QC:
        return ("copy",)
    if c < OFF_KC:
        return ("norm", 4, True)
    if c < OFF_VC:
        return ("norm", 5, False)
    if c < OFF_GATE:
        return ("copy",)
    return ("sigmoid",)


def _tile_groups(w_tiles, heads):
    groups = {}
    for j, wt in enumerate(w_tiles):
        kinds = tuple(_head_kind(wt * heads + h) for h in range(heads))
        groups.setdefault(kinds, []).append(j)
    return list(groups.items())


def _in_tiles(j, tiles):
    runs = []
    for t in tiles:
        if runs and runs[-1][1] == t:
            runs[-1][1] = t + 1
        else:
            runs.append([t, t + 1])
    cond = None
    for lo, hi in runs:
        c = (j == lo) if hi == lo + 1 else ((j >= lo) & (j < hi))
        cond = c if cond is None else (cond | c)
    return cond


def _proj_epilogue(kinds, acc_ref, rot_ref, scl_ref, o_ref, cos_ref, sin_ref, qkg_ref, dil):
    rows_per_class = TM // dil
    blocks = []
    for c in range(dil):
        for m0 in range(0, rows_per_class, EPI_CHUNK):
            if dil == 1:
                blocks.append((pl.ds(m0, EPI_CHUNK), c, m0))
            else:
                blocks.append((pl.ds(c + dil * m0, EPI_CHUNK, stride=dil), c, m0))
    half = HEAD_DIM // 2

    def pass1(h, kind):
        if kind[0] not in ("norm_rope", "norm"):
            return
        for src, c, m0 in blocks:
            tmp = pl.ds(c * rows_per_class + m0, EPI_CHUNK)
            a = acc_ref.at[h][src, :]
            ms = jnp.mean(a * a, axis=-1, keepdims=True)
            scl_ref[h, tmp, :] = jnp.broadcast_to(lax.rsqrt(ms + NORM_EPS), a.shape)
            if kind[0] == "norm_rope":
                rot_ref[h, tmp, :] = pltpu.roll(a, half, 1)

    def pass2():
        gains = {}
        for kind in kinds:
            if kind[0] in ("norm_rope", "norm") and kind[1:] not in gains:
                g = qkg_ref[kind[1]:kind[1] + 1, :]
                if kind[2]:
                    g = g * (SCALE * LOG2E)
                gains[kind[1:]] = (g, pltpu.roll(g, half, 1))
        for src, c, m0 in blocks:
            tmp = pl.ds(c * rows_per_class + m0, EPI_CHUNK)
            tables = {}
            for h, kind in enumerate(kinds):
                cols = slice(h * HEAD_DIM, (h + 1) * HEAD_DIM)
                a = acc_ref.at[h][src, :]
                if kind[0] == "norm_rope":
                    if kind[1:] not in tables:
                        g, g_rot = gains[kind[1:]]
                        tables[kind[1:]] = (g * cos_ref[src, :], g_rot * sin_ref[src, :])
                    t_cos, t_sin = tables[kind[1:]]
                    y = scl_ref[h, tmp, :] * (a * t_cos + rot_ref[h, tmp, :] * t_sin)
                elif kind[0] == "norm":
                    y = a * scl_ref[h, tmp, :] * gains[kind[1:]][0]
                elif kind[0] == "sigmoid":
                    y = 1.0 / (1.0 + jnp.exp(-a))
                else:
                    y = a
                if dil == 1:
                    o_ref[pl.ds(m0, EPI_CHUNK), cols] = y.astype(o_ref.dtype)
                else:
                    o_ref[c, pl.ds(m0, EPI_CHUNK), cols] = y.astype(o_ref.dtype)

    for h, kind in enumerate(kinds):
        pass1(h, kind)
    pass2()


def _proj_body(h_ref, w_ref, cos_ref, sin_ref, qkg_ref, o_ref, acc_ref, *norm_scratch, n_cols, groups, dil):
    rot_ref, scl_ref = norm_scratch if norm_scratch else (None, None)
    s = pl.program_id(0)
    je = jnp.maximum(s - 1, 0) % n_cols
    parity = s % 2

    @pl.when(s == 0)
    def _():
        acc_ref[1] = jnp.zeros(acc_ref.shape[1:], F32)

    def step(kinds, p):
        _proj_epilogue(kinds, acc_ref.at[1 - p], rot_ref, scl_ref, o_ref, cos_ref, sin_ref, qkg_ref, dil)
        res = jnp.dot(h_ref[...], w_ref[...], preferred_element_type=F32)
        for h in range(len(kinds)):
            acc_ref[p, h] = res[:, h * HEAD_DIM:(h + 1) * HEAD_DIM]

    for kinds, tiles in groups:
        for p in range(2):
            cond = parity == p
            if len(groups) > 1:
                cond = cond & _in_tiles(je, tiles)
            pl.when(cond)(functools.partial(step, kinds, p))


def _proj(h, w_all, layer, w_tiles, cos, sin_signed, qkg, bsz, seq, dil, name, tn=TN):
    t = h.shape[0]
    heads = tn // HEAD_DIM
    n_cols = len(w_tiles)
    w_base = w_tiles[0]
    w_step = w_tiles[1] - w_tiles[0] if n_cols > 1 else 0
    assert list(w_tiles) == [w_base + w_step * j for j in range(n_cols)]
    pos_tiles = seq // TM
    n_tiles = (t // TM) * n_cols
    groups = _tile_groups(w_tiles, heads)
    needs_norm = any(k[0] in ("norm_rope", "norm") for kinds, _ in groups for k in kinds)
    head_buf = pltpu.VMEM((heads, TM, HEAD_DIM), F32)

    dot_tile = lambda s: jnp.minimum(s, n_tiles - 1)
    epi_tile = lambda s: jnp.maximum(s - 1, 0)
    if dil == 1:
        out_shape = jax.ShapeDtypeStruct((t, n_cols * tn), BF16)
        out_spec = pl.BlockSpec((TM, tn), lambda s: (epi_tile(s) // n_cols, epi_tile(s) % n_cols))
    else:
        out_shape = jax.ShapeDtypeStruct((bsz, dil, seq // dil, n_cols * tn), BF16)
        out_spec = pl.BlockSpec(
            (None, dil, TM // dil, tn),
            lambda s: ((epi_tile(s) // n_cols) // pos_tiles, 0, (epi_tile(s) // n_cols) % pos_tiles,
                       epi_tile(s) % n_cols))
    table_spec = pl.BlockSpec((TM, HEAD_DIM), lambda s: ((epi_tile(s) // n_cols) % pos_tiles, 0))
    return pl.pallas_call(
        functools.partial(_proj_body, n_cols=n_cols, groups=groups, dil=dil),
        out_shape=out_shape,
        grid=(n_tiles + 1,),
        in_specs=[
            pl.BlockSpec((TM, D_MODEL), lambda s: (dot_tile(s) // n_cols, 0)),
            pl.BlockSpec((None, D_MODEL, tn), lambda s: (layer, 0, w_base + w_step * (dot_tile(s) % n_cols))),
            table_spec, table_spec,
            pl.BlockSpec((8, HEAD_DIM), lambda s: (0, 0)),
        ],
        out_specs=out_spec,
        scratch_shapes=[pltpu.VMEM((2, heads, TM, HEAD_DIM), F32)] + ([head_buf, head_buf] if needs_norm else []),
        compiler_params=pltpu.CompilerParams(dimension_semantics=("arbitrary",), vmem_limit_bytes=40 * MIB),
        name=name,
    )(h, w_all, cos, sin_signed, qkg)


def _band_bias(radius):
    r = np.arange(ATT_Q)[:, None]
    c = np.arange(ATT_Q + 2 * radius)[None, :]
    band = np.abs(c - radius - r) <= radius
    lo_ok = c >= radius
    hi_ok = c < ATT_Q + radius
    variants = [band & lo_ok, band, band & hi_ok, band & lo_ok & hi_ok]
    return np.stack([np.where(v, 0.0, NEG) for v in variants]).astype(np.float32)


def _band_variant(i, n_tiles):
    if n_tiles == 1:
        return 3
    return jnp.where(i == 0, 0, jnp.where(i == n_tiles - 1, 2, 1))


def _halo_specs(block, index, radius, n_rows):
    per = ATT_Q // radius
    last = n_rows // radius - 1
    return [
        pl.BlockSpec(block(radius), lambda *g: index(g, jnp.maximum(g[-1] * per - 1, 0))),
        pl.BlockSpec(block(ATT_Q), lambda *g: index(g, g[-1])),
        pl.BlockSpec(block(radius), lambda *g: index(g, jnp.minimum(g[-1] * per + per, last))),
    ]


def _attn_a_body(sink_ref, q_ref, kp_ref, kc_ref, kn_ref, vp_ref, vc_ref, vn_ref, bias_ref, o_ref):
    head_cols = lambda h: slice(h * HEAD_DIM, (h + 1) * HEAD_DIM)
    kcat = [jnp.concatenate([r[:, head_cols(kv)] for r in (kp_ref, kc_ref, kn_ref)], axis=0)
            for kv in range(A_KV_HEADS)]
    vcat = [jnp.concatenate([r[:, head_cols(kv)] for r in (vp_ref, vc_ref, vn_ref)], axis=0)
            for kv in range(A_KV_HEADS)]

    def finish(h, o, m, l):
        o_ref[:, head_cols(h)] = o.astype(o_ref.dtype)

    _pipelined_attention(
        A_Q_HEADS,
        scores=lambda h: _qk(q_ref[:, head_cols(h)], kcat[h // A_GROUP]) + bias_ref[...],
        values=lambda h: vcat[h // A_GROUP],
        finish=finish,
        extra_logit=lambda h: sink_ref[h] * LOG2E)


def _attn_a(qkv, sink, bias):
    bsz, seq, _ = qkv.shape
    n_tiles = seq // ATT_Q
    block = lambda r: (None, r, A_KV_W)
    k_specs = _halo_specs(block, lambda g, rb: (g[0], rb, OFF_KA // A_KV_W), A_RADIUS, seq)
    v_specs = _halo_specs(block, lambda g, rb: (g[0], rb, OFF_VA // A_KV_W), A_RADIUS, seq)
    return pl.pallas_call(
        _attn_a_body,
        out_shape=jax.ShapeDtypeStruct((bsz, seq, A_Q_W), BF16),
        grid=(bsz, n_tiles),
        in_specs=[
            pl.BlockSpec(memory_space=pltpu.SMEM),
            pl.BlockSpec((None, ATT_Q, A_Q_W), lambda b, i: (b, i, 0)),
            *k_specs, *v_specs,
            pl.BlockSpec((None, ATT_Q, ATT_Q + 2 * A_RADIUS), lambda b, i: (_band_variant(i, n_tiles), 0, 0)),
        ],
        out_specs=pl.BlockSpec((None, ATT_Q, A_Q_W), lambda b, i: (b, i, 0)),
        compiler_params=_params(2, 32),
        name="attn_a",
    )(sink, qkv, qkv, qkv, qkv, qkv, qkv, qkv, bias)


def _attn_b_body(q_ref, kp_ref, kc_ref, kn_ref, vp_ref, vc_ref, vn_ref, bias_ref, o_ref, lse_ref):
    head_cols = lambda h: slice(h * HEAD_DIM, (h + 1) * HEAD_DIM)
    cat = lambda refs, h: jnp.concatenate([r[:, head_cols(h)] for r in refs], axis=0)

    def finish(h, o, m, l):
        o_ref[:, head_cols(h)] = o.astype(o_ref.dtype)
        lse_ref[:, head_cols(h)] = jnp.broadcast_to(m + jnp.log2(l), o.shape)

    _pipelined_attention(
        B_HEADS_PER_GROUP,
        scores=lambda h: _qk(q_ref[:, head_cols(h)], cat((kp_ref, kc_ref, kn_ref), h)) + bias_ref[...],
        values=lambda h: cat((vp_ref, vc_ref, vn_ref), h),
        finish=finish)


def _attn_b_group(qkv, bias, group):
    bsz, dil, m, _ = qkv.shape
    n_tiles = m // ATT_Q
    block = lambda r: (None, None, r, B_OUT_W)
    k_specs = _halo_specs(block, lambda g, rb: (g[0], g[1], rb, 1), B_RADIUS, m)
    v_specs = _halo_specs(block, lambda g, rb: (g[0], g[1], rb, 2), B_RADIUS, m)
    io_spec = pl.BlockSpec(block(ATT_Q), lambda b, c, i: (b, c, i, 0))
    return pl.pallas_call(
        _attn_b_body,
        out_shape=[jax.ShapeDtypeStruct((bsz, dil, m, B_OUT_W), BF16),
                   jax.ShapeDtypeStruct((bsz, dil, m, B_OUT_W), F32)],
        grid=(bsz, dil, n_tiles),
        in_specs=[io_spec, *k_specs, *v_specs,
                  pl.BlockSpec((None, ATT_Q, ATT_Q + 2 * B_RADIUS),
                               lambda b, c, i: (_band_variant(i, n_tiles), 0, 0))],
        out_specs=[io_spec, io_spec],
        compiler_params=_params(3, 32),
        name=f"attn_b{group}",
    )(qkv, qkv, qkv, qkv, qkv, qkv, qkv, bias)


def _b_combine_body(*refs):
    in_refs = refs[:2 * B_GROUPS]
    out_ref, nat_o, nat_l = refs[2 * B_GROUPS:]
    o0_ref, l0_ref = in_refs[0], in_refs[1]
    for gi in range(1, B_GROUPS):
        o_ref, l_ref = in_refs[2 * gi], in_refs[2 * gi + 1]
        dil = B_PATTERNS[gi][1]
        for c in range(dil):
            dst = pl.ds(c, TM // dil, stride=dil)
            for h in range(B_HEADS_PER_GROUP):
                cols = slice(h * HEAD_DIM, (h + 1) * HEAD_DIM)
                nat_o.at[gi - 1, h][dst, :] = o_ref[c, :, cols].astype(F32)
                nat_l.at[gi - 1, h][dst, :] = l_ref[c, :, cols]

    def body(ci, carry):
        rows = pl.ds(pl.multiple_of(ci * COMBINE_CHUNK, COMBINE_CHUNK), COMBINE_CHUNK)
        for h in range(B_HEADS_PER_GROUP):
            cols = slice(h * HEAD_DIM, (h + 1) * HEAD_DIM)
            lses = [l0_ref[0, rows, cols]] + [nat_l[gi, h, rows, :] for gi in range(B_GROUPS - 1)]
            outs = [o0_ref[0, rows, cols].astype(F32)] + [nat_o[gi, h, rows, :] for gi in range(B_GROUPS - 1)]
            mx = functools.reduce(jnp.maximum, lses)
            ws = [jnp.exp2(l - mx) for l in lses]
            num = functools.reduce(lambda a, b: a + b, [w * o for w, o in zip(ws, outs)])
            den = functools.reduce(lambda a, b: a + b, ws)
            out_ref[rows, cols] = (num * (1.0 / den)).astype(out_ref.dtype)
        return carry

    lax.fori_loop(0, TM // COMBINE_CHUNK, body, 0)


def _b_combine(parts, seq):
    assert B_PATTERNS[0][1] == 1
    bsz = parts[0][0].shape[0]
    in_specs, args = [], []
    for (o, lse), (_, dil) in zip(parts, B_PATTERNS):
        spec = pl.BlockSpec((None, dil, TM // dil, B_OUT_W), lambda b, i: (b, 0, i, 0))
        in_specs += [spec, spec]
        args += [o, lse]
    nat_shape = (B_GROUPS - 1, B_HEADS_PER_GROUP, TM, HEAD_DIM)
    return pl.pallas_call(
        _b_combine_body,
        out_shape=jax.ShapeDtypeStruct((bsz, seq, B_OUT_W), BF16),
        grid=(bsz, seq // TM),
        in_specs=in_specs,
        out_specs=pl.BlockSpec((None, TM, B_OUT_W), lambda b, i: (b, i, 0)),
        scratch_shapes=[pltpu.VMEM(nat_shape, F32), pltpu.VMEM(nat_shape, F32)],
        compiler_params=_params(2, 40),
        name="b_combine",
    )(*args)


N_DROW = 2 * C_WIN_ROWS - 1
N_DCOL = 2 * C_WIN_COLS - 1
C_KEY_TILES = 3 * C_TILE_ROWS


def _c_bias_body(rpb_ref, o_ref):
    h = pl.program_id(0)
    qc = lax.broadcasted_iota(jnp.int32, (GRID_W, GRID_W), 0)
    kc = lax.broadcasted_iota(jnp.int32, (GRID_W, GRID_W), 1)
    dcol = jnp.clip(kc - qc, -(C_WIN_COLS - 1), C_WIN_COLS - 1) + (C_WIN_COLS - 1)
    col_start = jnp.clip(qc - C_WIN_COLS // 2, 0, GRID_W - C_WIN_COLS)
    col_ok = (kc >= col_start) & (kc < col_start + C_WIN_COLS)
    tiles = []
    for dr in range(N_DROW):
        t = jnp.zeros((GRID_W, GRID_W), F32)
        for d in range(N_DCOL):
            t = jnp.where(dcol == d, rpb_ref[(h * N_DROW + dr) * N_DCOL + d] * LOG2E, t)
        tiles.append(jnp.where(col_ok, t, NEG))
    masked = jnp.full((GRID_W, GRID_W), NEG, F32)
    half = C_WIN_ROWS // 2
    for var in range(3):
        for qi in range(C_TILE_ROWS):
            row = []
            for kj in range(C_KEY_TILES):
                if var == 0:
                    ok = C_TILE_ROWS <= kj < C_TILE_ROWS + C_WIN_ROWS
                elif var == 2:
                    ok = 2 * C_TILE_ROWS - C_WIN_ROWS <= kj < 2 * C_TILE_ROWS
                else:
                    ok = 0 <= kj - C_TILE_ROWS - qi + half < C_WIN_ROWS
                drow = kj - C_TILE_ROWS - qi
                row.append(tiles[drow + C_WIN_ROWS - 1] if ok else masked)
            o_ref[var, qi * GRID_W:(qi + 1) * GRID_W, :] = jnp.concatenate(row, axis=1)


def _c_bias(rpb_flat):
    return pl.pallas_call(
        _c_bias_body,
        out_shape=jax.ShapeDtypeStruct((3, C_HEADS, ATT_Q, C_KEY_TILES * GRID_W), F32),
        grid=(C_HEADS,),
        in_specs=[pl.BlockSpec(memory_space=pltpu.SMEM)],
        out_specs=pl.BlockSpec((3, None, ATT_Q, C_KEY_TILES * GRID_W), lambda h: (0, h, 0, 0)),
        compiler_params=_params(1, 32),
        name="c_bias",
    )(rpb_flat)


def _attn_c_body(q_ref, kp_ref, kc_ref, kn_ref, vp_ref, vc_ref, vn_ref, bias_ref, o_ref):
    head_cols = lambda h: slice(h * HEAD_DIM, (h + 1) * HEAD_DIM)
    cat = lambda refs, h: jnp.concatenate([r[:, head_cols(h)] for r in refs], axis=0)

    def finish(h, o, m, l):
        o_ref[:, head_cols(h)] = o.astype(o_ref.dtype)

    _pipelined_attention(
        C_HEADS,
        scores=lambda h: _qk(q_ref[:, head_cols(h)], cat((kp_ref, kc_ref, kn_ref), h)) + bias_ref[h],
        values=lambda h: cat((vp_ref, vc_ref, vn_ref), h),
        finish=finish)


def _attn_c(qkv, bias):
    bsz, seq, _ = qkv.shape
    n_tiles = seq // ATT_Q
    block = lambda r: (None, r, C_W)
    k_specs = _halo_specs(block, lambda g, rb: (g[0], rb, 1), ATT_Q, seq)
    v_specs = _halo_specs(block, lambda g, rb: (g[0], rb, 2), ATT_Q, seq)
    return pl.pallas_call(
        _attn_c_body,
        out_shape=jax.ShapeDtypeStruct((bsz, seq, C_W), BF16),
        grid=(bsz, n_tiles),
        in_specs=[
            pl.BlockSpec(block(ATT_Q), lambda b, i: (b, i, 0)),
            *k_specs, *v_specs,
            pl.BlockSpec((None, C_HEADS, ATT_Q, C_KEY_TILES * GRID_W),
                         lambda b, i: (jnp.where(i == 0, 0, jnp.where(i == n_tiles - 1, 2, 1)), 0, 0, 0)),
        ],
        out_specs=pl.BlockSpec(block(ATT_Q), lambda b, i: (b, i, 0)),
        compiler_params=_params(2, 40),
        name="attn_c",
    )(qkv, qkv, qkv, qkv, qkv, qkv, qkv, bias)


def _merge_body(oa_ref, ob_ref, oc_ref, ga_ref, gb_ref, gc_ref, wa_ref, wb_ref, wc_ref, o_ref):
    m = ga_ref[...].astype(F32) * jnp.dot(oa_ref[...], wa_ref[...], preferred_element_type=F32)
    m = m + gb_ref[...].astype(F32) * jnp.dot(ob_ref[...], wb_ref[...], preferred_element_type=F32)
    m = m + gc_ref[...].astype(F32) * jnp.dot(oc_ref[...], wc_ref[...], preferred_element_type=F32)
    o_ref[...] = m.astype(o_ref.dtype)


def _merge(oa, ob, oc, gates, wa, wb, wc, layer):
    t = oa.shape[0]
    gate = lambda k: pl.BlockSpec((TM, TN), lambda i, j: (i, k * (D_MODEL // TN) + j))
    wspec = lambda rows: pl.BlockSpec((None, rows, TN), lambda i, j: (layer, 0, j))
    return pl.pallas_call(
        _merge_body,
        out_shape=jax.ShapeDtypeStruct((t, D_MODEL), BF16),
        grid=(t // TM, D_MODEL // TN),
        in_specs=[
            pl.BlockSpec((TM, A_Q_W), lambda i, j: (i, 0)),
            pl.BlockSpec((TM, B_OUT_W), lambda i, j: (i, 0)),
            pl.BlockSpec((TM, C_W), lambda i, j: (i, 0)),
            gate(0), gate(1), gate(2),
            wspec(A_Q_W), wspec(B_OUT_W), wspec(C_W),
        ],
        out_specs=pl.BlockSpec((TM, TN), lambda i, j: (i, j)),
        compiler_params=_params(2, 48),
        name="merge",
    )(oa, ob, oc, gates, gates, gates, wa, wb, wc)


def _residual_matmul_body(a_ref, w_ref, x_ref, o_ref):
    o_ref[...] = x_ref[...] + jnp.dot(a_ref[...], w_ref[...], preferred_element_type=F32)


def _residual_matmul(a, w_all, layer, x, tn, name):
    t, k = a.shape
    return pl.pallas_call(
        _residual_matmul_body,
        out_shape=jax.ShapeDtypeStruct((t, D_MODEL), F32),
        grid=(t // TM, D_MODEL // tn),
        in_specs=[
            pl.BlockSpec((TM, k), lambda i, j: (i, 0)),
            pl.BlockSpec((None, k, tn), lambda i, j: (layer, 0, j)),
            pl.BlockSpec((TM, tn), lambda i, j: (i, j)),
        ],
        out_specs=pl.BlockSpec((TM, tn), lambda i, j: (i, j)),
        compiler_params=_params(2, 48),
        name=name,
    )(a, w_all, x)


def _out_proj_body(m_ref, w_ref, x_ref, g_ref, xo_ref, h_ref, y_ref, rs_ref):
    s = pl.program_id(0)
    parity = s % 2

    @pl.when(s == 0)
    def _():
        y_ref[1] = jnp.zeros(y_ref.shape[1:], F32)

    def step(p):
        prev = y_ref.at[1 - p]
        chunks = [pl.ds(r0, NORM_CHUNK) for r0 in range(0, TM_OUT, NORM_CHUNK)]
        for rows in chunks:
            y = prev[rows, :]
            ms = jnp.mean(y * y, axis=-1, keepdims=True)
            rs_ref[rows, :] = jnp.broadcast_to(lax.rsqrt(ms + NORM_EPS), (NORM_CHUNK, HEAD_DIM))
        for rows in chunks:
            rs = pltpu.repeat(rs_ref[rows, :], D_MODEL // HEAD_DIM, axis=1)
            h_ref[rows, :] = (prev[rows, :] * rs * g_ref[...]).astype(h_ref.dtype)
        y = x_ref[...] + jnp.dot(m_ref[...], w_ref[...], preferred_element_type=F32)
        xo_ref[...] = y
        y_ref[p] = y

    for p in range(2):
        pl.when(parity == p)(functools.partial(step, p))


def _out_proj(merged, w_all, layer, x, g):
    t = x.shape[0]
    n_tiles = t // TM_OUT
    cur = lambda s: (jnp.minimum(s, n_tiles - 1), 0)
    prev = lambda s: (jnp.maximum(s - 1, 0), 0)
    return pl.pallas_call(
        _out_proj_body,
        out_shape=[jax.ShapeDtypeStruct((t, D_MODEL), F32), jax.ShapeDtypeStruct((t, D_MODEL), BF16)],
        grid=(n_tiles + 1,),
        in_specs=[
            pl.BlockSpec((TM_OUT, D_MODEL), cur),
            pl.BlockSpec((None, D_MODEL, D_MODEL), lambda s: (layer, 0, 0), pipeline_mode=pl.Buffered(1)),
            pl.BlockSpec((TM_OUT, D_MODEL), cur),
            pl.BlockSpec((1, D_MODEL), lambda s: (0, 0)),
        ],
        out_specs=[pl.BlockSpec((TM_OUT, D_MODEL), cur), pl.BlockSpec((TM_OUT, D_MODEL), prev)],
        scratch_shapes=[pltpu.VMEM((2, TM_OUT, D_MODEL), F32), pltpu.VMEM((TM_OUT, HEAD_DIM), F32)],
        compiler_params=pltpu.CompilerParams(dimension_semantics=("arbitrary",), vmem_limit_bytes=52 * MIB),
        name="out_proj",
    )(merged, w_all, x, g)


def _ffn_up_body(h_ref, wg_ref, wu_ref, o_ref):
    h = h_ref[...]
    gt = jnp.dot(h, wg_ref[...], preferred_element_type=F32)
    up = jnp.dot(h, wu_ref[...], preferred_element_type=F32)
    o_ref[...] = (gt * (1.0 / (1.0 + jnp.exp(-gt))) * up).astype(o_ref.dtype)


def _ffn_up(h, w_all, layer):
    t = h.shape[0]
    n_tiles = D_FF // TN
    return pl.pallas_call(
        _ffn_up_body,
        out_shape=jax.ShapeDtypeStruct((t, D_FF), BF16),
        grid=(t // TM_FFN, n_tiles),
        in_specs=[
            pl.BlockSpec((TM_FFN, D_MODEL), lambda i, j: (i, 0)),
            pl.BlockSpec((None, D_MODEL, TN), lambda i, j: (layer, 0, j)),
            pl.BlockSpec((None, D_MODEL, TN), lambda i, j: (layer, 0, n_tiles + j)),
        ],
        out_specs=pl.BlockSpec((TM_FFN, TN), lambda i, j: (i, j)),
        compiler_params=_params(2, 52),
        name="ffn_up",
    )(h, w_all, w_all)


def _rope_tables(n):
    half = HEAD_DIM // 2
    inv_freq = ROPE_THETA ** (-jnp.arange(half, dtype=F32) * 2.0 / HEAD_DIM)
    ang = jnp.arange(n, dtype=F32)[:, None] * inv_freq[None, :]
    cos, sin = jnp.cos(ang), jnp.sin(ang)
    return jnp.concatenate([cos, cos], axis=-1), jnp.concatenate([-sin, sin], axis=-1)


def kernel(x, norm1_g, w_in, qk_norm_g, sink_a, rpb_c, w_br_a, w_br_b, w_br_c, w_o,
           norm2_g, w_gate_up, w_down):
    bsz, seq, d = x.shape
    assert d == D_MODEL and seq % TM == 0 and seq // ATT_Q >= 2 and (bsz * seq) % TM_FFN == 0
    assert seq // GRID_W >= C_WIN_ROWS and (seq // B_PATTERNS[-1][1]) % ATT_Q == 0
    t = bsz * seq
    cos, sin_signed = _rope_tables(seq)
    bias_a = jnp.asarray(_band_bias(A_RADIUS))
    bias_b = jnp.asarray(_band_bias(B_RADIUS))
    w_in, w_br_a, w_br_b, w_br_c, w_o, w_gate_up, w_down = (
        w.astype(BF16) for w in (w_in, w_br_a, w_br_b, w_br_c, w_o, w_gate_up, w_down))
    qkg = jnp.pad(qk_norm_g, ((0, 0), (0, 2), (0, 0)))
    tile = lambda off, tn=TN: off // tn
    wide = lambda width: TN_WIDE if width % TN_WIDE == 0 else TN
    xf = x.reshape(t, d)
    for layer in range(DEPTH):
        h = _norm1(xf, norm1_g[layer][None])
        proj = functools.partial(_proj, h, w_in, layer, cos=cos, sin_signed=sin_signed, qkg=qkg[layer],
                                 bsz=bsz, seq=seq)
        qkv_a = proj(range(tile(OFF_QA), tile(OFF_QB)), dil=1, name="proj_a")
        oa = _attn_a(qkv_a.reshape(bsz, seq, -1), sink_a[layer], bias_a)
        parts = []
        for group, (_, dil) in enumerate(B_PATTERNS):
            tiles = [tile(off) + group for off in (OFF_QB, OFF_KB, OFF_VB)]
            qkv_b = proj(tiles, dil=dil, name=f"proj_b{group}")
            if dil == 1:
                qkv_b = qkv_b.reshape(bsz, 1, seq, -1)
            parts.append(_attn_b_group(qkv_b, bias_b, group))
        ob = _b_combine(parts, seq)
        tn_c = wide(C_W)
        qkv_c = proj(range(tile(OFF_QC, tn_c), tile(OFF_GATE, tn_c)), dil=1, name="proj_c", tn=tn_c)
        oc = _attn_c(qkv_c.reshape(bsz, seq, -1), _c_bias(rpb_c[layer].reshape(-1)))
        tn_g = wide(math.gcd(D_MODEL, OFF_GATE))
        gates = proj(range(tile(OFF_GATE, tn_g), tile(N_IN, tn_g)), dil=1, name="proj_g", tn=tn_g)
        merged = _merge(oa.reshape(t, A_Q_W), ob.reshape(t, B_OUT_W), oc.reshape(t, C_W), gates,
                        w_br_a, w_br_b, w_br_c, layer)
        xf, h2 = _out_proj(merged, w_o, layer, xf, norm2_g[layer][None])
        act = _ffn_up(h2, w_gate_up, layer)
        xf = _residual_matmul(act, w_down, layer, xf, TN_DOWN, "ffn_down")
    return xf.reshape(bsz, seq, d)
```

```python
import functools
import math

import numpy as np
import jax
import jax.numpy as jnp
from jax import lax
from jax.experimental import pallas as pl
from jax.experimental.pallas import tpu as pltpu

F32 = jnp.float32
BF16 = jnp.bfloat16

D_MODEL = 2048
DEPTH = 4
HEAD_DIM = 128
SUBLANES = 8
ROPE_THETA = 10000.0
NORM_EPS = 1e-6
SCALE = HEAD_DIM ** -0.5
LOG2E = 1.4426950408889634
NEG = -1e30

A_Q_HEADS = 8
A_KV_HEADS = 2
A_GROUP = A_Q_HEADS // A_KV_HEADS
A_RADIUS = 128
B_PATTERNS = ((128, 1), (512, 4), (2048, 16))
B_GROUPS = len(B_PATTERNS)
B_HEADS_PER_GROUP = 4
B_RADIUS = 64
C_HEADS = 8
GRID_W = 64
C_WIN_ROWS = 8
C_WIN_COLS = 16

A_Q_W = A_Q_HEADS * HEAD_DIM
A_KV_W = A_KV_HEADS * HEAD_DIM
B_W = B_GROUPS * B_HEADS_PER_GROUP * HEAD_DIM
B_OUT_W = B_HEADS_PER_GROUP * HEAD_DIM
C_W = C_HEADS * HEAD_DIM
N_IN = A_Q_W + 2 * A_KV_W + 3 * B_W + 3 * C_W + 3 * D_MODEL
D_FF = ((8 * D_MODEL + 3 * 256 - 1) // (3 * 256)) * 256

OFF_QA = 0
OFF_KA = OFF_QA + A_Q_W
OFF_VA = OFF_KA + A_KV_W
OFF_QB = OFF_VA + A_KV_W
OFF_KB = OFF_QB + B_W
OFF_VB = OFF_KB + B_W
OFF_QC = OFF_VB + B_W
OFF_KC = OFF_QC + C_W
OFF_VC = OFF_KC + C_W
OFF_GATE = OFF_VC + C_W

TM = 1024
TN = 512
TN_WIDE = 1024
TM_OUT = 512
TM_DOWN = 256
TM_FFN = 2048
EPI_CHUNK = 64
NORM_CHUNK = 64
ATT_Q = 256
B_CLASS_BLOCK = 2
C_TILE_ROWS = ATT_Q // GRID_W
COMBINE_CHUNK = 128
MIB = 1024 * 1024


def _params(n_axes, vmem_mib):
    return pltpu.CompilerParams(
        dimension_semantics=("parallel",) * n_axes,
        vmem_limit_bytes=vmem_mib * MIB,
    )


def _rmsnorm_rows(x_ref, g_ref, h_ref):
    rows = x_ref.shape[0]

    def body(c, carry):
        r0 = pl.multiple_of(c * NORM_CHUNK, NORM_CHUNK)
        x = x_ref[pl.ds(r0, NORM_CHUNK), :]
        ms = jnp.mean(x * x, axis=-1, keepdims=True)
        h_ref[pl.ds(r0, NORM_CHUNK), :] = (x * lax.rsqrt(ms + NORM_EPS) * g_ref[...]).astype(h_ref.dtype)
        return carry

    lax.fori_loop(0, rows // NORM_CHUNK, body, 0)


def _pipelined_attention(n_units, scores, values, finish, extra_logit=None):
    s_next = scores(0)
    prev = None
    for u in range(n_units + 1):
        s = s_next
        if u + 1 < n_units:
            s_next = scores(u + 1)
        cur = None
        if u < n_units:
            m = jnp.max(s, axis=-1, keepdims=True)
            if extra_logit is not None:
                m = jnp.maximum(m, extra_logit(u))
            p = jnp.exp2(s - m)
            l = jnp.sum(p, axis=-1, keepdims=True)
            if extra_logit is not None:
                l = l + jnp.exp2(extra_logit(u) - m)
            cur = (p.astype(BF16), m, l)
        if prev is not None:
            p_prev, m_prev, l_prev = prev
            o = jnp.dot(p_prev, values(u - 1), preferred_element_type=F32) * (1.0 / l_prev)
            finish(u - 1, o, m_prev, l_prev)
        prev = cur


def _qk(q, k):
    return lax.dot_general(q, k, (((1,), (1,)), ((), ())), preferred_element_type=F32)


def _norm1_body(x_ref, g_ref, o_ref):
    _rmsnorm_rows(x_ref, g_ref, o_ref)


def _norm1(x, g):
    t = x.shape[0]
    return pl.pallas_call(
        _norm1_body,
        out_shape=jax.ShapeDtypeStruct((t, D_MODEL), BF16),
        grid=(t // TM,),
        in_specs=[pl.BlockSpec((TM, D_MODEL), lambda i: (i, 0)),
                  pl.BlockSpec((1, D_MODEL), lambda i: (0, 0))],
        out_specs=pl.BlockSpec((TM, D_MODEL), lambda i: (i, 0)),
        compiler_params=_params(1, 40),
        name="norm1",
    )(x, g)


def _head_kind(head):
    c = head * HEAD_DIM
    if c < OFF_KA:
        return ("norm_rope", 0, True)
    if c < OFF_VA:
        return ("norm_rope", 1, False)
    if c < OFF_QB:
        return ("copy",)
    if c < OFF_KB:
        return ("norm_rope", 2, True)
    if c < OFF_VB:
        return ("norm_rope", 3, False)
    if c < OFF_QC:
        return ("copy",)
    if c < OFF_KC:
        return ("norm", 4, True)
    if c < OFF_VC:
        return ("norm", 5, False)
    if c < OFF_GATE:
        return ("copy",)
    return ("sigmoid",)


def _tile_groups(w_tiles, heads):
    groups = {}
    for j, wt in enumerate(w_tiles):
        kinds = tuple(_head_kind(wt * heads + h) for h in range(heads))
        groups.setdefault(kinds, []).append(j)
    return list(groups.items())


def _in_tiles(j, tiles):
    runs = []
    for t in tiles:
        if runs and runs[-1][1] == t:
            runs[-1][1] = t + 1
        else:
            runs.append([t, t + 1])
    cond = None
    for lo, hi in runs:
        c = (j == lo) if hi == lo + 1 else ((j >= lo) & (j < hi))
        cond = c if cond is None else (cond | c)
    return cond


def _proj_epilogue(kinds, acc_ref, rot_ref, scl_ref, o_ref, cos_ref, sin_ref, qkg_ref, dil, pitch):
    rows_per_class = TM // dil
    blocks = []
    for c in range(dil):
        for m0 in range(0, rows_per_class, EPI_CHUNK):
            if dil == 1:
                blocks.append((pl.ds(m0, EPI_CHUNK), pl.ds(m0, EPI_CHUNK), c, m0))
            else:
                natural = pl.ds(c + dil * m0, EPI_CHUNK, stride=dil)
                blocks.append((natural if pitch is None else pl.ds(c * pitch + m0, EPI_CHUNK), natural, c, m0))
    half = HEAD_DIM // 2

    def pass1(h, kind):
        if kind[0] not in ("norm_rope", "norm"):
            return
        for src, _, c, m0 in blocks:
            tmp = pl.ds(c * rows_per_class + m0, EPI_CHUNK)
            a = acc_ref.at[h][src, :]
            ms = jnp.mean(a * a, axis=-1, keepdims=True)
            scl_ref[h, tmp, :] = jnp.broadcast_to(lax.rsqrt(ms + NORM_EPS), a.shape)
            if kind[0] == "norm_rope":
                rot_ref[h, tmp, :] = pltpu.roll(a, half, 1)

    def pass2():
        gains = {}
        for kind in kinds:
            if kind[0] in ("norm_rope", "norm") and kind[1:] not in gains:
                g = qkg_ref[kind[1]:kind[1] + 1, :]
                if kind[2]:
                    g = g * (SCALE * LOG2E)
                gains[kind[1:]] = (g, pltpu.roll(g, half, 1))
        for src, tab, c, m0 in blocks:
            tmp = pl.ds(c * rows_per_class + m0, EPI_CHUNK)
            tables = {}
            for h, kind in enumerate(kinds):
                cols = slice(h * HEAD_DIM, (h + 1) * HEAD_DIM)
                a = acc_ref.at[h][src, :]
                if kind[0] == "norm_rope":
                    if kind[1:] not in tables:
                        g, g_rot = gains[kind[1:]]
                        tables[kind[1:]] = (g * cos_ref[tab, :], g_rot * sin_ref[tab, :])
                    t_cos, t_sin = tables[kind[1:]]
                    y = scl_ref[h, tmp, :] * (a * t_cos + rot_ref[h, tmp, :] * t_sin)
                elif kind[0] == "norm":
                    y = a * scl_ref[h, tmp, :] * gains[kind[1:]][0]
                elif kind[0] == "sigmoid":
                    y = 1.0 / (1.0 + jnp.exp(-a))
                else:
                    y = a
                if dil == 1:
                    o_ref[pl.ds(m0, EPI_CHUNK), cols] = y.astype(o_ref.dtype)
                else:
                    o_ref[c, pl.ds(m0, EPI_CHUNK), cols] = y.astype(o_ref.dtype)

    for h, kind in enumerate(kinds):
        pass1(h, kind)
    pass2()


def _proj_body(h_ref, w_ref, cos_ref, sin_ref, qkg_ref, o_ref, acc_ref, *norm_scratch, n_cols, groups, dil, pitch):
    rot_ref, scl_ref = norm_scratch if norm_scratch else (None, None)
    s = pl.program_id(0)
    je = jnp.maximum(s - 1, 0) % n_cols
    parity = s % 2

    @pl.when(s == 0)
    def _():
        acc_ref[1] = jnp.zeros(acc_ref.shape[1:], F32)

    def step(kinds, p):
        _proj_epilogue(kinds, acc_ref.at[1 - p], rot_ref, scl_ref, o_ref, cos_ref, sin_ref, qkg_ref, dil, pitch)
        res = jnp.dot(h_ref[...], w_ref[...], preferred_element_type=F32)
        for h in range(len(kinds)):
            cols = slice(h * HEAD_DIM, (h + 1) * HEAD_DIM)
            if pitch is None:
                acc_ref[p, h] = res[:, cols]
                continue
            for r0 in range(0, TM, SUBLANES):
                dst = pl.ds((r0 % dil) * pitch + r0 // dil, SUBLANES, stride=pitch)
                acc_ref.at[p, h][dst, :] = res[r0:r0 + SUBLANES, cols]

    for kinds, tiles in groups:
        for p in range(2):
            cond = parity == p
            if len(groups) > 1:
                cond = cond & _in_tiles(je, tiles)
            pl.when(cond)(functools.partial(step, kinds, p))


def _proj(h, w_all, layer, w_tiles, cos, sin_signed, qkg, bsz, seq, dil, name, tn=TN):
    t = h.shape[0]
    heads = tn // HEAD_DIM
    n_cols = len(w_tiles)
    w_base = w_tiles[0]
    w_step = w_tiles[1] - w_tiles[0] if n_cols > 1 else 0
    assert list(w_tiles) == [w_base + w_step * j for j in range(n_cols)]
    pos_tiles = seq // TM
    n_tiles = (t // TM) * n_cols
    groups = _tile_groups(w_tiles, heads)
    needs_norm = any(k[0] in ("norm_rope", "norm") for kinds, _ in groups for k in kinds)
    head_buf = pltpu.VMEM((heads, TM, HEAD_DIM), F32)
    pitch = TM // dil + SUBLANES if dil % (2 * SUBLANES) == 0 else None
    acc_rows = TM if pitch is None else dil * pitch

    dot_tile = lambda s: jnp.minimum(s, n_tiles - 1)
    epi_tile = lambda s: jnp.maximum(s - 1, 0)
    if dil == 1:
        out_shape = jax.ShapeDtypeStruct((t, n_cols * tn), BF16)
        out_spec = pl.BlockSpec((TM, tn), lambda s: (epi_tile(s) // n_cols, epi_tile(s) % n_cols))
    else:
        out_shape = jax.ShapeDtypeStruct((bsz, dil, seq // dil, n_cols * tn), BF16)
        out_spec = pl.BlockSpec(
            (None, dil, TM // dil, tn),
            lambda s: ((epi_tile(s) // n_cols) // pos_tiles, 0, (epi_tile(s) // n_cols) % pos_tiles,
                       epi_tile(s) % n_cols))
    table_spec = pl.BlockSpec((TM, HEAD_DIM), lambda s: ((epi_tile(s) // n_cols) % pos_tiles, 0))
    return pl.pallas_call(
        functools.partial(_proj_body, n_cols=n_cols, groups=groups, dil=dil, pitch=pitch),
        out_shape=out_shape,
        grid=(n_tiles + 1,),
        in_specs=[
            pl.BlockSpec((TM, D_MODEL), lambda s: (dot_tile(s) // n_cols, 0)),
            pl.BlockSpec((None, D_MODEL, tn), lambda s: (layer, 0, w_base + w_step * (dot_tile(s) % n_cols))),
            table_spec, table_spec,
            pl.BlockSpec((8, HEAD_DIM), lambda s: (0, 0)),
        ],
        out_specs=out_spec,
        scratch_shapes=[pltpu.VMEM((2, heads, acc_rows, HEAD_DIM), F32)] + ([head_buf, head_buf] if needs_norm else []),
        compiler_params=pltpu.CompilerParams(dimension_semantics=("arbitrary",), vmem_limit_bytes=40 * MIB),
        name=name,
    )(h, w_all, cos, sin_signed, qkg)


def _band_bias(radius):
    r = np.arange(ATT_Q)[:, None]
    c = np.arange(ATT_Q + 2 * radius)[None, :]
    band = np.abs(c - radius - r) <= radius
    lo_ok = c >= radius
    hi_ok = c < ATT_Q + radius
    variants = [band & lo_ok, band, band & hi_ok, band & lo_ok & hi_ok]
    return np.stack([np.where(v, 0.0, NEG) for v in variants]).astype(np.float32)


def _band_variant(i, n_tiles):
    if n_tiles == 1:
        return 3
    return jnp.where(i == 0, 0, jnp.where(i == n_tiles - 1, 2, 1))


def _halo_specs(block, index, radius, n_rows):
    per = ATT_Q // radius
    last = n_rows // radius - 1
    return [
        pl.BlockSpec(block(radius), lambda *g: index(g, jnp.maximum(g[-1] * per - 1, 0))),
        pl.BlockSpec(block(ATT_Q), lambda *g: index(g, g[-1])),
        pl.BlockSpec(block(radius), lambda *g: index(g, jnp.minimum(g[-1] * per + per, last))),
    ]


def _attn_a_body(sink_ref, q_ref, kp_ref, kc_ref, kn_ref, vp_ref, vc_ref, vn_ref, bias_ref, o_ref):
    head_cols = lambda h: slice(h * HEAD_DIM, (h + 1) * HEAD_DIM)
    kcat = [jnp.concatenate([r[:, head_cols(kv)] for r in (kp_ref, kc_ref, kn_ref)], axis=0)
            for kv in range(A_KV_HEADS)]
    vcat = [jnp.concatenate([r[:, head_cols(kv)] for r in (vp_ref, vc_ref, vn_ref)], axis=0)
            for kv in range(A_KV_HEADS)]

    def finish(h, o, m, l):
        o_ref[:, head_cols(h)] = o.astype(o_ref.dtype)

    _pipelined_attention(
        A_Q_HEADS,
        scores=lambda h: _qk(q_ref[:, head_cols(h)], kcat[h // A_GROUP]) + bias_ref[...],
        values=lambda h: vcat[h // A_GROUP],
        finish=finish,
        extra_logit=lambda h: sink_ref[h] * LOG2E)


def _attn_a(qkv, sink, bias):
    bsz, seq, _ = qkv.shape
    n_tiles = seq // ATT_Q
    block = lambda r: (None, r, A_KV_W)
    k_specs = _halo_specs(block, lambda g, rb: (g[0], rb, OFF_KA // A_KV_W), A_RADIUS, seq)
    v_specs = _halo_specs(block, lambda g, rb: (g[0], rb, OFF_VA // A_KV_W), A_RADIUS, seq)
    return pl.pallas_call(
        _attn_a_body,
        out_shape=jax.ShapeDtypeStruct((bsz, seq, A_Q_W), BF16),
        grid=(bsz, n_tiles),
        in_specs=[
            pl.BlockSpec(memory_space=pltpu.SMEM),
            pl.BlockSpec((None, ATT_Q, A_Q_W), lambda b, i: (b, i, 0)),
            *k_specs, *v_specs,
            pl.BlockSpec((None, ATT_Q, ATT_Q + 2 * A_RADIUS), lambda b, i: (_band_variant(i, n_tiles), 0, 0)),
        ],
        out_specs=pl.BlockSpec((None, ATT_Q, A_Q_W), lambda b, i: (b, i, 0)),
        compiler_params=_params(2, 32),
        name="attn_a",
    )(sink, qkv, qkv, qkv, qkv, qkv, qkv, qkv, bias)


def _attn_b_body(q_ref, kp_ref, kc_ref, kn_ref, vp_ref, vc_ref, vn_ref, bias_ref, o_ref, lse_ref):
    n_classes = q_ref.shape[0]
    head_cols = lambda u: slice((u % B_HEADS_PER_GROUP) * HEAD_DIM, (u % B_HEADS_PER_GROUP + 1) * HEAD_DIM)
    cls = lambda u: u // B_HEADS_PER_GROUP
    cat = lambda refs, u: jnp.concatenate([r[cls(u), :, head_cols(u)] for r in refs], axis=0)

    def finish(u, o, m, l):
        o_ref[cls(u), :, head_cols(u)] = o.astype(o_ref.dtype)
        lse_ref[cls(u), :, head_cols(u)] = jnp.broadcast_to(m + jnp.log2(l), o.shape)

    _pipelined_attention(
        n_classes * B_HEADS_PER_GROUP,
        scores=lambda u: _qk(q_ref[cls(u), :, head_cols(u)], cat((kp_ref, kc_ref, kn_ref), u)) + bias_ref[...],
        values=lambda u: cat((vp_ref, vc_ref, vn_ref), u),
        finish=finish)


def _attn_b_group(qkv, bias, group):
    bsz, dil, m, _ = qkv.shape
    n_tiles = m // ATT_Q
    cb = min(dil, B_CLASS_BLOCK)
    block = lambda r: (None, cb, r, B_OUT_W)
    k_specs = _halo_specs(block, lambda g, rb: (g[0], g[1], rb, 1), B_RADIUS, m)
    v_specs = _halo_specs(block, lambda g, rb: (g[0], g[1], rb, 2), B_RADIUS, m)
    io_spec = pl.BlockSpec(block(ATT_Q), lambda b, c, i: (b, c, i, 0))
    return pl.pallas_call(
        _attn_b_body,
        out_shape=[jax.ShapeDtypeStruct((bsz, dil, m, B_OUT_W), BF16),
                   jax.ShapeDtypeStruct((bsz, dil, m, B_OUT_W), F32)],
        grid=(bsz, dil // cb, n_tiles),
        in_specs=[io_spec, *k_specs, *v_specs,
                  pl.BlockSpec((None, ATT_Q, ATT_Q + 2 * B_RADIUS),
                               lambda b, c, i: (_band_variant(i, n_tiles), 0, 0))],
        out_specs=[io_spec, io_spec],
        compiler_params=_params(3, 32),
        name=f"attn_b{group}",
    )(qkv, qkv, qkv, qkv, qkv, qkv, qkv, bias)


def _b_combine_body(*refs):
    in_refs = refs[:2 * B_GROUPS]
    out_ref, nat_o, nat_l = refs[2 * B_GROUPS:]
    o0_ref, l0_ref = in_refs[0], in_refs[1]
    for gi in range(1, B_GROUPS):
        o_ref, l_ref = in_refs[2 * gi], in_refs[2 * gi + 1]
        dil = B_PATTERNS[gi][1]
        for c in range(dil):
            dst = pl.ds(c, TM // dil, stride=dil)
            for h in range(B_HEADS_PER_GROUP):
                cols = slice(h * HEAD_DIM, (h + 1) * HEAD_DIM)
                nat_o.at[gi - 1, h][dst, :] = o_ref[c, :, cols].astype(F32)
                nat_l.at[gi - 1, h][dst, :] = l_ref[c, :, cols]

    def body(ci, carry):
        rows = pl.ds(pl.multiple_of(ci * COMBINE_CHUNK, COMBINE_CHUNK), COMBINE_CHUNK)
        for h in range(B_HEADS_PER_GROUP):
            cols = slice(h * HEAD_DIM, (h + 1) * HEAD_DIM)
            lses = [l0_ref[0, rows, cols]] + [nat_l[gi, h, rows, :] for gi in range(B_GROUPS - 1)]
            outs = [o0_ref[0, rows, cols].astype(F32)] + [nat_o[gi, h, rows, :] for gi in range(B_GROUPS - 1)]
            mx = functools.reduce(jnp.maximum, lses)
            ws = [jnp.exp2(l - mx) for l in lses]
            num = functools.reduce(lambda a, b: a + b, [w * o for w, o in zip(ws, outs)])
            den = functools.reduce(lambda a, b: a + b, ws)
            out_ref[rows, cols] = (num * (1.0 / den)).astype(out_ref.dtype)
        return carry

    lax.fori_loop(0, TM // COMBINE_CHUNK, body, 0)


def _b_combine(parts, seq):
    assert B_PATTERNS[0][1] == 1
    bsz = parts[0][0].shape[0]
    in_specs, args = [], []
    for (o, lse), (_, dil) in zip(parts, B_PATTERNS):
        spec = pl.BlockSpec((None, dil, TM // dil, B_OUT_W), lambda b, i: (b, 0, i, 0))
        in_specs += [spec, spec]
        args += [o, lse]
    nat_shape = (B_GROUPS - 1, B_HEADS_PER_GROUP, TM, HEAD_DIM)
    return pl.pallas_call(
        _b_combine_body,
        out_shape=jax.ShapeDtypeStruct((bsz, seq, B_OUT_W), BF16),
        grid=(bsz, seq // TM),
        in_specs=in_specs,
        out_specs=pl.BlockSpec((None, TM, B_OUT_W), lambda b, i: (b, i, 0)),
        scratch_shapes=[pltpu.VMEM(nat_shape, F32), pltpu.VMEM(nat_shape, F32)],
        compiler_params=_params(2, 40),
        name="b_combine",
    )(*args)


N_DROW = 2 * C_WIN_ROWS - 1
N_DCOL = 2 * C_WIN_COLS - 1
C_KEY_TILES = 3 * C_TILE_ROWS


def _c_bias_body(rpb_ref, o_ref):
    h = pl.program_id(0)
    qc = lax.broadcasted_iota(jnp.int32, (GRID_W, GRID_W), 0)
    kc = lax.broadcasted_iota(jnp.int32, (GRID_W, GRID_W), 1)
    dcol = jnp.clip(kc - qc, -(C_WIN_COLS - 1), C_WIN_COLS - 1) + (C_WIN_COLS - 1)
    col_start = jnp.clip(qc - C_WIN_COLS // 2, 0, GRID_W - C_WIN_COLS)
    col_ok = (kc >= col_start) & (kc < col_start + C_WIN_COLS)
    tiles = []
    for dr in range(N_DROW):
        t = jnp.zeros((GRID_W, GRID_W), F32)
        for d in range(N_DCOL):
            t = jnp.where(dcol == d, rpb_ref[(h * N_DROW + dr) * N_DCOL + d] * LOG2E, t)
        tiles.append(jnp.where(col_ok, t, NEG))
    masked = jnp.full((GRID_W, GRID_W), NEG, F32)
    half = C_WIN_ROWS // 2
    for var in range(3):
        for qi in range(C_TILE_ROWS):
            row = []
            for kj in range(C_KEY_TILES):
                if var == 0:
                    ok = C_TILE_ROWS <= kj < C_TILE_ROWS + C_WIN_ROWS
                elif var == 2:
                    ok = 2 * C_TILE_ROWS - C_WIN_ROWS <= kj < 2 * C_TILE_ROWS
                else:
                    ok = 0 <= kj - C_TILE_ROWS - qi + half < C_WIN_ROWS
                drow = kj - C_TILE_ROWS - qi
                row.append(tiles[drow + C_WIN_ROWS - 1] if ok else masked)
            o_ref[var, qi * GRID_W:(qi + 1) * GRID_W, :] = jnp.concatenate(row, axis=1)


def _c_bias(rpb_flat):
    return pl.pallas_call(
        _c_bias_body,
        out_shape=jax.ShapeDtypeStruct((3, C_HEADS, ATT_Q, C_KEY_TILES * GRID_W), F32),
        grid=(C_HEADS,),
        in_specs=[pl.BlockSpec(memory_space=pltpu.SMEM)],
        out_specs=pl.BlockSpec((3, None, ATT_Q, C_KEY_TILES * GRID_W), lambda h: (0, h, 0, 0)),
        compiler_params=_params(1, 32),
        name="c_bias",
    )(rpb_flat)


def _attn_c_body(q_ref, kp_ref, kc_ref, kn_ref, vp_ref, vc_ref, vn_ref, bias_ref, o_ref):
    head_cols = lambda h: slice(h * HEAD_DIM, (h + 1) * HEAD_DIM)
    cat = lambda refs, h: jnp.concatenate([r[:, head_cols(h)] for r in refs], axis=0)

    def finish(h, o, m, l):
        o_ref[:, head_cols(h)] = o.astype(o_ref.dtype)

    _pipelined_attention(
        C_HEADS,
        scores=lambda h: _qk(q_ref[:, head_cols(h)], cat((kp_ref, kc_ref, kn_ref), h)) + bias_ref[h],
        values=lambda h: cat((vp_ref, vc_ref, vn_ref), h),
        finish=finish)


def _attn_c(qkv, bias):
    bsz, seq, _ = qkv.shape
    n_tiles = seq // ATT_Q
    block = lambda r: (None, r, C_W)
    k_specs = _halo_specs(block, lambda g, rb: (g[0], rb, 1), ATT_Q, seq)
    v_specs = _halo_specs(block, lambda g, rb: (g[0], rb, 2), ATT_Q, seq)
    return pl.pallas_call(
        _attn_c_body,
        out_shape=jax.ShapeDtypeStruct((bsz, seq, C_W), BF16),
        grid=(bsz, n_tiles),
        in_specs=[
            pl.BlockSpec(block(ATT_Q), lambda b, i: (b, i, 0)),
            *k_specs, *v_specs,
            pl.BlockSpec((None, C_HEADS, ATT_Q, C_KEY_TILES * GRID_W),
                         lambda b, i: (jnp.where(i == 0, 0, jnp.where(i == n_tiles - 1, 2, 1)), 0, 0, 0)),
        ],
        out_specs=pl.BlockSpec(block(ATT_Q), lambda b, i: (b, i, 0)),
        compiler_params=_params(2, 40),
        name="attn_c",
    )(qkv, qkv, qkv, qkv, qkv, qkv, qkv, bias)


def _merge_body(oa_ref, ob_ref, oc_ref, ga_ref, gb_ref, gc_ref, wa_ref, wb_ref, wc_ref, o_ref):
    m = ga_ref[...].astype(F32) * jnp.dot(oa_ref[...], wa_ref[...], preferred_element_type=F32)
    m = m + gb_ref[...].astype(F32) * jnp.dot(ob_ref[...], wb_ref[...], preferred_element_type=F32)
    m = m + gc_ref[...].astype(F32) * jnp.dot(oc_ref[...], wc_ref[...], preferred_element_type=F32)
    o_ref[...] = m.astype(o_ref.dtype)


def _merge(oa, ob, oc, gates, wa, wb, wc, layer):
    t = oa.shape[0]
    gate = lambda k: pl.BlockSpec((TM, TN), lambda i, j: (i, k * (D_MODEL // TN) + j))
    wspec = lambda rows: pl.BlockSpec((None, rows, TN), lambda i, j: (layer, 0, j))
    return pl.pallas_call(
        _merge_body,
        out_shape=jax.ShapeDtypeStruct((t, D_MODEL), BF16),
        grid=(t // TM, D_MODEL // TN),
        in_specs=[
            pl.BlockSpec((TM, A_Q_W), lambda i, j: (i, 0)),
            pl.BlockSpec((TM, B_OUT_W), lambda i, j: (i, 0)),
            pl.BlockSpec((TM, C_W), lambda i, j: (i, 0)),
            gate(0), gate(1), gate(2),
            wspec(A_Q_W), wspec(B_OUT_W), wspec(C_W),
        ],
        out_specs=pl.BlockSpec((TM, TN), lambda i, j: (i, j)),
        compiler_params=_params(2, 48),
        name="merge",
    )(oa, ob, oc, gates, gates, gates, wa, wb, wc)


def _residual_rows_body(a_ref, w_ref, x_ref, g_ref, xo_ref, h_ref, y_ref, rs_ref, *, tm):
    s = pl.program_id(0)
    parity = s % 2

    @pl.when(s == 0)
    def _():
        y_ref[1] = jnp.zeros(y_ref.shape[1:], F32)

    def step(p):
        prev = y_ref.at[1 - p]
        chunks = [pl.ds(r0, NORM_CHUNK) for r0 in range(0, tm, NORM_CHUNK)]
        for rows in chunks:
            y = prev[rows, :]
            ms = jnp.mean(y * y, axis=-1, keepdims=True)
            rs_ref[rows, :] = jnp.broadcast_to(lax.rsqrt(ms + NORM_EPS), (NORM_CHUNK, HEAD_DIM))
        for rows in chunks:
            rs = pltpu.repeat(rs_ref[rows, :], D_MODEL // HEAD_DIM, axis=1)
            h_ref[rows, :] = (prev[rows, :] * rs * g_ref[...]).astype(h_ref.dtype)
        y = x_ref[...] + jnp.dot(a_ref[...], w_ref[...], preferred_element_type=F32)
        xo_ref[...] = y
        y_ref[p] = y

    for p in range(2):
        pl.when(parity == p)(functools.partial(step, p))


def _residual_rows_plain_body(a_ref, w_ref, x_ref, xo_ref):
    xo_ref[...] = x_ref[...] + jnp.dot(a_ref[...], w_ref[...], preferred_element_type=F32)


def _residual_rows(a, w_all, layer, x, g, tm, name):
    t, k = a.shape
    n_tiles = t // tm
    w_spec = pl.BlockSpec((None, k, D_MODEL), lambda s: (layer, 0, 0), pipeline_mode=pl.Buffered(1))
    if g is None:
        row = lambda s: (s, 0)
        return pl.pallas_call(
            _residual_rows_plain_body,
            out_shape=jax.ShapeDtypeStruct((t, D_MODEL), F32),
            grid=(n_tiles,),
            in_specs=[pl.BlockSpec((tm, k), row), w_spec, pl.BlockSpec((tm, D_MODEL), row)],
            out_specs=pl.BlockSpec((tm, D_MODEL), row),
            compiler_params=_params(1, 56),
            name=name,
        )(a, w_all, x), None
    cur = lambda s: (jnp.minimum(s, n_tiles - 1), 0)
    prev = lambda s: (jnp.maximum(s - 1, 0), 0)
    return pl.pallas_call(
        functools.partial(_residual_rows_body, tm=tm),
        out_shape=[jax.ShapeDtypeStruct((t, D_MODEL), F32), jax.ShapeDtypeStruct((t, D_MODEL), BF16)],
        grid=(n_tiles + 1,),
        in_specs=[
            pl.BlockSpec((tm, k), cur),
            w_spec,
            pl.BlockSpec((tm, D_MODEL), cur),
            pl.BlockSpec((1, D_MODEL), lambda s: (0, 0)),
        ],
        out_specs=[pl.BlockSpec((tm, D_MODEL), cur), pl.BlockSpec((tm, D_MODEL), prev)],
        scratch_shapes=[pltpu.VMEM((2, tm, D_MODEL), F32), pltpu.VMEM((tm, HEAD_DIM), F32)],
        compiler_params=pltpu.CompilerParams(dimension_semantics=("arbitrary",), vmem_limit_bytes=56 * MIB),
        name=name,
    )(a, w_all, x, g)


def _ffn_up_body(h_ref, wg_ref, wu_ref, o_ref):
    h = h_ref[...]
    gt = jnp.dot(h, wg_ref[...], preferred_element_type=F32)
    up = jnp.dot(h, wu_ref[...], preferred_element_type=F32)
    o_ref[...] = (gt * (1.0 / (1.0 + jnp.exp(-gt))) * up).astype(o_ref.dtype)


def _ffn_up(h, w_all, layer):
    t = h.shape[0]
    n_tiles = D_FF // TN
    return pl.pallas_call(
        _ffn_up_body,
        out_shape=jax.ShapeDtypeStruct((t, D_FF), BF16),
        grid=(t // TM_FFN, n_tiles),
        in_specs=[
            pl.BlockSpec((TM_FFN, D_MODEL), lambda i, j: (i, 0)),
            pl.BlockSpec((None, D_MODEL, TN), lambda i, j: (layer, 0, j)),
            pl.BlockSpec((None, D_MODEL, TN), lambda i, j: (layer, 0, n_tiles + j)),
        ],
        out_specs=pl.BlockSpec((TM_FFN, TN), lambda i, j: (i, j)),
        compiler_params=_params(2, 52),
        name="ffn_up",
    )(h, w_all, w_all)


def _rope_tables(n):
    half = HEAD_DIM // 2
    inv_freq = ROPE_THETA ** (-jnp.arange(half, dtype=F32) * 2.0 / HEAD_DIM)
    ang = jnp.arange(n, dtype=F32)[:, None] * inv_freq[None, :]
    cos, sin = jnp.cos(ang), jnp.sin(ang)
    return jnp.concatenate([cos, cos], axis=-1), jnp.concatenate([-sin, sin], axis=-1)


def kernel(x, norm1_g, w_in, qk_norm_g, sink_a, rpb_c, w_br_a, w_br_b, w_br_c, w_o,
           norm2_g, w_gate_up, w_down):
    bsz, seq, d = x.shape
    assert d == D_MODEL and seq % TM == 0 and seq // ATT_Q >= 2 and (bsz * seq) % TM_FFN == 0
    assert seq // GRID_W >= C_WIN_ROWS and (seq // B_PATTERNS[-1][1]) % ATT_Q == 0
    t = bsz * seq
    cos, sin_signed = _rope_tables(seq)
    bias_a = jnp.asarray(_band_bias(A_RADIUS))
    bias_b = jnp.asarray(_band_bias(B_RADIUS))
    w_in, w_br_a, w_br_b, w_br_c, w_o, w_gate_up, w_down = (
        w.astype(BF16) for w in (w_in, w_br_a, w_br_b, w_br_c, w_o, w_gate_up, w_down))
    qkg = jnp.pad(qk_norm_g, ((0, 0), (0, 2), (0, 0)))
    tile = lambda off, tn=TN: off // tn
    wide = lambda width: TN_WIDE if width % TN_WIDE == 0 else TN
    xf = x.reshape(t, d)
    h = _norm1(xf, norm1_g[0][None])
    for layer in range(DEPTH):
        proj = functools.partial(_proj, h, w_in, layer, cos=cos, sin_signed=sin_signed, qkg=qkg[layer],
                                 bsz=bsz, seq=seq)
        qkv_a = proj(range(tile(OFF_QA), tile(OFF_QB)), dil=1, name="proj_a")
        oa = _attn_a(qkv_a.reshape(bsz, seq, -1), sink_a[layer], bias_a)
        parts = []
        for group, (_, dil) in enumerate(B_PATTERNS):
            tiles = [tile(off) + group for off in (OFF_QB, OFF_KB, OFF_VB)]
            qkv_b = proj(tiles, dil=dil, name=f"proj_b{group}")
            if dil == 1:
                qkv_b = qkv_b.reshape(bsz, 1, seq, -1)
            parts.append(_attn_b_group(qkv_b, bias_b, group))
        ob = _b_combine(parts, seq)
        tn_c = wide(C_W)
        qkv_c = proj(range(tile(OFF_QC, tn_c), tile(OFF_GATE, tn_c)), dil=1, name="proj_c", tn=tn_c)
        oc = _attn_c(qkv_c.reshape(bsz, seq, -1), _c_bias(rpb_c[layer].reshape(-1)))
        tn_g = wide(math.gcd(D_MODEL, OFF_GATE))
        gates = proj(range(tile(OFF_GATE, tn_g), tile(N_IN, tn_g)), dil=1, name="proj_g", tn=tn_g)
        merged = _merge(oa.reshape(t, A_Q_W), ob.reshape(t, B_OUT_W), oc.reshape(t, C_W), gates,
                        w_br_a, w_br_b, w_br_c, layer)
        xf, h2 = _residual_rows(merged, w_o, layer, xf, norm2_g[layer][None], TM_OUT, "out_proj")
        act = _ffn_up(h2, w_gate_up, layer)
        next_g = norm1_g[layer + 1][None] if layer + 1 < DEPTH else None
        xf, h = _residual_rows(act, w_down, layer, xf, next_g, TM_DOWN, "ffn_down")
    return xf.reshape(bsz, seq, d)
```

```python
import functools

import numpy as np
import jax
import jax.numpy as jnp
from jax import lax
from jax.experimental import pallas as pl
from jax.experimental.pallas import tpu as pltpu

F32 = jnp.float32
BF16 = jnp.bfloat16

D_MODEL = 2048
DEPTH = 4
HEAD_DIM = 128
SUBLANES = 8
ROPE_THETA = 10000.0
NORM_EPS = 1e-6
SCALE = HEAD_DIM ** -0.5
LOG2E = 1.4426950408889634
NEG = -1e30

A_Q_HEADS = 8
A_KV_HEADS = 2
A_GROUP = A_Q_HEADS // A_KV_HEADS
A_RADIUS = 128
B_PATTERNS = ((128, 1), (512, 4), (2048, 16))
B_GROUPS = len(B_PATTERNS)
B_HEADS_PER_GROUP = 4
B_RADIUS = 64
C_HEADS = 8
GRID_W = 64
C_WIN_ROWS = 8
C_WIN_COLS = 16

A_Q_W = A_Q_HEADS * HEAD_DIM
A_KV_W = A_KV_HEADS * HEAD_DIM
B_W = B_GROUPS * B_HEADS_PER_GROUP * HEAD_DIM
B_OUT_W = B_HEADS_PER_GROUP * HEAD_DIM
C_W = C_HEADS * HEAD_DIM
N_IN = A_Q_W + 2 * A_KV_W + 3 * B_W + 3 * C_W + 3 * D_MODEL
D_FF = ((8 * D_MODEL + 3 * 256 - 1) // (3 * 256)) * 256

OFF_QA = 0
OFF_KA = OFF_QA + A_Q_W
OFF_VA = OFF_KA + A_KV_W
OFF_QB = OFF_VA + A_KV_W
OFF_KB = OFF_QB + B_W
OFF_VB = OFF_KB + B_W
OFF_QC = OFF_VB + B_W
OFF_KC = OFF_QC + C_W
OFF_VC = OFF_KC + C_W
OFF_GATE = OFF_VC + C_W

TM = 1024
TN = 512
TN_WIDE = 1024
TM_OUT = 512
TM_DOWN = 256
TM_FFN = 2048
EPI_CHUNK = 64
NORM_CHUNK = 64
ATT_Q = 256
B_CLASS_BLOCK = 2
C_TILE_ROWS = ATT_Q // GRID_W
COMBINE_CHUNK = 128
MIB = 1024 * 1024


def _params(n_axes, vmem_mib):
    return pltpu.CompilerParams(
        dimension_semantics=("parallel",) * n_axes,
        vmem_limit_bytes=vmem_mib * MIB,
    )


def _rmsnorm_rows(x_ref, g_ref, h_ref):
    rows = x_ref.shape[0]

    def body(c, carry):
        r0 = pl.multiple_of(c * NORM_CHUNK, NORM_CHUNK)
        x = x_ref[pl.ds(r0, NORM_CHUNK), :]
        ms = jnp.mean(x * x, axis=-1, keepdims=True)
        h_ref[pl.ds(r0, NORM_CHUNK), :] = (x * lax.rsqrt(ms + NORM_EPS) * g_ref[...]).astype(h_ref.dtype)
        return carry

    lax.fori_loop(0, rows // NORM_CHUNK, body, 0)


def _pipelined_attention(n_units, scores, values, finish, extra_logit=None):
    s_next = scores(0)
    prev = None
    for u in range(n_units + 1):
        s = s_next
        if u + 1 < n_units:
            s_next = scores(u + 1)
        cur = None
        if u < n_units:
            m = jnp.max(s, axis=-1, keepdims=True)
            if extra_logit is not None:
                m = jnp.maximum(m, extra_logit(u))
            p = jnp.exp2(s - m)
            l = jnp.sum(p, axis=-1, keepdims=True)
            if extra_logit is not None:
                l = l + jnp.exp2(extra_logit(u) - m)
            cur = (p.astype(BF16), m, l)
        if prev is not None:
            p_prev, m_prev, l_prev = prev
            o = jnp.dot(p_prev, values(u - 1), preferred_element_type=F32) * (1.0 / l_prev)
            finish(u - 1, o, m_prev, l_prev)
        prev = cur


def _qk(q, k):
    return lax.dot_general(q, k, (((1,), (1,)), ((), ())), preferred_element_type=F32)


def _norm1_body(x_ref, g_ref, o_ref):
    _rmsnorm_rows(x_ref, g_ref, o_ref)


def _norm1(x, g):
    t = x.shape[0]
    return pl.pallas_call(
        _norm1_body,
        out_shape=jax.ShapeDtypeStruct((t, D_MODEL), BF16),
        grid=(t // TM,),
        in_specs=[pl.BlockSpec((TM, D_MODEL), lambda i: (i, 0)),
                  pl.BlockSpec((1, D_MODEL), lambda i: (0, 0))],
        out_specs=pl.BlockSpec((TM, D_MODEL), lambda i: (i, 0)),
        compiler_params=_params(1, 40),
        name="norm1",
    )(x, g)


def _head_kind(head):
    c = head * HEAD_DIM
    if c < OFF_KA:
        return ("norm_rope", 0, True)
    if c < OFF_VA:
        return ("norm_rope", 1, False)
    if c < OFF_QB:
        return ("copy",)
    if c < OFF_KB:
        return ("norm_rope", 2, True)
    if c < OFF_VB:
        return ("norm_rope", 3, False)
    if c < OFF_QC:
        return ("copy",)
    if c < OFF_KC:
        return ("norm", 4, True)
    if c < OFF_VC:
        return ("norm", 5, False)
    assert c < OFF_GATE
    return ("copy",)


def _tile_groups(w_tiles, heads):
    groups = {}
    for j, wt in enumerate(w_tiles):
        kinds = tuple(_head_kind(wt * heads + h) for h in range(heads))
        groups.setdefault(kinds, []).append(j)
    return list(groups.items())


def _in_tiles(j, tiles):
    runs = []
    for t in tiles:
        if runs and runs[-1][1] == t:
            runs[-1][1] = t + 1
        else:
            runs.append([t, t + 1])
    cond = None
    for lo, hi in runs:
        c = (j == lo) if hi == lo + 1 else ((j >= lo) & (j < hi))
        cond = c if cond is None else (cond | c)
    return cond


def _proj_epilogue(kinds, acc_ref, rot_ref, scl_ref, o_ref, cos_ref, sin_ref, qkg_ref, dil, pitch):
    rows_per_class = TM // dil
    blocks = []
    for c in range(dil):
        for m0 in range(0, rows_per_class, EPI_CHUNK):
            if dil == 1:
                blocks.append((pl.ds(m0, EPI_CHUNK), pl.ds(m0, EPI_CHUNK), c, m0))
            else:
                natural = pl.ds(c + dil * m0, EPI_CHUNK, stride=dil)
                blocks.append((natural if pitch is None else pl.ds(c * pitch + m0, EPI_CHUNK), natural, c, m0))
    half = HEAD_DIM // 2

    def pass1(h, kind):
        if kind[0] not in ("norm_rope", "norm"):
            return
        for src, _, c, m0 in blocks:
            tmp = pl.ds(c * rows_per_class + m0, EPI_CHUNK)
            a = acc_ref.at[h][src, :]
            ms = jnp.mean(a * a, axis=-1, keepdims=True)
            scl_ref[h, tmp, :] = jnp.broadcast_to(lax.rsqrt(ms + NORM_EPS), a.shape)
            if kind[0] == "norm_rope":
                rot_ref[h, tmp, :] = pltpu.roll(a, half, 1)

    def pass2():
        gains = {}
        for kind in kinds:
            if kind[0] in ("norm_rope", "norm") and kind[1:] not in gains:
                g = qkg_ref[kind[1]:kind[1] + 1, :]
                if kind[2]:
                    g = g * (SCALE * LOG2E)
                gains[kind[1:]] = (g, pltpu.roll(g, half, 1))
        for src, tab, c, m0 in blocks:
            tmp = pl.ds(c * rows_per_class + m0, EPI_CHUNK)
            tables = {}
            for h, kind in enumerate(kinds):
                cols = slice(h * HEAD_DIM, (h + 1) * HEAD_DIM)
                a = acc_ref.at[h][src, :]
                if kind[0] == "norm_rope":
                    if kind[1:] not in tables:
                        g, g_rot = gains[kind[1:]]
                        tables[kind[1:]] = (g * cos_ref[tab, :], g_rot * sin_ref[tab, :])
                    t_cos, t_sin = tables[kind[1:]]
                    y = scl_ref[h, tmp, :] * (a * t_cos + rot_ref[h, tmp, :] * t_sin)
                elif kind[0] == "norm":
                    y = a * scl_ref[h, tmp, :] * gains[kind[1:]][0]
                else:
                    y = a
                if dil == 1:
                    o_ref[pl.ds(m0, EPI_CHUNK), cols] = y.astype(o_ref.dtype)
                else:
                    o_ref[c, pl.ds(m0, EPI_CHUNK), cols] = y.astype(o_ref.dtype)

    for h, kind in enumerate(kinds):
        pass1(h, kind)
    pass2()


def _proj_body(h_ref, w_ref, cos_ref, sin_ref, qkg_ref, o_ref, acc_ref, *norm_scratch, n_cols, groups, dil, pitch):
    rot_ref, scl_ref = norm_scratch if norm_scratch else (None, None)
    s = pl.program_id(0)
    je = jnp.maximum(s - 1, 0) % n_cols
    parity = s % 2

    @pl.when(s == 0)
    def _():
        acc_ref[1] = jnp.zeros(acc_ref.shape[1:], F32)

    def step(kinds, p):
        _proj_epilogue(kinds, acc_ref.at[1 - p], rot_ref, scl_ref, o_ref, cos_ref, sin_ref, qkg_ref, dil, pitch)
        res = jnp.dot(h_ref[...], w_ref[...], preferred_element_type=F32)
        for h in range(len(kinds)):
            cols = slice(h * HEAD_DIM, (h + 1) * HEAD_DIM)
            if pitch is None:
                acc_ref[p, h] = res[:, cols]
                continue
            for r0 in range(0, TM, SUBLANES):
                dst = pl.ds((r0 % dil) * pitch + r0 // dil, SUBLANES, stride=pitch)
                acc_ref.at[p, h][dst, :] = res[r0:r0 + SUBLANES, cols]

    for kinds, tiles in groups:
        for p in range(2):
            cond = parity == p
            if len(groups) > 1:
                cond = cond & _in_tiles(je, tiles)
            pl.when(cond)(functools.partial(step, kinds, p))


def _proj(h, w_all, layer, w_tiles, cos, sin_signed, qkg, bsz, seq, dil, name, tn=TN):
    t = h.shape[0]
    heads = tn // HEAD_DIM
    n_cols = len(w_tiles)
    w_base = w_tiles[0]
    w_step = w_tiles[1] - w_tiles[0] if n_cols > 1 else 0
    assert list(w_tiles) == [w_base + w_step * j for j in range(n_cols)]
    pos_tiles = seq // TM
    n_tiles = (t // TM) * n_cols
    groups = _tile_groups(w_tiles, heads)
    needs_norm = any(k[0] in ("norm_rope", "norm") for kinds, _ in groups for k in kinds)
    head_buf = pltpu.VMEM((heads, TM, HEAD_DIM), F32)
    pitch = TM // dil + SUBLANES if dil % (2 * SUBLANES) == 0 else None
    acc_rows = TM if pitch is None else dil * pitch

    dot_tile = lambda s: jnp.minimum(s, n_tiles - 1)
    epi_tile = lambda s: jnp.maximum(s - 1, 0)
    if dil == 1:
        out_shape = jax.ShapeDtypeStruct((t, n_cols * tn), BF16)
        out_spec = pl.BlockSpec((TM, tn), lambda s: (epi_tile(s) // n_cols, epi_tile(s) % n_cols))
    else:
        out_shape = jax.ShapeDtypeStruct((bsz, dil, seq // dil, n_cols * tn), BF16)
        out_spec = pl.BlockSpec(
            (None, dil, TM // dil, tn),
            lambda s: ((epi_tile(s) // n_cols) // pos_tiles, 0, (epi_tile(s) // n_cols) % pos_tiles,
                       epi_tile(s) % n_cols))
    table_spec = pl.BlockSpec((TM, HEAD_DIM), lambda s: ((epi_tile(s) // n_cols) % pos_tiles, 0))
    return pl.pallas_call(
        functools.partial(_proj_body, n_cols=n_cols, groups=groups, dil=dil, pitch=pitch),
        out_shape=out_shape,
        grid=(n_tiles + 1,),
        in_specs=[
            pl.BlockSpec((TM, D_MODEL), lambda s: (dot_tile(s) // n_cols, 0)),
            pl.BlockSpec((None, D_MODEL, tn), lambda s: (layer, 0, w_base + w_step * (dot_tile(s) % n_cols))),
            table_spec, table_spec,
            pl.BlockSpec((8, HEAD_DIM), lambda s: (0, 0)),
        ],
        out_specs=out_spec,
        scratch_shapes=[pltpu.VMEM((2, heads, acc_rows, HEAD_DIM), F32)] + ([head_buf, head_buf] if needs_norm else []),
        compiler_params=pltpu.CompilerParams(dimension_semantics=("arbitrary",), vmem_limit_bytes=40 * MIB),
        name=name,
    )(h, w_all, cos, sin_signed, qkg)


def _band_bias(radius):
    r = np.arange(ATT_Q)[:, None]
    c = np.arange(ATT_Q + 2 * radius)[None, :]
    band = np.abs(c - radius - r) <= radius
    lo_ok = c >= radius
    hi_ok = c < ATT_Q + radius
    variants = [band & lo_ok, band, band & hi_ok, band & lo_ok & hi_ok]
    return np.stack([np.where(v, 0.0, NEG) for v in variants]).astype(np.float32)


def _band_variant(i, n_tiles):
    if n_tiles == 1:
        return 3
    return jnp.where(i == 0, 0, jnp.where(i == n_tiles - 1, 2, 1))


def _halo_specs(block, index, radius, n_rows):
    per = ATT_Q // radius
    last = n_rows // radius - 1
    return [
        pl.BlockSpec(block(radius), lambda *g: index(g, jnp.maximum(g[-1] * per - 1, 0))),
        pl.BlockSpec(block(ATT_Q), lambda *g: index(g, g[-1])),
        pl.BlockSpec(block(radius), lambda *g: index(g, jnp.minimum(g[-1] * per + per, last))),
    ]


def _attn_a_body(sink_ref, q_ref, kp_ref, kc_ref, kn_ref, vp_ref, vc_ref, vn_ref, bias_ref, o_ref):
    head_cols = lambda h: slice(h * HEAD_DIM, (h + 1) * HEAD_DIM)
    kcat = [jnp.concatenate([r[:, head_cols(kv)] for r in (kp_ref, kc_ref, kn_ref)], axis=0)
            for kv in range(A_KV_HEADS)]
    vcat = [jnp.concatenate([r[:, head_cols(kv)] for r in (vp_ref, vc_ref, vn_ref)], axis=0)
            for kv in range(A_KV_HEADS)]

    def finish(h, o, m, l):
        o_ref[:, head_cols(h)] = o.astype(o_ref.dtype)

    _pipelined_attention(
        A_Q_HEADS,
        scores=lambda h: _qk(q_ref[:, head_cols(h)], kcat[h // A_GROUP]) + bias_ref[...],
        values=lambda h: vcat[h // A_GROUP],
        finish=finish,
        extra_logit=lambda h: sink_ref[h] * LOG2E)


def _attn_a(qkv, sink, bias):
    bsz, seq, _ = qkv.shape
    n_tiles = seq // ATT_Q
    block = lambda r: (None, r, A_KV_W)
    k_specs = _halo_specs(block, lambda g, rb: (g[0], rb, OFF_KA // A_KV_W), A_RADIUS, seq)
    v_specs = _halo_specs(block, lambda g, rb: (g[0], rb, OFF_VA // A_KV_W), A_RADIUS, seq)
    return pl.pallas_call(
        _attn_a_body,
        out_shape=jax.ShapeDtypeStruct((bsz, seq, A_Q_W), BF16),
        grid=(bsz, n_tiles),
        in_specs=[
            pl.BlockSpec(memory_space=pltpu.SMEM),
            pl.BlockSpec((None, ATT_Q, A_Q_W), lambda b, i: (b, i, 0)),
            *k_specs, *v_specs,
            pl.BlockSpec((None, ATT_Q, ATT_Q + 2 * A_RADIUS), lambda b, i: (_band_variant(i, n_tiles), 0, 0)),
        ],
        out_specs=pl.BlockSpec((None, ATT_Q, A_Q_W), lambda b, i: (b, i, 0)),
        compiler_params=_params(2, 32),
        name="attn_a",
    )(sink, qkv, qkv, qkv, qkv, qkv, qkv, qkv, bias)


def _attn_b_body(q_ref, kp_ref, kc_ref, kn_ref, vp_ref, vc_ref, vn_ref, bias_ref, o_ref, lse_ref):
    n_classes = q_ref.shape[0]
    head_cols = lambda u: slice((u % B_HEADS_PER_GROUP) * HEAD_DIM, (u % B_HEADS_PER_GROUP + 1) * HEAD_DIM)
    cls = lambda u: u // B_HEADS_PER_GROUP
    cat = lambda refs, u: jnp.concatenate([r[cls(u), :, head_cols(u)] for r in refs], axis=0)

    def finish(u, o, m, l):
        o_ref[cls(u), :, head_cols(u)] = o.astype(o_ref.dtype)
        lse_ref[cls(u), :, head_cols(u)] = jnp.broadcast_to(m + jnp.log2(l), o.shape)

    _pipelined_attention(
        n_classes * B_HEADS_PER_GROUP,
        scores=lambda u: _qk(q_ref[cls(u), :, head_cols(u)], cat((kp_ref, kc_ref, kn_ref), u)) + bias_ref[...],
        values=lambda u: cat((vp_ref, vc_ref, vn_ref), u),
        finish=finish)


def _attn_b_group(qkv, bias, group):
    bsz, dil, m, _ = qkv.shape
    n_tiles = m // ATT_Q
    cb = min(dil, B_CLASS_BLOCK)
    block = lambda r: (None, cb, r, B_OUT_W)
    k_specs = _halo_specs(block, lambda g, rb: (g[0], g[1], rb, 1), B_RADIUS, m)
    v_specs = _halo_specs(block, lambda g, rb: (g[0], g[1], rb, 2), B_RADIUS, m)
    io_spec = pl.BlockSpec(block(ATT_Q), lambda b, c, i: (b, c, i, 0))
    return pl.pallas_call(
        _attn_b_body,
        out_shape=[jax.ShapeDtypeStruct((bsz, dil, m, B_OUT_W), BF16),
                   jax.ShapeDtypeStruct((bsz, dil, m, B_OUT_W), F32)],
        grid=(bsz, dil // cb, n_tiles),
        in_specs=[io_spec, *k_specs, *v_specs,
                  pl.BlockSpec((None, ATT_Q, ATT_Q + 2 * B_RADIUS),
                               lambda b, c, i: (_band_variant(i, n_tiles), 0, 0))],
        out_specs=[io_spec, io_spec],
        compiler_params=_params(3, 32),
        name=f"attn_b{group}",
    )(qkv, qkv, qkv, qkv, qkv, qkv, qkv, bias)


def _b_combine_body(*refs):
    in_refs = refs[:2 * B_GROUPS]
    out_ref, nat_o, nat_l = refs[2 * B_GROUPS:]
    o0_ref, l0_ref = in_refs[0], in_refs[1]
    for gi in range(1, B_GROUPS):
        o_ref, l_ref = in_refs[2 * gi], in_refs[2 * gi + 1]
        dil = B_PATTERNS[gi][1]
        for c in range(dil):
            dst = pl.ds(c, TM // dil, stride=dil)
            for h in range(B_HEADS_PER_GROUP):
                cols = slice(h * HEAD_DIM, (h + 1) * HEAD_DIM)
                nat_o.at[gi - 1, h][dst, :] = o_ref[c, :, cols].astype(F32)
                nat_l.at[gi - 1, h][dst, :] = l_ref[c, :, cols]

    def body(ci, carry):
        rows = pl.ds(pl.multiple_of(ci * COMBINE_CHUNK, COMBINE_CHUNK), COMBINE_CHUNK)
        for h in range(B_HEADS_PER_GROUP):
            cols = slice(h * HEAD_DIM, (h + 1) * HEAD_DIM)
            lses = [l0_ref[0, rows, cols]] + [nat_l[gi, h, rows, :] for gi in range(B_GROUPS - 1)]
            outs = [o0_ref[0, rows, cols].astype(F32)] + [nat_o[gi, h, rows, :] for gi in range(B_GROUPS - 1)]
            mx = functools.reduce(jnp.maximum, lses)
            ws = [jnp.exp2(l - mx) for l in lses]
            num = functools.reduce(lambda a, b: a + b, [w * o for w, o in zip(ws, outs)])
            den = functools.reduce(lambda a, b: a + b, ws)
            out_ref[rows, cols] = (num * (1.0 / den)).astype(out_ref.dtype)
        return carry

    lax.fori_loop(0, TM // COMBINE_CHUNK, body, 0)


def _b_combine(parts, seq):
    assert B_PATTERNS[0][1] == 1
    bsz = parts[0][0].shape[0]
    in_specs, args = [], []
    for (o, lse), (_, dil) in zip(parts, B_PATTERNS):
        spec = pl.BlockSpec((None, dil, TM // dil, B_OUT_W), lambda b, i: (b, 0, i, 0))
        in_specs += [spec, spec]
        args += [o, lse]
    nat_shape = (B_GROUPS - 1, B_HEADS_PER_GROUP, TM, HEAD_DIM)
    return pl.pallas_call(
        _b_combine_body,
        out_shape=jax.ShapeDtypeStruct((bsz, seq, B_OUT_W), BF16),
        grid=(bsz, seq // TM),
        in_specs=in_specs,
        out_specs=pl.BlockSpec((None, TM, B_OUT_W), lambda b, i: (b, i, 0)),
        scratch_shapes=[pltpu.VMEM(nat_shape, F32), pltpu.VMEM(nat_shape, F32)],
        compiler_params=_params(2, 40),
        name="b_combine",
    )(*args)


N_DROW = 2 * C_WIN_ROWS - 1
N_DCOL = 2 * C_WIN_COLS - 1
C_KEY_TILES = 3 * C_TILE_ROWS


def _c_bias_body(rpb_ref, o_ref):
    h = pl.program_id(0)
    qc = lax.broadcasted_iota(jnp.int32, (GRID_W, GRID_W), 0)
    kc = lax.broadcasted_iota(jnp.int32, (GRID_W, GRID_W), 1)
    dcol = jnp.clip(kc - qc, -(C_WIN_COLS - 1), C_WIN_COLS - 1) + (C_WIN_COLS - 1)
    col_start = jnp.clip(qc - C_WIN_COLS // 2, 0, GRID_W - C_WIN_COLS)
    col_ok = (kc >= col_start) & (kc < col_start + C_WIN_COLS)
    tiles = []
    for dr in range(N_DROW):
        t = jnp.zeros((GRID_W, GRID_W), F32)
        for d in range(N_DCOL):
            t = jnp.where(dcol == d, rpb_ref[(h * N_DROW + dr) * N_DCOL + d] * LOG2E, t)
        tiles.append(jnp.where(col_ok, t, NEG))
    masked = jnp.full((GRID_W, GRID_W), NEG, F32)
    half = C_WIN_ROWS // 2
    for var in range(3):
        for qi in range(C_TILE_ROWS):
            row = []
            for kj in range(C_KEY_TILES):
                if var == 0:
                    ok = C_TILE_ROWS <= kj < C_TILE_ROWS + C_WIN_ROWS
                elif var == 2:
                    ok = 2 * C_TILE_ROWS - C_WIN_ROWS <= kj < 2 * C_TILE_ROWS
                else:
                    ok = 0 <= kj - C_TILE_ROWS - qi + half < C_WIN_ROWS
                drow = kj - C_TILE_ROWS - qi
                row.append(tiles[drow + C_WIN_ROWS - 1] if ok else masked)
            o_ref[var, qi * GRID_W:(qi + 1) * GRID_W, :] = jnp.concatenate(row, axis=1)


def _c_bias(rpb_flat):
    return pl.pallas_call(
        _c_bias_body,
        out_shape=jax.ShapeDtypeStruct((3, C_HEADS, ATT_Q, C_KEY_TILES * GRID_W), F32),
        grid=(C_HEADS,),
        in_specs=[pl.BlockSpec(memory_space=pltpu.SMEM)],
        out_specs=pl.BlockSpec((3, None, ATT_Q, C_KEY_TILES * GRID_W), lambda h: (0, h, 0, 0)),
        compiler_params=_params(1, 32),
        name="c_bias",
    )(rpb_flat)


def _attn_c_body(q_ref, kp_ref, kc_ref, kn_ref, vp_ref, vc_ref, vn_ref, bias_ref, o_ref):
    head_cols = lambda h: slice(h * HEAD_DIM, (h + 1) * HEAD_DIM)
    cat = lambda refs, h: jnp.concatenate([r[:, head_cols(h)] for r in refs], axis=0)

    def finish(h, o, m, l):
        o_ref[:, head_cols(h)] = o.astype(o_ref.dtype)

    _pipelined_attention(
        C_HEADS,
        scores=lambda h: _qk(q_ref[:, head_cols(h)], cat((kp_ref, kc_ref, kn_ref), h)) + bias_ref[h],
        values=lambda h: cat((vp_ref, vc_ref, vn_ref), h),
        finish=finish)


def _attn_c(qkv, bias):
    bsz, seq, _ = qkv.shape
    n_tiles = seq // ATT_Q
    block = lambda r: (None, r, C_W)
    k_specs = _halo_specs(block, lambda g, rb: (g[0], rb, 1), ATT_Q, seq)
    v_specs = _halo_specs(block, lambda g, rb: (g[0], rb, 2), ATT_Q, seq)
    return pl.pallas_call(
        _attn_c_body,
        out_shape=jax.ShapeDtypeStruct((bsz, seq, C_W), BF16),
        grid=(bsz, n_tiles),
        in_specs=[
            pl.BlockSpec(block(ATT_Q), lambda b, i: (b, i, 0)),
            *k_specs, *v_specs,
            pl.BlockSpec((None, C_HEADS, ATT_Q, C_KEY_TILES * GRID_W),
                         lambda b, i: (jnp.where(i == 0, 0, jnp.where(i == n_tiles - 1, 2, 1)), 0, 0, 0)),
        ],
        out_specs=pl.BlockSpec(block(ATT_Q), lambda b, i: (b, i, 0)),
        compiler_params=_params(2, 40),
        name="attn_c",
    )(qkv, qkv, qkv, qkv, qkv, qkv, qkv, bias)


def _merge_body(h_ref, wga_ref, wgb_ref, wgc_ref, oa_ref, ob_ref, oc_ref, wa_ref, wb_ref, wc_ref, o_ref):
    h = h_ref[...]
    m = None
    for wg_ref, o_in_ref, w_ref in ((wga_ref, oa_ref, wa_ref), (wgb_ref, ob_ref, wb_ref), (wgc_ref, oc_ref, wc_ref)):
        logit = jnp.dot(h, wg_ref[...], preferred_element_type=F32)
        gate = 1.0 / (1.0 + jnp.exp(-logit))
        term = gate * jnp.dot(o_in_ref[...], w_ref[...], preferred_element_type=F32)
        m = term if m is None else m + term
    o_ref[...] = m.astype(o_ref.dtype)


def _merge(h, w_in, oa, ob, oc, wa, wb, wc, layer):
    t = oa.shape[0]
    gate_w = lambda k: pl.BlockSpec((None, D_MODEL, TN), lambda i, j: (layer, 0, (OFF_GATE + k * D_MODEL) // TN + j))
    wspec = lambda rows: pl.BlockSpec((None, rows, TN), lambda i, j: (layer, 0, j))
    return pl.pallas_call(
        _merge_body,
        out_shape=jax.ShapeDtypeStruct((t, D_MODEL), BF16),
        grid=(t // TM, D_MODEL // TN),
        in_specs=[
            pl.BlockSpec((TM, D_MODEL), lambda i, j: (i, 0)),
            gate_w(0), gate_w(1), gate_w(2),
            pl.BlockSpec((TM, A_Q_W), lambda i, j: (i, 0)),
            pl.BlockSpec((TM, B_OUT_W), lambda i, j: (i, 0)),
            pl.BlockSpec((TM, C_W), lambda i, j: (i, 0)),
            wspec(A_Q_W), wspec(B_OUT_W), wspec(C_W),
        ],
        out_specs=pl.BlockSpec((TM, TN), lambda i, j: (i, j)),
        compiler_params=_params(2, 56),
        name="merge",
    )(h, w_in, w_in, w_in, oa, ob, oc, wa, wb, wc)


def _residual_rows_body(a_ref, w_ref, x_ref, g_ref, xo_ref, h_ref, y_ref, rs_ref, *, tm):
    s = pl.program_id(0)
    parity = s % 2

    @pl.when(s == 0)
    def _():
        y_ref[1] = jnp.zeros(y_ref.shape[1:], F32)

    def step(p):
        prev = y_ref.at[1 - p]
        chunks = [pl.ds(r0, NORM_CHUNK) for r0 in range(0, tm, NORM_CHUNK)]
        for rows in chunks:
            y = prev[rows, :]
            ms = jnp.mean(y * y, axis=-1, keepdims=True)
            rs_ref[rows, :] = jnp.broadcast_to(lax.rsqrt(ms + NORM_EPS), (NORM_CHUNK, HEAD_DIM))
        for rows in chunks:
            rs = pltpu.repeat(rs_ref[rows, :], D_MODEL // HEAD_DIM, axis=1)
            h_ref[rows, :] = (prev[rows, :] * rs * g_ref[...]).astype(h_ref.dtype)
        y = x_ref[...] + jnp.dot(a_ref[...], w_ref[...], preferred_element_type=F32)
        xo_ref[...] = y
        y_ref[p] = y

    for p in range(2):
        pl.when(parity == p)(functools.partial(step, p))


def _residual_rows_plain_body(a_ref, w_ref, x_ref, xo_ref):
    xo_ref[...] = x_ref[...] + jnp.dot(a_ref[...], w_ref[...], preferred_element_type=F32)


def _residual_rows(a, w_all, layer, x, g, tm, name):
    t, k = a.shape
    n_tiles = t // tm
    w_spec = pl.BlockSpec((None, k, D_MODEL), lambda s: (layer, 0, 0), pipeline_mode=pl.Buffered(1))
    if g is None:
        row = lambda s: (s, 0)
        return pl.pallas_call(
            _residual_rows_plain_body,
            out_shape=jax.ShapeDtypeStruct((t, D_MODEL), F32),
            grid=(n_tiles,),
            in_specs=[pl.BlockSpec((tm, k), row), w_spec, pl.BlockSpec((tm, D_MODEL), row)],
            out_specs=pl.BlockSpec((tm, D_MODEL), row),
            compiler_params=_params(1, 56),
            name=name,
        )(a, w_all, x), None
    cur = lambda s: (jnp.minimum(s, n_tiles - 1), 0)
    prev = lambda s: (jnp.maximum(s - 1, 0), 0)
    return pl.pallas_call(
        functools.partial(_residual_rows_body, tm=tm),
        out_shape=[jax.ShapeDtypeStruct((t, D_MODEL), F32), jax.ShapeDtypeStruct((t, D_MODEL), BF16)],
        grid=(n_tiles + 1,),
        in_specs=[
            pl.BlockSpec((tm, k), cur),
            w_spec,
            pl.BlockSpec((tm, D_MODEL), cur),
            pl.BlockSpec((1, D_MODEL), lambda s: (0, 0)),
        ],
        out_specs=[pl.BlockSpec((tm, D_MODEL), cur), pl.BlockSpec((tm, D_MODEL), prev)],
        scratch_shapes=[pltpu.VMEM((2, tm, D_MODEL), F32), pltpu.VMEM((tm, HEAD_DIM), F32)],
        compiler_params=pltpu.CompilerParams(dimension_semantics=("arbitrary",), vmem_limit_bytes=56 * MIB),
        name=name,
    )(a, w_all, x, g)


def _ffn_up_body(h_ref, wg_ref, wu_ref, o_ref):
    h = h_ref[...]
    gt = jnp.dot(h, wg_ref[...], preferred_element_type=F32)
    up = jnp.dot(h, wu_ref[...], preferred_element_type=F32)
    o_ref[...] = (gt * (1.0 / (1.0 + jnp.exp(-gt))) * up).astype(o_ref.dtype)


def _ffn_up(h, w_all, layer):
    t = h.shape[0]
    n_tiles = D_FF // TN
    return pl.pallas_call(
        _ffn_up_body,
        out_shape=jax.ShapeDtypeStruct((t, D_FF), BF16),
        grid=(t // TM_FFN, n_tiles),
        in_specs=[
            pl.BlockSpec((TM_FFN, D_MODEL), lambda i, j: (i, 0)),
            pl.BlockSpec((None, D_MODEL, TN), lambda i, j: (layer, 0, j)),
            pl.BlockSpec((None, D_MODEL, TN), lambda i, j: (layer, 0, n_tiles + j)),
        ],
        out_specs=pl.BlockSpec((TM_FFN, TN), lambda i, j: (i, j)),
        compiler_params=_params(2, 52),
        name="ffn_up",
    )(h, w_all, w_all)


def _rope_tables(n):
    half = HEAD_DIM // 2
    inv_freq = ROPE_THETA ** (-jnp.arange(half, dtype=F32) * 2.0 / HEAD_DIM)
    ang = jnp.arange(n, dtype=F32)[:, None] * inv_freq[None, :]
    cos, sin = jnp.cos(ang), jnp.sin(ang)
    return jnp.concatenate([cos, cos], axis=-1), jnp.concatenate([-sin, sin], axis=-1)


def kernel(x, norm1_g, w_in, qk_norm_g, sink_a, rpb_c, w_br_a, w_br_b, w_br_c, w_o,
           norm2_g, w_gate_up, w_down):
    bsz, seq, d = x.shape
    assert d == D_MODEL and seq % TM == 0 and seq // ATT_Q >= 2 and (bsz * seq) % TM_FFN == 0
    assert seq // GRID_W >= C_WIN_ROWS and (seq // B_PATTERNS[-1][1]) % ATT_Q == 0
    t = bsz * seq
    cos, sin_signed = _rope_tables(seq)
    bias_a = jnp.asarray(_band_bias(A_RADIUS))
    bias_b = jnp.asarray(_band_bias(B_RADIUS))
    w_in, w_br_a, w_br_b, w_br_c, w_o, w_gate_up, w_down = (
        w.astype(BF16) for w in (w_in, w_br_a, w_br_b, w_br_c, w_o, w_gate_up, w_down))
    qkg = jnp.pad(qk_norm_g, ((0, 0), (0, 2), (0, 0)))
    tile = lambda off, tn=TN: off // tn
    wide = lambda width: TN_WIDE if width % TN_WIDE == 0 else TN
    xf = x.reshape(t, d)
    h = _norm1(xf, norm1_g[0][None])
    for layer in range(DEPTH):
        proj = functools.partial(_proj, h, w_in, layer, cos=cos, sin_signed=sin_signed, qkg=qkg[layer],
                                 bsz=bsz, seq=seq)
        qkv_a = proj(range(tile(OFF_QA), tile(OFF_QB)), dil=1, name="proj_a")
        oa = _attn_a(qkv_a.reshape(bsz, seq, -1), sink_a[layer], bias_a)
        parts = []
        for group, (_, dil) in enumerate(B_PATTERNS):
            tiles = [tile(off) + group for off in (OFF_QB, OFF_KB, OFF_VB)]
            qkv_b = proj(tiles, dil=dil, name=f"proj_b{group}")
            if dil == 1:
                qkv_b = qkv_b.reshape(bsz, 1, seq, -1)
            parts.append(_attn_b_group(qkv_b, bias_b, group))
        ob = _b_combine(parts, seq)
        tn_c = wide(C_W)
        qkv_c = proj(range(tile(OFF_QC, tn_c), tile(OFF_GATE, tn_c)), dil=1, name="proj_c", tn=tn_c)
        oc = _attn_c(qkv_c.reshape(bsz, seq, -1), _c_bias(rpb_c[layer].reshape(-1)))
        merged = _merge(h, w_in, oa.reshape(t, A_Q_W), ob.reshape(t, B_OUT_W), oc.reshape(t, C_W),
                        w_br_a, w_br_b, w_br_c, layer)
        xf, h2 = _residual_rows(merged, w_o, layer, xf, norm2_g[layer][None], TM_OUT, "out_proj")
        act = _ffn_up(h2, w_gate_up, layer)
        next_g = norm1_g[layer + 1][None] if layer + 1 < DEPTH else None
        xf, h = _residual_rows(act, w_down, layer, xf, next_g, TM_DOWN, "ffn_down")
    return xf.reshape(bsz, seq, d)
```

```python
import functools

import numpy as np
import jax
import jax.numpy as jnp
from jax import lax
from jax.experimental import pallas as pl
from jax.experimental.pallas import tpu as pltpu

F32 = jnp.float32
BF16 = jnp.bfloat16

D_MODEL = 2048
DEPTH = 4
HEAD_DIM = 128
SUBLANES = 8
ROPE_THETA = 10000.0
NORM_EPS = 1e-6
SCALE = HEAD_DIM ** -0.5
LOG2E = 1.4426950408889634
NEG = -1e30

A_Q_HEADS = 8
A_KV_HEADS = 2
A_GROUP = A_Q_HEADS // A_KV_HEADS
A_RADIUS = 128
B_PATTERNS = ((128, 1), (512, 4), (2048, 16))
B_GROUPS = len(B_PATTERNS)
B_HEADS_PER_GROUP = 4
B_RADIUS = 64
C_HEADS = 8
GRID_W = 64
C_WIN_ROWS = 8
C_WIN_COLS = 16

A_Q_W = A_Q_HEADS * HEAD_DIM
A_KV_W = A_KV_HEADS * HEAD_DIM
B_W = B_GROUPS * B_HEADS_PER_GROUP * HEAD_DIM
B_OUT_W = B_HEADS_PER_GROUP * HEAD_DIM
C_W = C_HEADS * HEAD_DIM
N_IN = A_Q_W + 2 * A_KV_W + 3 * B_W + 3 * C_W + 3 * D_MODEL
D_FF = ((8 * D_MODEL + 3 * 256 - 1) // (3 * 256)) * 256

OFF_QA = 0
OFF_KA = OFF_QA + A_Q_W
OFF_VA = OFF_KA + A_KV_W
OFF_QB = OFF_VA + A_KV_W
OFF_KB = OFF_QB + B_W
OFF_VB = OFF_KB + B_W
OFF_QC = OFF_VB + B_W
OFF_KC = OFF_QC + C_W
OFF_VC = OFF_KC + C_W
OFF_GATE = OFF_VC + C_W

TM = 1024
TN = 512
TN_WIDE = 1024
TM_OUT = 512
TM_DOWN = 256
TM_FFN = 2048
TM_ROPE = 2048
EPI_CHUNK = 64
NORM_CHUNK = 64
ATT_Q = 256
B_CLASS_BLOCK = 2
C_TILE_ROWS = ATT_Q // GRID_W
COMBINE_CHUNK = 128
MIB = 1024 * 1024


def _params(n_axes, vmem_mib):
    return pltpu.CompilerParams(
        dimension_semantics=("parallel",) * n_axes,
        vmem_limit_bytes=vmem_mib * MIB,
    )


def _rmsnorm_rows(x_ref, g_ref, h_ref):
    rows = x_ref.shape[0]

    def body(c, carry):
        r0 = pl.multiple_of(c * NORM_CHUNK, NORM_CHUNK)
        x = x_ref[pl.ds(r0, NORM_CHUNK), :]
        ms = jnp.mean(x * x, axis=-1, keepdims=True)
        h_ref[pl.ds(r0, NORM_CHUNK), :] = (x * lax.rsqrt(ms + NORM_EPS) * g_ref[...]).astype(h_ref.dtype)
        return carry

    lax.fori_loop(0, rows // NORM_CHUNK, body, 0)


def _pipelined_attention(n_units, scores, values, finish, extra_logit=None):
    s_next = scores(0)
    prev = None
    for u in range(n_units + 1):
        s = s_next
        if u + 1 < n_units:
            s_next = scores(u + 1)
        cur = None
        if u < n_units:
            m = jnp.max(s, axis=-1, keepdims=True)
            if extra_logit is not None:
                m = jnp.maximum(m, extra_logit(u))
            p = jnp.exp2(s - m)
            l = jnp.sum(p, axis=-1, keepdims=True)
            if extra_logit is not None:
                l = l + jnp.exp2(extra_logit(u) - m)
            cur = (p.astype(BF16), m, l)
        if prev is not None:
            p_prev, m_prev, l_prev = prev
            o = jnp.dot(p_prev, values(u - 1), preferred_element_type=F32) * (1.0 / l_prev)
            finish(u - 1, o, m_prev, l_prev)
        prev = cur


def _qk(q, k):
    return lax.dot_general(q, k, (((1,), (1,)), ((), ())), preferred_element_type=F32)


def _norm1_body(x_ref, g_ref, o_ref):
    _rmsnorm_rows(x_ref, g_ref, o_ref)


def _norm1(x, g):
    t = x.shape[0]
    return pl.pallas_call(
        _norm1_body,
        out_shape=jax.ShapeDtypeStruct((t, D_MODEL), BF16),
        grid=(t // TM,),
        in_specs=[pl.BlockSpec((TM, D_MODEL), lambda i: (i, 0)),
                  pl.BlockSpec((1, D_MODEL), lambda i: (0, 0))],
        out_specs=pl.BlockSpec((TM, D_MODEL), lambda i: (i, 0)),
        compiler_params=_params(1, 40),
        name="norm1",
    )(x, g)


def _head_kind(head):
    c = head * HEAD_DIM
    if c < OFF_KA:
        return ("norm_rope", 0, True)
    if c < OFF_VA:
        return ("norm_rope", 1, False)
    if c < OFF_QB:
        return ("copy",)
    if c < OFF_KB:
        return ("norm_rope", 2, True)
    if c < OFF_VB:
        return ("norm_rope", 3, False)
    if c < OFF_QC:
        return ("copy",)
    if c < OFF_KC:
        return ("norm", 4, True)
    if c < OFF_VC:
        return ("norm", 5, False)
    assert c < OFF_GATE
    return ("copy",)


def _tile_groups(w_tiles, heads):
    groups = {}
    for j, wt in enumerate(w_tiles):
        kinds = tuple(_head_kind(wt * heads + h) for h in range(heads))
        groups.setdefault(kinds, []).append(j)
    return list(groups.items())


def _in_tiles(j, tiles):
    runs = []
    for t in tiles:
        if runs and runs[-1][1] == t:
            runs[-1][1] = t + 1
        else:
            runs.append([t, t + 1])
    cond = None
    for lo, hi in runs:
        c = (j == lo) if hi == lo + 1 else ((j >= lo) & (j < hi))
        cond = c if cond is None else (cond | c)
    return cond


def _proj_epilogue(kinds, acc_ref, rot_ref, scl_ref, o_ref, cos_ref, sin_ref, qkg_ref, dil, pitch, tm):
    rows_per_class = tm // dil
    blocks = []
    for c in range(dil):
        for m0 in range(0, rows_per_class, EPI_CHUNK):
            if dil == 1:
                blocks.append((pl.ds(m0, EPI_CHUNK), pl.ds(m0, EPI_CHUNK), c, m0))
            else:
                natural = pl.ds(c + dil * m0, EPI_CHUNK, stride=dil)
                blocks.append((natural if pitch is None else pl.ds(c * pitch + m0, EPI_CHUNK), natural, c, m0))
    half = HEAD_DIM // 2

    def pass1(h, kind):
        if kind[0] not in ("norm_rope", "norm"):
            return
        for src, _, c, m0 in blocks:
            tmp = pl.ds(c * rows_per_class + m0, EPI_CHUNK)
            a = acc_ref.at[h][src, :]
            ms = jnp.mean(a * a, axis=-1, keepdims=True)
            scl_ref[h, tmp, :] = jnp.broadcast_to(lax.rsqrt(ms + NORM_EPS), a.shape)
            if kind[0] == "norm_rope":
                rot_ref[h, tmp, :] = pltpu.roll(a, half, 1)

    def pass2():
        gains = {}
        for kind in kinds:
            if kind[0] in ("norm_rope", "norm") and kind[1:] not in gains:
                g = qkg_ref[kind[1]:kind[1] + 1, :]
                if kind[2]:
                    g = g * (SCALE * LOG2E)
                gains[kind[1:]] = (g, pltpu.roll(g, half, 1))
        for src, tab, c, m0 in blocks:
            tmp = pl.ds(c * rows_per_class + m0, EPI_CHUNK)
            tables = {}
            for h, kind in enumerate(kinds):
                cols = slice(h * HEAD_DIM, (h + 1) * HEAD_DIM)
                a = acc_ref.at[h][src, :]
                if kind[0] == "norm_rope":
                    if kind[1:] not in tables:
                        g, g_rot = gains[kind[1:]]
                        tables[kind[1:]] = (g * cos_ref[tab, :], g_rot * sin_ref[tab, :])
                    t_cos, t_sin = tables[kind[1:]]
                    y = scl_ref[h, tmp, :] * (a * t_cos + rot_ref[h, tmp, :] * t_sin)
                elif kind[0] == "norm":
                    y = a * scl_ref[h, tmp, :] * gains[kind[1:]][0]
                else:
                    y = a
                if dil == 1:
                    o_ref[pl.ds(m0, EPI_CHUNK), cols] = y.astype(o_ref.dtype)
                else:
                    o_ref[c, pl.ds(m0, EPI_CHUNK), cols] = y.astype(o_ref.dtype)

    for h, kind in enumerate(kinds):
        pass1(h, kind)
    pass2()


def _proj_body(h_ref, w_ref, cos_ref, sin_ref, qkg_ref, o_ref, acc_ref, *norm_scratch, n_cols, groups, dil, pitch, tm):
    rot_ref, scl_ref = norm_scratch if norm_scratch else (None, None)
    s = pl.program_id(0)
    je = jnp.maximum(s - 1, 0) % n_cols
    parity = s % 2

    @pl.when(s == 0)
    def _():
        acc_ref[1] = jnp.zeros(acc_ref.shape[1:], F32)

    def step(kinds, p):
        _proj_epilogue(kinds, acc_ref.at[1 - p], rot_ref, scl_ref, o_ref, cos_ref, sin_ref, qkg_ref, dil, pitch, tm)
        res = jnp.dot(h_ref[...], w_ref[...], preferred_element_type=F32)
        for h in range(len(kinds)):
            cols = slice(h * HEAD_DIM, (h + 1) * HEAD_DIM)
            if pitch is None:
                acc_ref[p, h] = res[:, cols]
                continue
            for r0 in range(0, tm, SUBLANES):
                dst = pl.ds((r0 % dil) * pitch + r0 // dil, SUBLANES, stride=pitch)
                acc_ref.at[p, h][dst, :] = res[r0:r0 + SUBLANES, cols]

    for kinds, tiles in groups:
        for p in range(2):
            cond = parity == p
            if len(groups) > 1:
                cond = cond & _in_tiles(je, tiles)
            pl.when(cond)(functools.partial(step, kinds, p))


def _proj(h, w_all, layer, w_tiles, cos, sin_signed, qkg, bsz, seq, dil, name, tn=TN, tm=TM):
    t = h.shape[0]
    heads = tn // HEAD_DIM
    n_cols = len(w_tiles)
    w_base = w_tiles[0]
    w_step = w_tiles[1] - w_tiles[0] if n_cols > 1 else 0
    assert list(w_tiles) == [w_base + w_step * j for j in range(n_cols)]
    pos_tiles = seq // tm
    n_tiles = (t // tm) * n_cols
    groups = _tile_groups(w_tiles, heads)
    needs_norm = any(k[0] in ("norm_rope", "norm") for kinds, _ in groups for k in kinds)
    head_buf = pltpu.VMEM((heads, tm, HEAD_DIM), F32)
    pitch = tm // dil + SUBLANES if dil % (2 * SUBLANES) == 0 else None
    acc_rows = tm if pitch is None else dil * pitch

    dot_tile = lambda s: jnp.minimum(s, n_tiles - 1)
    epi_tile = lambda s: jnp.maximum(s - 1, 0)
    if dil == 1:
        out_shape = jax.ShapeDtypeStruct((t, n_cols * tn), BF16)
        out_spec = pl.BlockSpec((tm, tn), lambda s: (epi_tile(s) // n_cols, epi_tile(s) % n_cols))
    else:
        out_shape = jax.ShapeDtypeStruct((bsz, dil, seq // dil, n_cols * tn), BF16)
        out_spec = pl.BlockSpec(
            (None, dil, tm // dil, tn),
            lambda s: ((epi_tile(s) // n_cols) // pos_tiles, 0, (epi_tile(s) // n_cols) % pos_tiles,
                       epi_tile(s) % n_cols))
    table_spec = pl.BlockSpec((tm, HEAD_DIM), lambda s: ((epi_tile(s) // n_cols) % pos_tiles, 0))
    return pl.pallas_call(
        functools.partial(_proj_body, n_cols=n_cols, groups=groups, dil=dil, pitch=pitch, tm=tm),
        out_shape=out_shape,
        grid=(n_tiles + 1,),
        in_specs=[
            pl.BlockSpec((tm, D_MODEL), lambda s: (dot_tile(s) // n_cols, 0)),
            pl.BlockSpec((None, D_MODEL, tn), lambda s: (layer, 0, w_base + w_step * (dot_tile(s) % n_cols))),
            table_spec, table_spec,
            pl.BlockSpec((8, HEAD_DIM), lambda s: (0, 0)),
        ],
        out_specs=out_spec,
        scratch_shapes=[pltpu.VMEM((2, heads, acc_rows, HEAD_DIM), F32)] + ([head_buf, head_buf] if needs_norm else []),
        compiler_params=pltpu.CompilerParams(dimension_semantics=("arbitrary",), vmem_limit_bytes=56 * MIB),
        name=name,
    )(h, w_all, cos, sin_signed, qkg)


def _band_bias(radius):
    r = np.arange(ATT_Q)[:, None]
    c = np.arange(ATT_Q + 2 * radius)[None, :]
    band = np.abs(c - radius - r) <= radius
    lo_ok = c >= radius
    hi_ok = c < ATT_Q + radius
    variants = [band & lo_ok, band, band & hi_ok, band & lo_ok & hi_ok]
    return np.stack([np.where(v, 0.0, NEG) for v in variants]).astype(np.float32)


def _band_variant(i, n_tiles):
    if n_tiles == 1:
        return 3
    return jnp.where(i == 0, 0, jnp.where(i == n_tiles - 1, 2, 1))


def _halo_specs(block, index, radius, n_rows, q_rows=ATT_Q):
    per = q_rows // radius
    last = n_rows // radius - 1
    return [
        pl.BlockSpec(block(radius), lambda *g: index(g, jnp.maximum(g[-1] * per - 1, 0))),
        pl.BlockSpec(block(q_rows), lambda *g: index(g, g[-1])),
        pl.BlockSpec(block(radius), lambda *g: index(g, jnp.minimum(g[-1] * per + per, last))),
    ]


def _attn_a_body(sink_ref, q_ref, kp_ref, kc_ref, kn_ref, vp_ref, vc_ref, vn_ref, bias_ref, o_ref):
    head_cols = lambda h: slice(h * HEAD_DIM, (h + 1) * HEAD_DIM)
    kcat = [jnp.concatenate([r[:, head_cols(kv)] for r in (kp_ref, kc_ref, kn_ref)], axis=0)
            for kv in range(A_KV_HEADS)]
    vcat = [jnp.concatenate([r[:, head_cols(kv)] for r in (vp_ref, vc_ref, vn_ref)], axis=0)
            for kv in range(A_KV_HEADS)]

    def finish(h, o, m, l):
        o_ref[:, head_cols(h)] = o.astype(o_ref.dtype)

    _pipelined_attention(
        A_Q_HEADS,
        scores=lambda h: _qk(q_ref[:, head_cols(h)], kcat[h // A_GROUP]) + bias_ref[...],
        values=lambda h: vcat[h // A_GROUP],
        finish=finish,
        extra_logit=lambda h: sink_ref[h] * LOG2E)


def _attn_a(qkv, sink, bias):
    bsz, seq, _ = qkv.shape
    n_tiles = seq // ATT_Q
    block = lambda r: (None, r, A_KV_W)
    k_specs = _halo_specs(block, lambda g, rb: (g[0], rb, OFF_KA // A_KV_W), A_RADIUS, seq)
    v_specs = _halo_specs(block, lambda g, rb: (g[0], rb, OFF_VA // A_KV_W), A_RADIUS, seq)
    return pl.pallas_call(
        _attn_a_body,
        out_shape=jax.ShapeDtypeStruct((bsz, seq, A_Q_W), BF16),
        grid=(bsz, n_tiles),
        in_specs=[
            pl.BlockSpec(memory_space=pltpu.SMEM),
            pl.BlockSpec((None, ATT_Q, A_Q_W), lambda b, i: (b, i, 0)),
            *k_specs, *v_specs,
            pl.BlockSpec((None, ATT_Q, ATT_Q + 2 * A_RADIUS), lambda b, i: (_band_variant(i, n_tiles), 0, 0)),
        ],
        out_specs=pl.BlockSpec((None, ATT_Q, A_Q_W), lambda b, i: (b, i, 0)),
        compiler_params=_params(2, 32),
        name="attn_a",
    )(sink, qkv, qkv, qkv, qkv, qkv, qkv, qkv, bias)


def _attn_b_body(q_ref, kp_ref, kc_ref, kn_ref, vp_ref, vc_ref, vn_ref, *rest):
    *bias_refs, o_ref, lse_ref = rest
    n_classes, n_qt = q_ref.shape[0], len(bias_refs)
    head_cols = lambda u: slice((u % B_HEADS_PER_GROUP) * HEAD_DIM, (u % B_HEADS_PER_GROUP + 1) * HEAD_DIM)
    tile = lambda u: (u // B_HEADS_PER_GROUP) % n_qt
    cls = lambda u: u // (B_HEADS_PER_GROUP * n_qt)
    q_rows = lambda u: slice(tile(u) * ATT_Q, (tile(u) + 1) * ATT_Q)

    def keys(refs, u):
        full = jnp.concatenate([r[cls(u), :, head_cols(u)] for r in refs], axis=0)
        return full[tile(u) * ATT_Q:(tile(u) + 1) * ATT_Q + 2 * B_RADIUS, :]

    def finish(u, o, m, l):
        o_ref[cls(u), q_rows(u), head_cols(u)] = o.astype(o_ref.dtype)
        lse_ref[cls(u), q_rows(u), head_cols(u)] = jnp.broadcast_to(m + jnp.log2(l), o.shape)

    _pipelined_attention(
        n_classes * n_qt * B_HEADS_PER_GROUP,
        scores=lambda u: (_qk(q_ref[cls(u), q_rows(u), head_cols(u)], keys((kp_ref, kc_ref, kn_ref), u))
                          + bias_refs[tile(u)][...]),
        values=lambda u: keys((vp_ref, vc_ref, vn_ref), u),
        finish=finish)


def _attn_b_group(qkv, bias, group):
    bsz, dil, m, _ = qkv.shape
    n_tiles = m // ATT_Q
    cb = min(dil, B_CLASS_BLOCK)
    qt = B_CLASS_BLOCK // cb
    q_rows = qt * ATT_Q
    block = lambda r: (None, cb, r, B_OUT_W)
    k_specs = _halo_specs(block, lambda g, rb: (g[0], g[1], rb, 1), B_RADIUS, m, q_rows)
    v_specs = _halo_specs(block, lambda g, rb: (g[0], g[1], rb, 2), B_RADIUS, m, q_rows)
    io_spec = pl.BlockSpec(block(q_rows), lambda b, c, i: (b, c, i, 0))
    bias_specs = [pl.BlockSpec((None, ATT_Q, ATT_Q + 2 * B_RADIUS),
                               lambda b, c, i, t=t: (_band_variant(i * qt + t, n_tiles), 0, 0)) for t in range(qt)]
    return pl.pallas_call(
        _attn_b_body,
        out_shape=[jax.ShapeDtypeStruct((bsz, dil, m, B_OUT_W), BF16),
                   jax.ShapeDtypeStruct((bsz, dil, m, B_OUT_W), F32)],
        grid=(bsz, dil // cb, n_tiles // qt),
        in_specs=[io_spec, *k_specs, *v_specs, *bias_specs],
        out_specs=[io_spec, io_spec],
        compiler_params=_params(3, 32),
        name=f"attn_b{group}",
    )(qkv, qkv, qkv, qkv, qkv, qkv, qkv, *([bias] * qt))


def _b_combine_body(*refs):
    in_refs = refs[:2 * B_GROUPS]
    out_ref, nat_o, nat_l = refs[2 * B_GROUPS:]
    o0_ref, l0_ref = in_refs[0], in_refs[1]
    for gi in range(1, B_GROUPS):
        o_ref, l_ref = in_refs[2 * gi], in_refs[2 * gi + 1]
        dil = B_PATTERNS[gi][1]
        for c in range(dil):
            dst = pl.ds(c, TM // dil, stride=dil)
            for h in range(B_HEADS_PER_GROUP):
                cols = slice(h * HEAD_DIM, (h + 1) * HEAD_DIM)
                nat_o.at[gi - 1, h][dst, :] = o_ref[c, :, cols].astype(F32)
                nat_l.at[gi - 1, h][dst, :] = l_ref[c, :, cols]

    def body(ci, carry):
        rows = pl.ds(pl.multiple_of(ci * COMBINE_CHUNK, COMBINE_CHUNK), COMBINE_CHUNK)
        for h in range(B_HEADS_PER_GROUP):
            cols = slice(h * HEAD_DIM, (h + 1) * HEAD_DIM)
            lses = [l0_ref[0, rows, cols]] + [nat_l[gi, h, rows, :] for gi in range(B_GROUPS - 1)]
            outs = [o0_ref[0, rows, cols].astype(F32)] + [nat_o[gi, h, rows, :] for gi in range(B_GROUPS - 1)]
            mx = functools.reduce(jnp.maximum, lses)
            ws = [jnp.exp2(l - mx) for l in lses]
            num = functools.reduce(lambda a, b: a + b, [w * o for w, o in zip(ws, outs)])
            den = functools.reduce(lambda a, b: a + b, ws)
            out_ref[rows, cols] = (num * (1.0 / den)).astype(out_ref.dtype)
        return carry

    lax.fori_loop(0, TM // COMBINE_CHUNK, body, 0)


def _b_combine(parts, seq):
    assert B_PATTERNS[0][1] == 1
    bsz = parts[0][0].shape[0]
    in_specs, args = [], []
    for (o, lse), (_, dil) in zip(parts, B_PATTERNS):
        spec = pl.BlockSpec((None, dil, TM // dil, B_OUT_W), lambda b, i: (b, 0, i, 0))
        in_specs += [spec, spec]
        args += [o, lse]
    nat_shape = (B_GROUPS - 1, B_HEADS_PER_GROUP, TM, HEAD_DIM)
    return pl.pallas_call(
        _b_combine_body,
        out_shape=jax.ShapeDtypeStruct((bsz, seq, B_OUT_W), BF16),
        grid=(bsz, seq // TM),
        in_specs=in_specs,
        out_specs=pl.BlockSpec((None, TM, B_OUT_W), lambda b, i: (b, i, 0)),
        scratch_shapes=[pltpu.VMEM(nat_shape, F32), pltpu.VMEM(nat_shape, F32)],
        compiler_params=_params(2, 40),
        name="b_combine",
    )(*args)


N_DROW = 2 * C_WIN_ROWS - 1
N_DCOL = 2 * C_WIN_COLS - 1
C_KEY_TILES = 3 * C_TILE_ROWS


def _c_bias_body(rpb_ref, o_ref):
    h = pl.program_id(0)
    qc = lax.broadcasted_iota(jnp.int32, (GRID_W, GRID_W), 0)
    kc = lax.broadcasted_iota(jnp.int32, (GRID_W, GRID_W), 1)
    dcol = jnp.clip(kc - qc, -(C_WIN_COLS - 1), C_WIN_COLS - 1) + (C_WIN_COLS - 1)
    col_start = jnp.clip(qc - C_WIN_COLS // 2, 0, GRID_W - C_WIN_COLS)
    col_ok = (kc >= col_start) & (kc < col_start + C_WIN_COLS)
    tiles = []
    for dr in range(N_DROW):
        t = jnp.zeros((GRID_W, GRID_W), F32)
        for d in range(N_DCOL):
            t = jnp.where(dcol == d, rpb_ref[(h * N_DROW + dr) * N_DCOL + d] * LOG2E, t)
        tiles.append(jnp.where(col_ok, t, NEG))
    masked = jnp.full((GRID_W, GRID_W), NEG, F32)
    half = C_WIN_ROWS // 2
    for var in range(3):
        for qi in range(C_TILE_ROWS):
            row = []
            for kj in range(C_KEY_TILES):
                if var == 0:
                    ok = C_TILE_ROWS <= kj < C_TILE_ROWS + C_WIN_ROWS
                elif var == 2:
                    ok = 2 * C_TILE_ROWS - C_WIN_ROWS <= kj < 2 * C_TILE_ROWS
                else:
                    ok = 0 <= kj - C_TILE_ROWS - qi + half < C_WIN_ROWS
                drow = kj - C_TILE_ROWS - qi
                row.append(tiles[drow + C_WIN_ROWS - 1] if ok else masked)
            o_ref[var, qi * GRID_W:(qi + 1) * GRID_W, :] = jnp.concatenate(row, axis=1)


def _c_bias(rpb_flat):
    return pl.pallas_call(
        _c_bias_body,
        out_shape=jax.ShapeDtypeStruct((3, C_HEADS, ATT_Q, C_KEY_TILES * GRID_W), F32),
        grid=(C_HEADS,),
        in_specs=[pl.BlockSpec(memory_space=pltpu.SMEM)],
        out_specs=pl.BlockSpec((3, None, ATT_Q, C_KEY_TILES * GRID_W), lambda h: (0, h, 0, 0)),
        compiler_params=_params(1, 32),
        name="c_bias",
    )(rpb_flat)


def _attn_c_body(q_ref, kp_ref, kc_ref, kn_ref, vp_ref, vc_ref, vn_ref, bias_ref, o_ref):
    head_cols = lambda h: slice(h * HEAD_DIM, (h + 1) * HEAD_DIM)
    cat = lambda refs, h: jnp.concatenate([r[:, head_cols(h)] for r in refs], axis=0)

    def finish(h, o, m, l):
        o_ref[:, head_cols(h)] = o.astype(o_ref.dtype)

    _pipelined_attention(
        C_HEADS,
        scores=lambda h: _qk(q_ref[:, head_cols(h)], cat((kp_ref, kc_ref, kn_ref), h)) + bias_ref[h],
        values=lambda h: cat((vp_ref, vc_ref, vn_ref), h),
        finish=finish)


def _attn_c(qkv, bias):
    bsz, seq, _ = qkv.shape
    n_tiles = seq // ATT_Q
    block = lambda r: (None, r, C_W)
    k_specs = _halo_specs(block, lambda g, rb: (g[0], rb, 1), ATT_Q, seq)
    v_specs = _halo_specs(block, lambda g, rb: (g[0], rb, 2), ATT_Q, seq)
    return pl.pallas_call(
        _attn_c_body,
        out_shape=jax.ShapeDtypeStruct((bsz, seq, C_W), BF16),
        grid=(bsz, n_tiles),
        in_specs=[
            pl.BlockSpec(block(ATT_Q), lambda b, i: (b, i, 0)),
            *k_specs, *v_specs,
            pl.BlockSpec((None, C_HEADS, ATT_Q, C_KEY_TILES * GRID_W),
                         lambda b, i: (jnp.where(i == 0, 0, jnp.where(i == n_tiles - 1, 2, 1)), 0, 0, 0)),
        ],
        out_specs=pl.BlockSpec(block(ATT_Q), lambda b, i: (b, i, 0)),
        compiler_params=_params(2, 40),
        name="attn_c",
    )(qkv, qkv, qkv, qkv, qkv, qkv, qkv, bias)


def _merge_body(h_ref, wga_ref, wgb_ref, wgc_ref, oa_ref, ob_ref, oc_ref, wa_ref, wb_ref, wc_ref, o_ref):
    h = h_ref[...]
    m = None
    for wg_ref, o_in_ref, w_ref in ((wga_ref, oa_ref, wa_ref), (wgb_ref, ob_ref, wb_ref), (wgc_ref, oc_ref, wc_ref)):
        logit = jnp.dot(h, wg_ref[...], preferred_element_type=F32)
        gate = 1.0 / (1.0 + jnp.exp(-logit))
        term = gate * jnp.dot(o_in_ref[...], w_ref[...], preferred_element_type=F32)
        m = term if m is None else m + term
    o_ref[...] = m.astype(o_ref.dtype)


def _merge(h, w_in, oa, ob, oc, wa, wb, wc, layer):
    t = oa.shape[0]
    gate_w = lambda k: pl.BlockSpec((None, D_MODEL, TN), lambda i, j: (layer, 0, (OFF_GATE + k * D_MODEL) // TN + j))
    wspec = lambda rows: pl.BlockSpec((None, rows, TN), lambda i, j: (layer, 0, j))
    return pl.pallas_call(
        _merge_body,
        out_shape=jax.ShapeDtypeStruct((t, D_MODEL), BF16),
        grid=(t // TM, D_MODEL // TN),
        in_specs=[
            pl.BlockSpec((TM, D_MODEL), lambda i, j: (i, 0)),
            gate_w(0), gate_w(1), gate_w(2),
            pl.BlockSpec((TM, A_Q_W), lambda i, j: (i, 0)),
            pl.BlockSpec((TM, B_OUT_W), lambda i, j: (i, 0)),
            pl.BlockSpec((TM, C_W), lambda i, j: (i, 0)),
            wspec(A_Q_W), wspec(B_OUT_W), wspec(C_W),
        ],
        out_specs=pl.BlockSpec((TM, TN), lambda i, j: (i, j)),
        compiler_params=_params(2, 56),
        name="merge",
    )(h, w_in, w_in, w_in, oa, ob, oc, wa, wb, wc)


def _residual_rows_body(a_ref, w_ref, x_ref, g_ref, xo_ref, h_ref, y_ref, rs_ref, *, tm):
    s = pl.program_id(0)
    parity = s % 2

    @pl.when(s == 0)
    def _():
        y_ref[1] = jnp.zeros(y_ref.shape[1:], F32)

    def step(p):
        prev = y_ref.at[1 - p]
        chunks = [pl.ds(r0, NORM_CHUNK) for r0 in range(0, tm, NORM_CHUNK)]
        for rows in chunks:
            y = prev[rows, :]
            ms = jnp.mean(y * y, axis=-1, keepdims=True)
            rs_ref[rows, :] = jnp.broadcast_to(lax.rsqrt(ms + NORM_EPS), (NORM_CHUNK, HEAD_DIM))
        for rows in chunks:
            rs = pltpu.repeat(rs_ref[rows, :], D_MODEL // HEAD_DIM, axis=1)
            h_ref[rows, :] = (prev[rows, :] * rs * g_ref[...]).astype(h_ref.dtype)
        y = x_ref[...] + jnp.dot(a_ref[...], w_ref[...], preferred_element_type=F32)
        xo_ref[...] = y
        y_ref[p] = y

    for p in range(2):
        pl.when(parity == p)(functools.partial(step, p))


def _residual_rows_plain_body(a_ref, w_ref, x_ref, xo_ref):
    xo_ref[...] = x_ref[...] + jnp.dot(a_ref[...], w_ref[...], preferred_element_type=F32)


def _residual_rows(a, w_all, layer, x, g, tm, name):
    t, k = a.shape
    n_tiles = t // tm
    w_spec = pl.BlockSpec((None, k, D_MODEL), lambda s: (layer, 0, 0), pipeline_mode=pl.Buffered(1))
    if g is None:
        row = lambda s: (s, 0)
        return pl.pallas_call(
            _residual_rows_plain_body,
            out_shape=jax.ShapeDtypeStruct((t, D_MODEL), F32),
            grid=(n_tiles,),
            in_specs=[pl.BlockSpec((tm, k), row), w_spec, pl.BlockSpec((tm, D_MODEL), row)],
            out_specs=pl.BlockSpec((tm, D_MODEL), row),
            compiler_params=_params(1, 56),
            name=name,
        )(a, w_all, x), None
    cur = lambda s: (jnp.minimum(s, n_tiles - 1), 0)
    prev = lambda s: (jnp.maximum(s - 1, 0), 0)
    return pl.pallas_call(
        functools.partial(_residual_rows_body, tm=tm),
        out_shape=[jax.ShapeDtypeStruct((t, D_MODEL), F32), jax.ShapeDtypeStruct((t, D_MODEL), BF16)],
        grid=(n_tiles + 1,),
        in_specs=[
            pl.BlockSpec((tm, k), cur),
            w_spec,
            pl.BlockSpec((tm, D_MODEL), cur),
            pl.BlockSpec((1, D_MODEL), lambda s: (0, 0)),
        ],
        out_specs=[pl.BlockSpec((tm, D_MODEL), cur), pl.BlockSpec((tm, D_MODEL), prev)],
        scratch_shapes=[pltpu.VMEM((2, tm, D_MODEL), F32), pltpu.VMEM((tm, HEAD_DIM), F32)],
        compiler_params=pltpu.CompilerParams(dimension_semantics=("arbitrary",), vmem_limit_bytes=56 * MIB),
        name=name,
    )(a, w_all, x, g)


def _ffn_up_body(h_ref, wg_ref, wu_ref, o_ref):
    h = h_ref[...]
    gt = jnp.dot(h, wg_ref[...], preferred_element_type=F32)
    up = jnp.dot(h, wu_ref[...], preferred_element_type=F32)
    o_ref[...] = (gt * (1.0 / (1.0 + jnp.exp(-gt))) * up).astype(o_ref.dtype)


def _ffn_up(h, w_all, layer):
    t = h.shape[0]
    n_tiles = D_FF // TN
    return pl.pallas_call(
        _ffn_up_body,
        out_shape=jax.ShapeDtypeStruct((t, D_FF), BF16),
        grid=(t // TM_FFN, n_tiles),
        in_specs=[
            pl.BlockSpec((TM_FFN, D_MODEL), lambda i, j: (i, 0)),
            pl.BlockSpec((None, D_MODEL, TN), lambda i, j: (layer, 0, j)),
            pl.BlockSpec((None, D_MODEL, TN), lambda i, j: (layer, 0, n_tiles + j)),
        ],
        out_specs=pl.BlockSpec((TM_FFN, TN), lambda i, j: (i, j)),
        compiler_params=_params(2, 52),
        name="ffn_up",
    )(h, w_all, w_all)


def _rope_tables(n):
    half = HEAD_DIM // 2
    inv_freq = ROPE_THETA ** (-jnp.arange(half, dtype=F32) * 2.0 / HEAD_DIM)
    ang = jnp.arange(n, dtype=F32)[:, None] * inv_freq[None, :]
    cos, sin = jnp.cos(ang), jnp.sin(ang)
    return jnp.concatenate([cos, cos], axis=-1), jnp.concatenate([-sin, sin], axis=-1)


def kernel(x, norm1_g, w_in, qk_norm_g, sink_a, rpb_c, w_br_a, w_br_b, w_br_c, w_o,
           norm2_g, w_gate_up, w_down):
    bsz, seq, d = x.shape
    assert d == D_MODEL and seq % TM_ROPE == 0 and seq // ATT_Q >= 2 and (bsz * seq) % TM_FFN == 0
    assert seq // GRID_W >= C_WIN_ROWS and (seq // B_PATTERNS[-1][1]) % ATT_Q == 0
    t = bsz * seq
    cos, sin_signed = _rope_tables(seq)
    bias_a = jnp.asarray(_band_bias(A_RADIUS))
    bias_b = jnp.asarray(_band_bias(B_RADIUS))
    w_in, w_br_a, w_br_b, w_br_c, w_o, w_gate_up, w_down = (
        w.astype(BF16) for w in (w_in, w_br_a, w_br_b, w_br_c, w_o, w_gate_up, w_down))
    qkg = jnp.pad(qk_norm_g, ((0, 0), (0, 2), (0, 0)))
    tile = lambda off, tn=TN: off // tn
    wide = lambda width: TN_WIDE if width % TN_WIDE == 0 else TN
    xf = x.reshape(t, d)
    h = _norm1(xf, norm1_g[0][None])
    for layer in range(DEPTH):
        proj = functools.partial(_proj, h, w_in, layer, cos=cos, sin_signed=sin_signed, qkg=qkg[layer],
                                 bsz=bsz, seq=seq)
        qkv_a = proj(range(tile(OFF_QA), tile(OFF_QB)), dil=1, name="proj_a", tm=TM_ROPE)
        oa = _attn_a(qkv_a.reshape(bsz, seq, -1), sink_a[layer], bias_a)
        parts = []
        for group, (_, dil) in enumerate(B_PATTERNS):
            tiles = [tile(off) + group for off in (OFF_QB, OFF_KB, OFF_VB)]
            qkv_b = proj(tiles, dil=dil, name=f"proj_b{group}", tm=TM_ROPE)
            if dil == 1:
                qkv_b = qkv_b.reshape(bsz, 1, seq, -1)
            parts.append(_attn_b_group(qkv_b, bias_b, group))
        ob = _b_combine(parts, seq)
        tn_c = wide(C_W)
        qkv_c = proj(range(tile(OFF_QC, tn_c), tile(OFF_GATE, tn_c)), dil=1, name="proj_c", tn=tn_c)
        oc = _attn_c(qkv_c.reshape(bsz, seq, -1), _c_bias(rpb_c[layer].reshape(-1)))
        merged = _merge(h, w_in, oa.reshape(t, A_Q_W), ob.reshape(t, B_OUT_W), oc.reshape(t, C_W),
                        w_br_a, w_br_b, w_br_c, layer)
        xf, h2 = _residual_rows(merged, w_o, layer, xf, norm2_g[layer][None], TM_OUT, "out_proj")
        act = _ffn_up(h2, w_gate_up, layer)
        next_g = norm1_g[layer + 1][None] if layer + 1 < DEPTH else None
        xf, h = _residual_rows(act, w_down, layer, xf, next_g, TM_DOWN, "ffn_down")
    return xf.reshape(bsz, seq, d)
```

```python
import functools

import numpy as np
import jax
import jax.numpy as jnp
from jax import lax
from jax.experimental import pallas as pl
from jax.experimental.pallas import tpu as pltpu

F32 = jnp.float32
BF16 = jnp.bfloat16

D_MODEL = 2048
DEPTH = 4
HEAD_DIM = 128
SUBLANES = 8
ROPE_THETA = 10000.0
NORM_EPS = 1e-6
SCALE = HEAD_DIM ** -0.5
LOG2E = 1.4426950408889634
NEG = -1e30

A_Q_HEADS = 8
A_KV_HEADS = 2
A_GROUP = A_Q_HEADS // A_KV_HEADS
A_RADIUS = 128
B_PATTERNS = ((128, 1), (512, 4), (2048, 16))
B_GROUPS = len(B_PATTERNS)
B_HEADS_PER_GROUP = 4
B_RADIUS = 64
C_HEADS = 8
GRID_W = 64
C_WIN_ROWS = 8
C_WIN_COLS = 16

A_Q_W = A_Q_HEADS * HEAD_DIM
A_KV_W = A_KV_HEADS * HEAD_DIM
B_W = B_GROUPS * B_HEADS_PER_GROUP * HEAD_DIM
B_OUT_W = B_HEADS_PER_GROUP * HEAD_DIM
C_W = C_HEADS * HEAD_DIM
N_IN = A_Q_W + 2 * A_KV_W + 3 * B_W + 3 * C_W + 3 * D_MODEL
D_FF = ((8 * D_MODEL + 3 * 256 - 1) // (3 * 256)) * 256

OFF_QA = 0
OFF_KA = OFF_QA + A_Q_W
OFF_VA = OFF_KA + A_KV_W
OFF_QB = OFF_VA + A_KV_W
OFF_KB = OFF_QB + B_W
OFF_VB = OFF_KB + B_W
OFF_QC = OFF_VB + B_W
OFF_KC = OFF_QC + C_W
OFF_VC = OFF_KC + C_W
OFF_GATE = OFF_VC + C_W

TM = 1024
TN = 512
TN_WIDE = 1024
TM_OUT = 512
TM_DOWN = 256
TM_FFN = 2048
TM_ROPE = 1024
EPI_CHUNK = 64
NORM_CHUNK = 64
ATT_Q = 256
B_CLASS_BLOCK = 2
C_TILE_ROWS = ATT_Q // GRID_W
COMBINE_CHUNK = 128
MIB = 1024 * 1024


def _params(n_axes, vmem_mib):
    return pltpu.CompilerParams(
        dimension_semantics=("parallel",) * n_axes,
        vmem_limit_bytes=vmem_mib * MIB,
    )


def _rmsnorm_rows(x_ref, g_ref, h_ref):
    rows = x_ref.shape[0]

    def body(c, carry):
        r0 = pl.multiple_of(c * NORM_CHUNK, NORM_CHUNK)
        x = x_ref[pl.ds(r0, NORM_CHUNK), :]
        ms = jnp.mean(x * x, axis=-1, keepdims=True)
        h_ref[pl.ds(r0, NORM_CHUNK), :] = (x * lax.rsqrt(ms + NORM_EPS) * g_ref[...]).astype(h_ref.dtype)
        return carry

    lax.fori_loop(0, rows // NORM_CHUNK, body, 0)


def _pipelined_attention(n_units, scores, values, finish, extra_logit=None):
    s_next = scores(0)
    prev = None
    for u in range(n_units + 1):
        s = s_next
        if u + 1 < n_units:
            s_next = scores(u + 1)
        cur = None
        if u < n_units:
            m = jnp.max(s, axis=-1, keepdims=True)
            if extra_logit is not None:
                m = jnp.maximum(m, extra_logit(u))
            p = jnp.exp2(s - m)
            l = jnp.sum(p, axis=-1, keepdims=True)
            if extra_logit is not None:
                l = l + jnp.exp2(extra_logit(u) - m)
            cur = (p.astype(BF16), m, l)
        if prev is not None:
            p_prev, m_prev, l_prev = prev
            o = jnp.dot(p_prev, values(u - 1), preferred_element_type=F32) * (1.0 / l_prev)
            finish(u - 1, o, m_prev, l_prev)
        prev = cur


def _qk(q, k):
    return lax.dot_general(q, k, (((1,), (1,)), ((), ())), preferred_element_type=F32)


def _norm1_body(x_ref, g_ref, o_ref):
    _rmsnorm_rows(x_ref, g_ref, o_ref)


def _norm1(x, g):
    t = x.shape[0]
    return pl.pallas_call(
        _norm1_body,
        out_shape=jax.ShapeDtypeStruct((t, D_MODEL), BF16),
        grid=(t // TM,),
        in_specs=[pl.BlockSpec((TM, D_MODEL), lambda i: (i, 0)),
                  pl.BlockSpec((1, D_MODEL), lambda i: (0, 0))],
        out_specs=pl.BlockSpec((TM, D_MODEL), lambda i: (i, 0)),
        compiler_params=_params(1, 40),
        name="norm1",
    )(x, g)


def _head_kind(head):
    c = head * HEAD_DIM
    if c < OFF_KA:
        return ("norm_rope", 0, True)
    if c < OFF_VA:
        return ("norm_rope", 1, False)
    if c < OFF_QB:
        return ("copy",)
    if c < OFF_KB:
        return ("norm_rope", 2, True)
    if c < OFF_VB:
        return ("norm_rope", 3, False)
    if c < OFF_QC:
        return ("copy",)
    if c < OFF_KC:
        return ("norm", 4, True)
    if c < OFF_VC:
        return ("norm", 5, False)
    assert c < OFF_GATE
    return ("copy",)


def _tile_groups(w_tiles, heads):
    groups = {}
    for j, wt in enumerate(w_tiles):
        kinds = tuple(_head_kind(wt * heads + h) for h in range(heads))
        groups.setdefault(kinds, []).append(j)
    return list(groups.items())


def _in_tiles(j, tiles):
    runs = []
    for t in tiles:
        if runs and runs[-1][1] == t:
            runs[-1][1] = t + 1
        else:
            runs.append([t, t + 1])
    cond = None
    for lo, hi in runs:
        c = (j == lo) if hi == lo + 1 else ((j >= lo) & (j < hi))
        cond = c if cond is None else (cond | c)
    return cond


def _proj_epilogue(kinds, acc_ref, rot_ref, scl_ref, o_ref, cos_ref, sin_ref, qkg_ref, dil, pitch, tm):
    rows_per_class = tm // dil
    blocks = []
    for c in range(dil):
        for m0 in range(0, rows_per_class, EPI_CHUNK):
            if dil == 1:
                blocks.append((pl.ds(m0, EPI_CHUNK), pl.ds(m0, EPI_CHUNK), c, m0))
            else:
                natural = pl.ds(c + dil * m0, EPI_CHUNK, stride=dil)
                blocks.append((natural if pitch is None else pl.ds(c * pitch + m0, EPI_CHUNK), natural, c, m0))
    half = HEAD_DIM // 2

    def pass1(h, kind):
        if kind[0] not in ("norm_rope", "norm"):
            return
        for src, _, c, m0 in blocks:
            tmp = pl.ds(c * rows_per_class + m0, EPI_CHUNK)
            a = acc_ref.at[h][src, :]
            ms = jnp.mean(a * a, axis=-1, keepdims=True)
            scl_ref[h, tmp, :] = jnp.broadcast_to(lax.rsqrt(ms + NORM_EPS), a.shape)
            if kind[0] == "norm_rope":
                rot_ref[h, tmp, :] = pltpu.roll(a, half, 1)

    def pass2():
        gains = {}
        for kind in kinds:
            if kind[0] in ("norm_rope", "norm") and kind[1:] not in gains:
                g = qkg_ref[kind[1]:kind[1] + 1, :]
                if kind[2]:
                    g = g * (SCALE * LOG2E)
                gains[kind[1:]] = (g, pltpu.roll(g, half, 1))
        for src, tab, c, m0 in blocks:
            tmp = pl.ds(c * rows_per_class + m0, EPI_CHUNK)
            tables = {}
            for h, kind in enumerate(kinds):
                cols = slice(h * HEAD_DIM, (h + 1) * HEAD_DIM)
                a = acc_ref.at[h][src, :]
                if kind[0] == "norm_rope":
                    if kind[1:] not in tables:
                        g, g_rot = gains[kind[1:]]
                        tables[kind[1:]] = (g * cos_ref[tab, :], g_rot * sin_ref[tab, :])
                    t_cos, t_sin = tables[kind[1:]]
                    y = scl_ref[h, tmp, :] * (a * t_cos + rot_ref[h, tmp, :] * t_sin)
                elif kind[0] == "norm":
                    y = a * scl_ref[h, tmp, :] * gains[kind[1:]][0]
                else:
                    y = a
                if dil == 1:
                    o_ref[pl.ds(m0, EPI_CHUNK), cols] = y.astype(o_ref.dtype)
                else:
                    o_ref[c, pl.ds(m0, EPI_CHUNK), cols] = y.astype(o_ref.dtype)

    for h, kind in enumerate(kinds):
        pass1(h, kind)
    pass2()


def _proj_body(h_ref, w_ref, cos_ref, sin_ref, qkg_ref, o_ref, acc_ref, *norm_scratch, n_cols, groups, dil, pitch, tm):
    rot_ref, scl_ref = norm_scratch if norm_scratch else (None, None)
    s = pl.program_id(0)
    je = jnp.maximum(s - 1, 0) % n_cols
    parity = s % 2

    @pl.when(s == 0)
    def _():
        acc_ref[1] = jnp.zeros(acc_ref.shape[1:], F32)

    def step(kinds, p):
        _proj_epilogue(kinds, acc_ref.at[1 - p], rot_ref, scl_ref, o_ref, cos_ref, sin_ref, qkg_ref, dil, pitch, tm)
        res = jnp.dot(h_ref[...], w_ref[...], preferred_element_type=F32)
        for h in range(len(kinds)):
            cols = slice(h * HEAD_DIM, (h + 1) * HEAD_DIM)
            if pitch is None:
                acc_ref[p, h] = res[:, cols]
                continue
            for r0 in range(0, tm, SUBLANES):
                dst = pl.ds((r0 % dil) * pitch + r0 // dil, SUBLANES, stride=pitch)
                acc_ref.at[p, h][dst, :] = res[r0:r0 + SUBLANES, cols]

    for kinds, tiles in groups:
        for p in range(2):
            cond = parity == p
            if len(groups) > 1:
                cond = cond & _in_tiles(je, tiles)
            pl.when(cond)(functools.partial(step, kinds, p))


def _proj(h, w_all, layer, w_tiles, cos, sin_signed, qkg, bsz, seq, dil, name, tn=TN, tm=TM):
    t = h.shape[0]
    heads = tn // HEAD_DIM
    n_cols = len(w_tiles)
    w_base = w_tiles[0]
    w_step = w_tiles[1] - w_tiles[0] if n_cols > 1 else 0
    assert list(w_tiles) == [w_base + w_step * j for j in range(n_cols)]
    pos_tiles = seq // tm
    n_tiles = (t // tm) * n_cols
    groups = _tile_groups(w_tiles, heads)
    needs_norm = any(k[0] in ("norm_rope", "norm") for kinds, _ in groups for k in kinds)
    head_buf = pltpu.VMEM((heads, tm, HEAD_DIM), F32)
    pitch = tm // dil + SUBLANES if dil % (2 * SUBLANES) == 0 else None
    acc_rows = tm if pitch is None else dil * pitch

    dot_tile = lambda s: jnp.minimum(s, n_tiles - 1)
    epi_tile = lambda s: jnp.maximum(s - 1, 0)
    if dil == 1:
        out_shape = jax.ShapeDtypeStruct((t, n_cols * tn), BF16)
        out_spec = pl.BlockSpec((tm, tn), lambda s: (epi_tile(s) // n_cols, epi_tile(s) % n_cols))
    else:
        out_shape = jax.ShapeDtypeStruct((bsz, dil, seq // dil, n_cols * tn), BF16)
        out_spec = pl.BlockSpec(
            (None, dil, tm // dil, tn),
            lambda s: ((epi_tile(s) // n_cols) // pos_tiles, 0, (epi_tile(s) // n_cols) % pos_tiles,
                       epi_tile(s) % n_cols))
    table_spec = pl.BlockSpec((tm, HEAD_DIM), lambda s: ((epi_tile(s) // n_cols) % pos_tiles, 0))
    return pl.pallas_call(
        functools.partial(_proj_body, n_cols=n_cols, groups=groups, dil=dil, pitch=pitch, tm=tm),
        out_shape=out_shape,
        grid=(n_tiles + 1,),
        in_specs=[
            pl.BlockSpec((tm, D_MODEL), lambda s: (dot_tile(s) // n_cols, 0)),
            pl.BlockSpec((None, D_MODEL, tn), lambda s: (layer, 0, w_base + w_step * (dot_tile(s) % n_cols))),
            table_spec, table_spec,
            pl.BlockSpec((8, HEAD_DIM), lambda s: (0, 0)),
        ],
        out_specs=out_spec,
        scratch_shapes=[pltpu.VMEM((2, heads, acc_rows, HEAD_DIM), F32)] + ([head_buf, head_buf] if needs_norm else []),
        compiler_params=pltpu.CompilerParams(dimension_semantics=("arbitrary",), vmem_limit_bytes=56 * MIB),
        name=name,
    )(h, w_all, cos, sin_signed, qkg)


def _band_bias(radius):
    r = np.arange(ATT_Q)[:, None]
    c = np.arange(ATT_Q + 2 * radius)[None, :]
    band = np.abs(c - radius - r) <= radius
    lo_ok = c >= radius
    hi_ok = c < ATT_Q + radius
    variants = [band & lo_ok, band, band & hi_ok, band & lo_ok & hi_ok]
    return np.stack([np.where(v, 0.0, NEG) for v in variants]).astype(np.float32)


def _band_variant(i, n_tiles):
    if n_tiles == 1:
        return 3
    return jnp.where(i == 0, 0, jnp.where(i == n_tiles - 1, 2, 1))


def _halo_specs(block, index, radius, n_rows, q_rows=ATT_Q):
    per = q_rows // radius
    last = n_rows // radius - 1
    return [
        pl.BlockSpec(block(radius), lambda *g: index(g, jnp.maximum(g[-1] * per - 1, 0))),
        pl.BlockSpec(block(q_rows), lambda *g: index(g, g[-1])),
        pl.BlockSpec(block(radius), lambda *g: index(g, jnp.minimum(g[-1] * per + per, last))),
    ]


def _attn_a_body(sink_ref, q_ref, kp_ref, kc_ref, kn_ref, vp_ref, vc_ref, vn_ref, bias_ref, o_ref):
    head_cols = lambda h: slice(h * HEAD_DIM, (h + 1) * HEAD_DIM)
    kcat = [jnp.concatenate([r[:, head_cols(kv)] for r in (kp_ref, kc_ref, kn_ref)], axis=0)
            for kv in range(A_KV_HEADS)]
    vcat = [jnp.concatenate([r[:, head_cols(kv)] for r in (vp_ref, vc_ref, vn_ref)], axis=0)
            for kv in range(A_KV_HEADS)]

    def finish(h, o, m, l):
        o_ref[:, head_cols(h)] = o.astype(o_ref.dtype)

    _pipelined_attention(
        A_Q_HEADS,
        scores=lambda h: _qk(q_ref[:, head_cols(h)], kcat[h // A_GROUP]) + bias_ref[...],
        values=lambda h: vcat[h // A_GROUP],
        finish=finish,
        extra_logit=lambda h: sink_ref[h] * LOG2E)


def _attn_a(qkv, sink, bias):
    bsz, seq, _ = qkv.shape
    n_tiles = seq // ATT_Q
    block = lambda r: (None, r, A_KV_W)
    k_specs = _halo_specs(block, lambda g, rb: (g[0], rb, OFF_KA // A_KV_W), A_RADIUS, seq)
    v_specs = _halo_specs(block, lambda g, rb: (g[0], rb, OFF_VA // A_KV_W), A_RADIUS, seq)
    return pl.pallas_call(
        _attn_a_body,
        out_shape=jax.ShapeDtypeStruct((bsz, seq, A_Q_W), BF16),
        grid=(bsz, n_tiles),
        in_specs=[
            pl.BlockSpec(memory_space=pltpu.SMEM),
            pl.BlockSpec((None, ATT_Q, A_Q_W), lambda b, i: (b, i, 0)),
            *k_specs, *v_specs,
            pl.BlockSpec((None, ATT_Q, ATT_Q + 2 * A_RADIUS), lambda b, i: (_band_variant(i, n_tiles), 0, 0)),
        ],
        out_specs=pl.BlockSpec((None, ATT_Q, A_Q_W), lambda b, i: (b, i, 0)),
        compiler_params=_params(2, 32),
        name="attn_a",
    )(sink, qkv, qkv, qkv, qkv, qkv, qkv, qkv, bias)


def _attn_b_body(q_ref, kp_ref, kc_ref, kn_ref, vp_ref, vc_ref, vn_ref, *rest):
    *bias_refs, o_ref, lse_ref = rest
    n_classes, n_qt = q_ref.shape[0], len(bias_refs)
    head_cols = lambda u: slice((u % B_HEADS_PER_GROUP) * HEAD_DIM, (u % B_HEADS_PER_GROUP + 1) * HEAD_DIM)
    tile = lambda u: (u // B_HEADS_PER_GROUP) % n_qt
    cls = lambda u: u // (B_HEADS_PER_GROUP * n_qt)
    q_rows = lambda u: slice(tile(u) * ATT_Q, (tile(u) + 1) * ATT_Q)

    def keys(refs, u):
        full = jnp.concatenate([r[cls(u), :, head_cols(u)] for r in refs], axis=0)
        return full[tile(u) * ATT_Q:(tile(u) + 1) * ATT_Q + 2 * B_RADIUS, :]

    def finish(u, o, m, l):
        o_ref[cls(u), q_rows(u), head_cols(u)] = o.astype(o_ref.dtype)
        lse_ref[cls(u), q_rows(u), head_cols(u)] = jnp.broadcast_to(m + jnp.log2(l), o.shape)

    _pipelined_attention(
        n_classes * n_qt * B_HEADS_PER_GROUP,
        scores=lambda u: (_qk(q_ref[cls(u), q_rows(u), head_cols(u)], keys((kp_ref, kc_ref, kn_ref), u))
                          + bias_refs[tile(u)][...]),
        values=lambda u: keys((vp_ref, vc_ref, vn_ref), u),
        finish=finish)


def _attn_b_group(qkv, bias, group):
    bsz, dil, m, _ = qkv.shape
    n_tiles = m // ATT_Q
    cb = min(dil, B_CLASS_BLOCK)
    qt = B_CLASS_BLOCK // cb
    q_rows = qt * ATT_Q
    block = lambda r: (None, cb, r, B_OUT_W)
    k_specs = _halo_specs(block, lambda g, rb: (g[0], g[1], rb, 1), B_RADIUS, m, q_rows)
    v_specs = _halo_specs(block, lambda g, rb: (g[0], g[1], rb, 2), B_RADIUS, m, q_rows)
    io_spec = pl.BlockSpec(block(q_rows), lambda b, c, i: (b, c, i, 0))
    bias_specs = [pl.BlockSpec((None, ATT_Q, ATT_Q + 2 * B_RADIUS),
                               lambda b, c, i, t=t: (_band_variant(i * qt + t, n_tiles), 0, 0)) for t in range(qt)]
    return pl.pallas_call(
        _attn_b_body,
        out_shape=[jax.ShapeDtypeStruct((bsz, dil, m, B_OUT_W), BF16),
                   jax.ShapeDtypeStruct((bsz, dil, m, B_OUT_W), F32)],
        grid=(bsz, dil // cb, n_tiles // qt),
        in_specs=[io_spec, *k_specs, *v_specs, *bias_specs],
        out_specs=[io_spec, io_spec],
        compiler_params=_params(3, 32),
        name=f"attn_b{group}",
    )(qkv, qkv, qkv, qkv, qkv, qkv, qkv, *([bias] * qt))


def _b_combine_body(*refs):
    in_refs = refs[:2 * B_GROUPS]
    out_ref, nat_o, nat_l = refs[2 * B_GROUPS:]
    o0_ref, l0_ref = in_refs[0], in_refs[1]
    for gi in range(1, B_GROUPS):
        o_ref, l_ref = in_refs[2 * gi], in_refs[2 * gi + 1]
        dil = B_PATTERNS[gi][1]
        for c in range(dil):
            dst = pl.ds(c, TM // dil, stride=dil)
            for h in range(B_HEADS_PER_GROUP):
                cols = slice(h * HEAD_DIM, (h + 1) * HEAD_DIM)
                nat_o.at[gi - 1, h][dst, :] = o_ref[c, :, cols].astype(F32)
                nat_l.at[gi - 1, h][dst, :] = l_ref[c, :, cols]

    def body(ci, carry):
        rows = pl.ds(pl.multiple_of(ci * COMBINE_CHUNK, COMBINE_CHUNK), COMBINE_CHUNK)
        for h in range(B_HEADS_PER_GROUP):
            cols = slice(h * HEAD_DIM, (h + 1) * HEAD_DIM)
            lses = [l0_ref[0, rows, cols]] + [nat_l[gi, h, rows, :] for gi in range(B_GROUPS - 1)]
            outs = [o0_ref[0, rows, cols].astype(F32)] + [nat_o[gi, h, rows, :] for gi in range(B_GROUPS - 1)]
            mx = functools.reduce(jnp.maximum, lses)
            ws = [jnp.exp2(l - mx) for l in lses]
            num = functools.reduce(lambda a, b: a + b, [w * o for w, o in zip(ws, outs)])
            den = functools.reduce(lambda a, b: a + b, ws)
            out_ref[rows, cols] = (num * (1.0 / den)).astype(out_ref.dtype)
        return carry

    lax.fori_loop(0, TM // COMBINE_CHUNK, body, 0)


def _b_combine(parts, seq):
    assert B_PATTERNS[0][1] == 1
    bsz = parts[0][0].shape[0]
    in_specs, args = [], []
    for (o, lse), (_, dil) in zip(parts, B_PATTERNS):
        spec = pl.BlockSpec((None, dil, TM // dil, B_OUT_W), lambda b, i: (b, 0, i, 0))
        in_specs += [spec, spec]
        args += [o, lse]
    nat_shape = (B_GROUPS - 1, B_HEADS_PER_GROUP, TM, HEAD_DIM)
    return pl.pallas_call(
        _b_combine_body,
        out_shape=jax.ShapeDtypeStruct((bsz, seq, B_OUT_W), BF16),
        grid=(bsz, seq // TM),
        in_specs=in_specs,
        out_specs=pl.BlockSpec((None, TM, B_OUT_W), lambda b, i: (b, i, 0)),
        scratch_shapes=[pltpu.VMEM(nat_shape, F32), pltpu.VMEM(nat_shape, F32)],
        compiler_params=_params(2, 40),
        name="b_combine",
    )(*args)


N_DROW = 2 * C_WIN_ROWS - 1
N_DCOL = 2 * C_WIN_COLS - 1
C_KEY_TILES = 3 * C_TILE_ROWS


def _c_bias_body(rpb_ref, o_ref):
    h = pl.program_id(0)
    qc = lax.broadcasted_iota(jnp.int32, (GRID_W, GRID_W), 0)
    kc = lax.broadcasted_iota(jnp.int32, (GRID_W, GRID_W), 1)
    dcol = jnp.clip(kc - qc, -(C_WIN_COLS - 1), C_WIN_COLS - 1) + (C_WIN_COLS - 1)
    col_start = jnp.clip(qc - C_WIN_COLS // 2, 0, GRID_W - C_WIN_COLS)
    col_ok = (kc >= col_start) & (kc < col_start + C_WIN_COLS)
    tiles = []
    for dr in range(N_DROW):
        t = jnp.zeros((GRID_W, GRID_W), F32)
        for d in range(N_DCOL):
            t = jnp.where(dcol == d, rpb_ref[(h * N_DROW + dr) * N_DCOL + d] * LOG2E, t)
        tiles.append(jnp.where(col_ok, t, NEG))
    masked = jnp.full((GRID_W, GRID_W), NEG, F32)
    half = C_WIN_ROWS // 2
    for var in range(3):
        for qi in range(C_TILE_ROWS):
            row = []
            for kj in range(C_KEY_TILES):
                if var == 0:
                    ok = C_TILE_ROWS <= kj < C_TILE_ROWS + C_WIN_ROWS
                elif var == 2:
                    ok = 2 * C_TILE_ROWS - C_WIN_ROWS <= kj < 2 * C_TILE_ROWS
                else:
                    ok = 0 <= kj - C_TILE_ROWS - qi + half < C_WIN_ROWS
                drow = kj - C_TILE_ROWS - qi
                row.append(tiles[drow + C_WIN_ROWS - 1] if ok else masked)
            o_ref[var, qi * GRID_W:(qi + 1) * GRID_W, :] = jnp.concatenate(row, axis=1)


def _c_bias(rpb_flat):
    return pl.pallas_call(
        _c_bias_body,
        out_shape=jax.ShapeDtypeStruct((3, C_HEADS, ATT_Q, C_KEY_TILES * GRID_W), F32),
        grid=(C_HEADS,),
        in_specs=[pl.BlockSpec(memory_space=pltpu.SMEM)],
        out_specs=pl.BlockSpec((3, None, ATT_Q, C_KEY_TILES * GRID_W), lambda h: (0, h, 0, 0)),
        compiler_params=_params(1, 32),
        name="c_bias",
    )(rpb_flat)


def _attn_c_body(q_ref, kp_ref, kc_ref, kn_ref, vp_ref, vc_ref, vn_ref, bias_ref, o_ref):
    head_cols = lambda h: slice(h * HEAD_DIM, (h + 1) * HEAD_DIM)
    cat = lambda refs, h: jnp.concatenate([r[:, head_cols(h)] for r in refs], axis=0)

    def finish(h, o, m, l):
        o_ref[:, head_cols(h)] = o.astype(o_ref.dtype)

    _pipelined_attention(
        C_HEADS,
        scores=lambda h: _qk(q_ref[:, head_cols(h)], cat((kp_ref, kc_ref, kn_ref), h)) + bias_ref[h],
        values=lambda h: cat((vp_ref, vc_ref, vn_ref), h),
        finish=finish)


def _attn_c(qkv, bias):
    bsz, seq, _ = qkv.shape
    n_tiles = seq // ATT_Q
    block = lambda r: (None, r, C_W)
    k_specs = _halo_specs(block, lambda g, rb: (g[0], rb, 1), ATT_Q, seq)
    v_specs = _halo_specs(block, lambda g, rb: (g[0], rb, 2), ATT_Q, seq)
    return pl.pallas_call(
        _attn_c_body,
        out_shape=jax.ShapeDtypeStruct((bsz, seq, C_W), BF16),
        grid=(bsz, n_tiles),
        in_specs=[
            pl.BlockSpec(block(ATT_Q), lambda b, i: (b, i, 0)),
            *k_specs, *v_specs,
            pl.BlockSpec((None, C_HEADS, ATT_Q, C_KEY_TILES * GRID_W),
                         lambda b, i: (jnp.where(i == 0, 0, jnp.where(i == n_tiles - 1, 2, 1)), 0, 0, 0)),
        ],
        out_specs=pl.BlockSpec(block(ATT_Q), lambda b, i: (b, i, 0)),
        compiler_params=_params(2, 40),
        name="attn_c",
    )(qkv, qkv, qkv, qkv, qkv, qkv, qkv, bias)


def _merge_body(h_ref, wga_ref, wgb_ref, wgc_ref, oa_ref, ob_ref, oc_ref, wa_ref, wb_ref, wc_ref, o_ref):
    h = h_ref[...]
    m = None
    for wg_ref, o_in_ref, w_ref in ((wga_ref, oa_ref, wa_ref), (wgb_ref, ob_ref, wb_ref), (wgc_ref, oc_ref, wc_ref)):
        logit = jnp.dot(h, wg_ref[...], preferred_element_type=F32)
        gate = 1.0 / (1.0 + jnp.exp(-logit))
        term = gate * jnp.dot(o_in_ref[...], w_ref[...], preferred_element_type=F32)
        m = term if m is None else m + term
    o_ref[...] = m.astype(o_ref.dtype)


def _merge(h, w_in, oa, ob, oc, wa, wb, wc, layer):
    t = oa.shape[0]
    gate_w = lambda k: pl.BlockSpec((None, D_MODEL, TN), lambda i, j: (layer, 0, (OFF_GATE + k * D_MODEL) // TN + j))
    wspec = lambda rows: pl.BlockSpec((None, rows, TN), lambda i, j: (layer, 0, j))
    return pl.pallas_call(
        _merge_body,
        out_shape=jax.ShapeDtypeStruct((t, D_MODEL), BF16),
        grid=(t // TM, D_MODEL // TN),
        in_specs=[
            pl.BlockSpec((TM, D_MODEL), lambda i, j: (i, 0)),
            gate_w(0), gate_w(1), gate_w(2),
            pl.BlockSpec((TM, A_Q_W), lambda i, j: (i, 0)),
            pl.BlockSpec((TM, B_OUT_W), lambda i, j: (i, 0)),
            pl.BlockSpec((TM, C_W), lambda i, j: (i, 0)),
            wspec(A_Q_W), wspec(B_OUT_W), wspec(C_W),
        ],
        out_specs=pl.BlockSpec((TM, TN), lambda i, j: (i, j)),
        compiler_params=_params(2, 56),
        name="merge",
    )(h, w_in, w_in, w_in, oa, ob, oc, wa, wb, wc)


def _residual_rows_body(a_ref, w_ref, x_ref, g_ref, xo_ref, h_ref, y_ref, rs_ref, *, tm):
    s = pl.program_id(0)
    parity = s % 2

    @pl.when(s == 0)
    def _():
        y_ref[1] = jnp.zeros(y_ref.shape[1:], F32)

    def step(p):
        prev = y_ref.at[1 - p]
        chunks = [pl.ds(r0, NORM_CHUNK) for r0 in range(0, tm, NORM_CHUNK)]
        for rows in chunks:
            y = prev[rows, :]
            ms = jnp.mean(y * y, axis=-1, keepdims=True)
            rs_ref[rows, :] = jnp.broadcast_to(lax.rsqrt(ms + NORM_EPS), (NORM_CHUNK, HEAD_DIM))
        for rows in chunks:
            rs = pltpu.repeat(rs_ref[rows, :], D_MODEL // HEAD_DIM, axis=1)
            h_ref[rows, :] = (prev[rows, :] * rs * g_ref[...]).astype(h_ref.dtype)
        y = x_ref[...] + jnp.dot(a_ref[...], w_ref[...], preferred_element_type=F32)
        xo_ref[...] = y
        y_ref[p] = y

    for p in range(2):
        pl.when(parity == p)(functools.partial(step, p))


def _residual_rows_plain_body(a_ref, w_ref, x_ref, xo_ref):
    xo_ref[...] = x_ref[...] + jnp.dot(a_ref[...], w_ref[...], preferred_element_type=F32)


def _residual_rows(a, w_all, layer, x, g, tm, name):
    t, k = a.shape
    n_tiles = t // tm
    w_spec = pl.BlockSpec((None, k, D_MODEL), lambda s: (layer, 0, 0), pipeline_mode=pl.Buffered(1))
    if g is None:
        row = lambda s: (s, 0)
        return pl.pallas_call(
            _residual_rows_plain_body,
            out_shape=jax.ShapeDtypeStruct((t, D_MODEL), F32),
            grid=(n_tiles,),
            in_specs=[pl.BlockSpec((tm, k), row), w_spec, pl.BlockSpec((tm, D_MODEL), row)],
            out_specs=pl.BlockSpec((tm, D_MODEL), row),
            compiler_params=_params(1, 56),
            name=name,
        )(a, w_all, x), None
    cur = lambda s: (jnp.minimum(s, n_tiles - 1), 0)
    prev = lambda s: (jnp.maximum(s - 1, 0), 0)
    return pl.pallas_call(
        functools.partial(_residual_rows_body, tm=tm),
        out_shape=[jax.ShapeDtypeStruct((t, D_MODEL), F32), jax.ShapeDtypeStruct((t, D_MODEL), BF16)],
        grid=(n_tiles + 1,),
        in_specs=[
            pl.BlockSpec((tm, k), cur),
            w_spec,
            pl.BlockSpec((tm, D_MODEL), cur),
            pl.BlockSpec((1, D_MODEL), lambda s: (0, 0)),
        ],
        out_specs=[pl.BlockSpec((tm, D_MODEL), cur), pl.BlockSpec((tm, D_MODEL), prev)],
        scratch_shapes=[pltpu.VMEM((2, tm, D_MODEL), F32), pltpu.VMEM((tm, HEAD_DIM), F32)],
        compiler_params=pltpu.CompilerParams(dimension_semantics=("arbitrary",), vmem_limit_bytes=56 * MIB),
        name=name,
    )(a, w_all, x, g)


def _ffn_up_body(h_ref, wg_ref, wu_ref, o_ref):
    h = h_ref[...]
    gt = jnp.dot(h, wg_ref[...], preferred_element_type=F32)
    up = jnp.dot(h, wu_ref[...], preferred_element_type=F32)
    o_ref[...] = (gt * (1.0 / (1.0 + jnp.exp(-gt))) * up).astype(o_ref.dtype)


def _ffn_up(h, w_all, layer):
    t = h.shape[0]
    n_tiles = D_FF // TN
    return pl.pallas_call(
        _ffn_up_body,
        out_shape=jax.ShapeDtypeStruct((t, D_FF), BF16),
        grid=(t // TM_FFN, n_tiles),
        in_specs=[
            pl.BlockSpec((TM_FFN, D_MODEL), lambda i, j: (i, 0)),
            pl.BlockSpec((None, D_MODEL, TN), lambda i, j: (layer, 0, j)),
            pl.BlockSpec((None, D_MODEL, TN), lambda i, j: (layer, 0, n_tiles + j)),
        ],
        out_specs=pl.BlockSpec((TM_FFN, TN), lambda i, j: (i, j)),
        compiler_params=_params(2, 52),
        name="ffn_up",
    )(h, w_all, w_all)


def _rope_tables(n):
    half = HEAD_DIM // 2
    inv_freq = ROPE_THETA ** (-jnp.arange(half, dtype=F32) * 2.0 / HEAD_DIM)
    ang = jnp.arange(n, dtype=F32)[:, None] * inv_freq[None, :]
    cos, sin = jnp.cos(ang), jnp.sin(ang)
    return jnp.concatenate([cos, cos], axis=-1), jnp.concatenate([-sin, sin], axis=-1)


def kernel(x, norm1_g, w_in, qk_norm_g, sink_a, rpb_c, w_br_a, w_br_b, w_br_c, w_o,
           norm2_g, w_gate_up, w_down):
    bsz, seq, d = x.shape
    assert d == D_MODEL and seq % TM_ROPE == 0 and seq // ATT_Q >= 2 and (bsz * seq) % TM_FFN == 0
    assert seq // GRID_W >= C_WIN_ROWS and (seq // B_PATTERNS[-1][1]) % ATT_Q == 0
    t = bsz * seq
    cos, sin_signed = _rope_tables(seq)
    bias_a = jnp.asarray(_band_bias(A_RADIUS))
    bias_b = jnp.asarray(_band_bias(B_RADIUS))
    w_in, w_br_a, w_br_b, w_br_c, w_o, w_gate_up, w_down = (
        w.astype(BF16) for w in (w_in, w_br_a, w_br_b, w_br_c, w_o, w_gate_up, w_down))
    qkg = jnp.pad(qk_norm_g, ((0, 0), (0, 2), (0, 0)))
    tile = lambda off, tn=TN: off // tn
    wide = lambda width: TN_WIDE if width % TN_WIDE == 0 else TN
    xf = x.reshape(t, d)
    h = _norm1(xf, norm1_g[0][None])
    for layer in range(DEPTH):
        proj = functools.partial(_proj, h, w_in, layer, cos=cos, sin_signed=sin_signed, qkg=qkg[layer],
                                 bsz=bsz, seq=seq)
        qkv_a = proj(range(tile(OFF_QA), tile(OFF_QB)), dil=1, name="proj_a", tm=TM_ROPE)
        oa = _attn_a(qkv_a.reshape(bsz, seq, -1), sink_a[layer], bias_a)
        parts = []
        for group, (_, dil) in enumerate(B_PATTERNS):
            tiles = [tile(off) + group for off in (OFF_QB, OFF_KB, OFF_VB)]
            qkv_b = proj(tiles, dil=dil, name=f"proj_b{group}", tm=TM_ROPE)
            if dil == 1:
                qkv_b = qkv_b.reshape(bsz, 1, seq, -1)
            parts.append(_attn_b_group(qkv_b, bias_b, group))
        ob = _b_combine(parts, seq)
        tn_c = wide(C_W)
        qkv_c = proj(range(tile(OFF_QC, tn_c), tile(OFF_GATE, tn_c)), dil=1, name="proj_c", tn=tn_c)
        oc = _attn_c(qkv_c.reshape(bsz, seq, -1), _c_bias(rpb_c[layer].reshape(-1)))
        merged = _merge(h, w_in, oa.reshape(t, A_Q_W), ob.reshape(t, B_OUT_W), oc.reshape(t, C_W),
                        w_br_a, w_br_b, w_br_c, layer)
        xf, h2 = _residual_rows(merged, w_o, layer, xf, norm2_g[layer][None], TM_OUT, "out_proj")
        act = _ffn_up(h2, w_gate_up, layer)
        next_g = norm1_g[layer + 1][None] if layer + 1 < DEPTH else None
        xf, h = _residual_rows(act, w_down, layer, xf, next_g, TM_DOWN, "ffn_down")
    return xf.reshape(bsz, seq, d)
```

```python
import functools

import numpy as np
import jax
import jax.numpy as jnp
from jax import lax
from jax.experimental import pallas as pl
from jax.experimental.pallas import tpu as pltpu

F32 = jnp.float32
BF16 = jnp.bfloat16

D_MODEL = 2048
DEPTH = 4
HEAD_DIM = 128
SUBLANES = 8
ROPE_THETA = 10000.0
NORM_EPS = 1e-6
SCALE = HEAD_DIM ** -0.5
LOG2E = 1.4426950408889634
NEG = -1e30

A_Q_HEADS = 8
A_KV_HEADS = 2
A_GROUP = A_Q_HEADS // A_KV_HEADS
A_RADIUS = 128
B_PATTERNS = ((128, 1), (512, 4), (2048, 16))
B_GROUPS = len(B_PATTERNS)
B_HEADS_PER_GROUP = 4
B_RADIUS = 64
C_HEADS = 8
GRID_W = 64
C_WIN_ROWS = 8
C_WIN_COLS = 16

A_Q_W = A_Q_HEADS * HEAD_DIM
A_KV_W = A_KV_HEADS * HEAD_DIM
B_W = B_GROUPS * B_HEADS_PER_GROUP * HEAD_DIM
B_OUT_W = B_HEADS_PER_GROUP * HEAD_DIM
C_W = C_HEADS * HEAD_DIM
N_IN = A_Q_W + 2 * A_KV_W + 3 * B_W + 3 * C_W + 3 * D_MODEL
D_FF = ((8 * D_MODEL + 3 * 256 - 1) // (3 * 256)) * 256

OFF_QA = 0
OFF_KA = OFF_QA + A_Q_W
OFF_VA = OFF_KA + A_KV_W
OFF_QB = OFF_VA + A_KV_W
OFF_KB = OFF_QB + B_W
OFF_VB = OFF_KB + B_W
OFF_QC = OFF_VB + B_W
OFF_KC = OFF_QC + C_W
OFF_VC = OFF_KC + C_W
OFF_GATE = OFF_VC + C_W

TM = 1024
TN = 512
TN_WIDE = 1024
TM_OUT = 512
TM_DOWN = 256
TM_FFN = 2048
TM_ROPE = 1024
EPI_CHUNK = 64
NORM_CHUNK = 64
ATT_Q = 256
B_CLASS_BLOCK = 4
C_TILE_ROWS = ATT_Q // GRID_W
COMBINE_CHUNK = 128
MIB = 1024 * 1024


def _params(n_axes, vmem_mib):
    return pltpu.CompilerParams(
        dimension_semantics=("parallel",) * n_axes,
        vmem_limit_bytes=vmem_mib * MIB,
    )


def _rmsnorm_rows(x_ref, g_ref, h_ref):
    rows = x_ref.shape[0]

    def body(c, carry):
        r0 = pl.multiple_of(c * NORM_CHUNK, NORM_CHUNK)
        x = x_ref[pl.ds(r0, NORM_CHUNK), :]
        ms = jnp.mean(x * x, axis=-1, keepdims=True)
        h_ref[pl.ds(r0, NORM_CHUNK), :] = (x * lax.rsqrt(ms + NORM_EPS) * g_ref[...]).astype(h_ref.dtype)
        return carry

    lax.fori_loop(0, rows // NORM_CHUNK, body, 0)


def _pipelined_attention(n_units, scores, values, finish, extra_logit=None):
    s_next = scores(0)
    prev = None
    for u in range(n_units + 1):
        s = s_next
        if u + 1 < n_units:
            s_next = scores(u + 1)
        cur = None
        if u < n_units:
            m = jnp.max(s, axis=-1, keepdims=True)
            if extra_logit is not None:
                m = jnp.maximum(m, extra_logit(u))
            p = jnp.exp2(s - m)
            l = jnp.sum(p, axis=-1, keepdims=True)
            if extra_logit is not None:
                l = l + jnp.exp2(extra_logit(u) - m)
            cur = (p.astype(BF16), m, l)
        if prev is not None:
            p_prev, m_prev, l_prev = prev
            o = jnp.dot(p_prev, values(u - 1), preferred_element_type=F32) * (1.0 / l_prev)
            finish(u - 1, o, m_prev, l_prev)
        prev = cur


def _qk(q, k):
    return lax.dot_general(q, k, (((1,), (1,)), ((), ())), preferred_element_type=F32)


def _norm1_body(x_ref, g_ref, o_ref):
    _rmsnorm_rows(x_ref, g_ref, o_ref)


def _norm1(x, g):
    t = x.shape[0]
    return pl.pallas_call(
        _norm1_body,
        out_shape=jax.ShapeDtypeStruct((t, D_MODEL), BF16),
        grid=(t // TM,),
        in_specs=[pl.BlockSpec((TM, D_MODEL), lambda i: (i, 0)),
                  pl.BlockSpec((1, D_MODEL), lambda i: (0, 0))],
        out_specs=pl.BlockSpec((TM, D_MODEL), lambda i: (i, 0)),
        compiler_params=_params(1, 40),
        name="norm1",
    )(x, g)


def _head_kind(head):
    c = head * HEAD_DIM
    if c < OFF_KA:
        return ("norm_rope", 0, True)
    if c < OFF_VA:
        return ("norm_rope", 1, False)
    if c < OFF_QB:
        return ("copy",)
    if c < OFF_KB:
        return ("norm_rope", 2, True)
    if c < OFF_VB:
        return ("norm_rope", 3, False)
    if c < OFF_QC:
        return ("copy",)
    if c < OFF_KC:
        return ("norm", 4, True)
    if c < OFF_VC:
        return ("norm", 5, False)
    assert c < OFF_GATE
    return ("copy",)


def _tile_groups(w_tiles, heads):
    groups = {}
    for j, wt in enumerate(w_tiles):
        kinds = tuple(_head_kind(wt * heads + h) for h in range(heads))
        groups.setdefault(kinds, []).append(j)
    return list(groups.items())


def _in_tiles(j, tiles):
    runs = []
    for t in tiles:
        if runs and runs[-1][1] == t:
            runs[-1][1] = t + 1
        else:
            runs.append([t, t + 1])
    cond = None
    for lo, hi in runs:
        c = (j == lo) if hi == lo + 1 else ((j >= lo) & (j < hi))
        cond = c if cond is None else (cond | c)
    return cond


def _proj_epilogue(kinds, acc_ref, rot_ref, scl_ref, o_ref, cos_ref, sin_ref, qkg_ref, dil, pitch, tm):
    rows_per_class = tm // dil
    blocks = []
    for c in range(dil):
        for m0 in range(0, rows_per_class, EPI_CHUNK):
            if dil == 1:
                blocks.append((pl.ds(m0, EPI_CHUNK), pl.ds(m0, EPI_CHUNK), c, m0))
            else:
                natural = pl.ds(c + dil * m0, EPI_CHUNK, stride=dil)
                blocks.append((natural if pitch is None else pl.ds(c * pitch + m0, EPI_CHUNK), natural, c, m0))
    half = HEAD_DIM // 2

    def pass1(h, kind):
        if kind[0] not in ("norm_rope", "norm"):
            return
        for src, _, c, m0 in blocks:
            tmp = pl.ds(c * rows_per_class + m0, EPI_CHUNK)
            a = acc_ref.at[h][src, :]
            ms = jnp.mean(a * a, axis=-1, keepdims=True)
            scl_ref[h, tmp, :] = jnp.broadcast_to(lax.rsqrt(ms + NORM_EPS), a.shape)
            if kind[0] == "norm_rope":
                rot_ref[h, tmp, :] = pltpu.roll(a, half, 1)

    def pass2():
        gains = {}
        for kind in kinds:
            if kind[0] in ("norm_rope", "norm") and kind[1:] not in gains:
                g = qkg_ref[kind[1]:kind[1] + 1, :]
                if kind[2]:
                    g = g * (SCALE * LOG2E)
                gains[kind[1:]] = (g, pltpu.roll(g, half, 1))
        for src, tab, c, m0 in blocks:
            tmp = pl.ds(c * rows_per_class + m0, EPI_CHUNK)
            tables = {}
            for h, kind in enumerate(kinds):
                cols = slice(h * HEAD_DIM, (h + 1) * HEAD_DIM)
                a = acc_ref.at[h][src, :]
                if kind[0] == "norm_rope":
                    if kind[1:] not in tables:
                        g, g_rot = gains[kind[1:]]
                        tables[kind[1:]] = (g * cos_ref[tab, :], g_rot * sin_ref[tab, :])
                    t_cos, t_sin = tables[kind[1:]]
                    y = scl_ref[h, tmp, :] * (a * t_cos + rot_ref[h, tmp, :] * t_sin)
                elif kind[0] == "norm":
                    y = a * scl_ref[h, tmp, :] * gains[kind[1:]][0]
                else:
                    y = a
                if dil == 1:
                    o_ref[pl.ds(m0, EPI_CHUNK), cols] = y.astype(o_ref.dtype)
                else:
                    o_ref[c, pl.ds(m0, EPI_CHUNK), cols] = y.astype(o_ref.dtype)

    for h, kind in enumerate(kinds):
        pass1(h, kind)
    pass2()


def _proj_body(h_ref, w_ref, cos_ref, sin_ref, qkg_ref, o_ref, acc_ref, *norm_scratch, n_cols, groups, dil, pitch, tm):
    rot_ref, scl_ref = norm_scratch if norm_scratch else (None, None)
    s = pl.program_id(0)
    je = jnp.maximum(s - 1, 0) % n_cols
    parity = s % 2

    @pl.when(s == 0)
    def _():
        acc_ref[1] = jnp.zeros(acc_ref.shape[1:], F32)

    def step(kinds, p):
        _proj_epilogue(kinds, acc_ref.at[1 - p], rot_ref, scl_ref, o_ref, cos_ref, sin_ref, qkg_ref, dil, pitch, tm)
        res = jnp.dot(h_ref[...], w_ref[...], preferred_element_type=F32)
        for h in range(len(kinds)):
            cols = slice(h * HEAD_DIM, (h + 1) * HEAD_DIM)
            if pitch is None:
                acc_ref[p, h] = res[:, cols]
                continue
            for r0 in range(0, tm, SUBLANES):
                dst = pl.ds((r0 % dil) * pitch + r0 // dil, SUBLANES, stride=pitch)
                acc_ref.at[p, h][dst, :] = res[r0:r0 + SUBLANES, cols]

    for kinds, tiles in groups:
        for p in range(2):
            cond = parity == p
            if len(groups) > 1:
                cond = cond & _in_tiles(je, tiles)
            pl.when(cond)(functools.partial(step, kinds, p))


def _proj(h, w_all, layer, w_tiles, cos, sin_signed, qkg, bsz, seq, dil, name, tn=TN, tm=TM):
    t = h.shape[0]
    heads = tn // HEAD_DIM
    n_cols = len(w_tiles)
    w_base = w_tiles[0]
    w_step = w_tiles[1] - w_tiles[0] if n_cols > 1 else 0
    assert list(w_tiles) == [w_base + w_step * j for j in range(n_cols)]
    pos_tiles = seq // tm
    n_tiles = (t // tm) * n_cols
    groups = _tile_groups(w_tiles, heads)
    needs_norm = any(k[0] in ("norm_rope", "norm") for kinds, _ in groups for k in kinds)
    head_buf = pltpu.VMEM((heads, tm, HEAD_DIM), F32)
    pitch = tm // dil + SUBLANES if dil % (2 * SUBLANES) == 0 else None
    acc_rows = tm if pitch is None else dil * pitch

    dot_tile = lambda s: jnp.minimum(s, n_tiles - 1)
    epi_tile = lambda s: jnp.maximum(s - 1, 0)
    if dil == 1:
        out_shape = jax.ShapeDtypeStruct((t, n_cols * tn), BF16)
        out_spec = pl.BlockSpec((tm, tn), lambda s: (epi_tile(s) // n_cols, epi_tile(s) % n_cols))
    else:
        out_shape = jax.ShapeDtypeStruct((bsz, dil, seq // dil, n_cols * tn), BF16)
        out_spec = pl.BlockSpec(
            (None, dil, tm // dil, tn),
            lambda s: ((epi_tile(s) // n_cols) // pos_tiles, 0, (epi_tile(s) // n_cols) % pos_tiles,
                       epi_tile(s) % n_cols))
    table_spec = pl.BlockSpec((tm, HEAD_DIM), lambda s: ((epi_tile(s) // n_cols) % pos_tiles, 0))
    return pl.pallas_call(
        functools.partial(_proj_body, n_cols=n_cols, groups=groups, dil=dil, pitch=pitch, tm=tm),
        out_shape=out_shape,
        grid=(n_tiles + 1,),
        in_specs=[
            pl.BlockSpec((tm, D_MODEL), lambda s: (dot_tile(s) // n_cols, 0)),
            pl.BlockSpec((None, D_MODEL, tn), lambda s: (layer, 0, w_base + w_step * (dot_tile(s) % n_cols))),
            table_spec, table_spec,
            pl.BlockSpec((8, HEAD_DIM), lambda s: (0, 0)),
        ],
        out_specs=out_spec,
        scratch_shapes=[pltpu.VMEM((2, heads, acc_rows, HEAD_DIM), F32)] + ([head_buf, head_buf] if needs_norm else []),
        compiler_params=pltpu.CompilerParams(dimension_semantics=("arbitrary",), vmem_limit_bytes=56 * MIB),
        name=name,
    )(h, w_all, cos, sin_signed, qkg)


def _band_bias(radius):
    r = np.arange(ATT_Q)[:, None]
    c = np.arange(ATT_Q + 2 * radius)[None, :]
    band = np.abs(c - radius - r) <= radius
    lo_ok = c >= radius
    hi_ok = c < ATT_Q + radius
    variants = [band & lo_ok, band, band & hi_ok, band & lo_ok & hi_ok]
    return np.stack([np.where(v, 0.0, NEG) for v in variants]).astype(np.float32)


def _band_variant(i, n_tiles):
    if n_tiles == 1:
        return 3
    return jnp.where(i == 0, 0, jnp.where(i == n_tiles - 1, 2, 1))


def _halo_specs(block, index, radius, n_rows, q_rows=ATT_Q):
    per = q_rows // radius
    last = n_rows // radius - 1
    return [
        pl.BlockSpec(block(radius), lambda *g: index(g, jnp.maximum(g[-1] * per - 1, 0))),
        pl.BlockSpec(block(q_rows), lambda *g: index(g, g[-1])),
        pl.BlockSpec(block(radius), lambda *g: index(g, jnp.minimum(g[-1] * per + per, last))),
    ]


def _attn_a_body(sink_ref, q_ref, kp_ref, kc_ref, kn_ref, vp_ref, vc_ref, vn_ref, bias_ref, o_ref):
    head_cols = lambda h: slice(h * HEAD_DIM, (h + 1) * HEAD_DIM)
    kcat = [jnp.concatenate([r[:, head_cols(kv)] for r in (kp_ref, kc_ref, kn_ref)], axis=0)
            for kv in range(A_KV_HEADS)]
    vcat = [jnp.concatenate([r[:, head_cols(kv)] for r in (vp_ref, vc_ref, vn_ref)], axis=0)
            for kv in range(A_KV_HEADS)]

    def finish(h, o, m, l):
        o_ref[:, head_cols(h)] = o.astype(o_ref.dtype)

    _pipelined_attention(
        A_Q_HEADS,
        scores=lambda h: _qk(q_ref[:, head_cols(h)], kcat[h // A_GROUP]) + bias_ref[...],
        values=lambda h: vcat[h // A_GROUP],
        finish=finish,
        extra_logit=lambda h: sink_ref[h] * LOG2E)


def _attn_a(qkv, sink, bias):
    bsz, seq, _ = qkv.shape
    n_tiles = seq // ATT_Q
    block = lambda r: (None, r, A_KV_W)
    k_specs = _halo_specs(block, lambda g, rb: (g[0], rb, OFF_KA // A_KV_W), A_RADIUS, seq)
    v_specs = _halo_specs(block, lambda g, rb: (g[0], rb, OFF_VA // A_KV_W), A_RADIUS, seq)
    return pl.pallas_call(
        _attn_a_body,
        out_shape=jax.ShapeDtypeStruct((bsz, seq, A_Q_W), BF16),
        grid=(bsz, n_tiles),
        in_specs=[
            pl.BlockSpec(memory_space=pltpu.SMEM),
            pl.BlockSpec((None, ATT_Q, A_Q_W), lambda b, i: (b, i, 0)),
            *k_specs, *v_specs,
            pl.BlockSpec((None, ATT_Q, ATT_Q + 2 * A_RADIUS), lambda b, i: (_band_variant(i, n_tiles), 0, 0)),
        ],
        out_specs=pl.BlockSpec((None, ATT_Q, A_Q_W), lambda b, i: (b, i, 0)),
        compiler_params=_params(2, 32),
        name="attn_a",
    )(sink, qkv, qkv, qkv, qkv, qkv, qkv, qkv, bias)


def _attn_b_body(q_ref, kp_ref, kc_ref, kn_ref, vp_ref, vc_ref, vn_ref, *rest):
    *bias_refs, o_ref, lse_ref = rest
    n_classes, n_qt = q_ref.shape[0], len(bias_refs)
    head_cols = lambda u: slice((u % B_HEADS_PER_GROUP) * HEAD_DIM, (u % B_HEADS_PER_GROUP + 1) * HEAD_DIM)
    tile = lambda u: (u // B_HEADS_PER_GROUP) % n_qt
    cls = lambda u: u // (B_HEADS_PER_GROUP * n_qt)
    q_rows = lambda u: slice(tile(u) * ATT_Q, (tile(u) + 1) * ATT_Q)

    def keys(refs, u):
        full = jnp.concatenate([r[cls(u), :, head_cols(u)] for r in refs], axis=0)
        return full[tile(u) * ATT_Q:(tile(u) + 1) * ATT_Q + 2 * B_RADIUS, :]

    def finish(u, o, m, l):
        o_ref[cls(u), q_rows(u), head_cols(u)] = o.astype(o_ref.dtype)
        lse_ref[cls(u), q_rows(u), head_cols(u)] = jnp.broadcast_to(m + jnp.log2(l), o.shape)

    _pipelined_attention(
        n_classes * n_qt * B_HEADS_PER_GROUP,
        scores=lambda u: (_qk(q_ref[cls(u), q_rows(u), head_cols(u)], keys((kp_ref, kc_ref, kn_ref), u))
                          + bias_refs[tile(u)][...]),
        values=lambda u: keys((vp_ref, vc_ref, vn_ref), u),
        finish=finish)


def _attn_b_group(qkv, bias, group):
    bsz, dil, m, _ = qkv.shape
    n_tiles = m // ATT_Q
    cb = min(dil, B_CLASS_BLOCK)
    qt = B_CLASS_BLOCK // cb
    q_rows = qt * ATT_Q
    block = lambda r: (None, cb, r, B_OUT_W)
    k_specs = _halo_specs(block, lambda g, rb: (g[0], g[1], rb, 1), B_RADIUS, m, q_rows)
    v_specs = _halo_specs(block, lambda g, rb: (g[0], g[1], rb, 2), B_RADIUS, m, q_rows)
    io_spec = pl.BlockSpec(block(q_rows), lambda b, c, i: (b, c, i, 0))
    bias_specs = [pl.BlockSpec((None, ATT_Q, ATT_Q + 2 * B_RADIUS),
                               lambda b, c, i, t=t: (_band_variant(i * qt + t, n_tiles), 0, 0)) for t in range(qt)]
    return pl.pallas_call(
        _attn_b_body,
        out_shape=[jax.ShapeDtypeStruct((bsz, dil, m, B_OUT_W), BF16),
                   jax.ShapeDtypeStruct((bsz, dil, m, B_OUT_W), F32)],
        grid=(bsz, dil // cb, n_tiles // qt),
        in_specs=[io_spec, *k_specs, *v_specs, *bias_specs],
        out_specs=[io_spec, io_spec],
        compiler_params=_params(3, 32),
        name=f"attn_b{group}",
    )(qkv, qkv, qkv, qkv, qkv, qkv, qkv, *([bias] * qt))


def _b_combine_body(*refs):
    in_refs = refs[:2 * B_GROUPS]
    out_ref, nat_o, nat_l = refs[2 * B_GROUPS:]
    o0_ref, l0_ref = in_refs[0], in_refs[1]
    for gi in range(1, B_GROUPS):
        o_ref, l_ref = in_refs[2 * gi], in_refs[2 * gi + 1]
        dil = B_PATTERNS[gi][1]
        for c in range(dil):
            dst = pl.ds(c, TM // dil, stride=dil)
            for h in range(B_HEADS_PER_GROUP):
                cols = slice(h * HEAD_DIM, (h + 1) * HEAD_DIM)
                nat_o.at[gi - 1, h][dst, :] = o_ref[c, :, cols].astype(F32)
                nat_l.at[gi - 1, h][dst, :] = l_ref[c, :, cols]

    def body(ci, carry):
        rows = pl.ds(pl.multiple_of(ci * COMBINE_CHUNK, COMBINE_CHUNK), COMBINE_CHUNK)
        for h in range(B_HEADS_PER_GROUP):
            cols = slice(h * HEAD_DIM, (h + 1) * HEAD_DIM)
            lses = [l0_ref[0, rows, cols]] + [nat_l[gi, h, rows, :] for gi in range(B_GROUPS - 1)]
            outs = [o0_ref[0, rows, cols].astype(F32)] + [nat_o[gi, h, rows, :] for gi in range(B_GROUPS - 1)]
            mx = functools.reduce(jnp.maximum, lses)
            ws = [jnp.exp2(l - mx) for l in lses]
            num = functools.reduce(lambda a, b: a + b, [w * o for w, o in zip(ws, outs)])
            den = functools.reduce(lambda a, b: a + b, ws)
            out_ref[rows, cols] = (num * (1.0 / den)).astype(out_ref.dtype)
        return carry

    lax.fori_loop(0, TM // COMBINE_CHUNK, body, 0)


def _b_combine(parts, seq):
    assert B_PATTERNS[0][1] == 1
    bsz = parts[0][0].shape[0]
    in_specs, args = [], []
    for (o, lse), (_, dil) in zip(parts, B_PATTERNS):
        spec = pl.BlockSpec((None, dil, TM // dil, B_OUT_W), lambda b, i: (b, 0, i, 0))
        in_specs += [spec, spec]
        args += [o, lse]
    nat_shape = (B_GROUPS - 1, B_HEADS_PER_GROUP, TM, HEAD_DIM)
    return pl.pallas_call(
        _b_combine_body,
        out_shape=jax.ShapeDtypeStruct((bsz, seq, B_OUT_W), BF16),
        grid=(bsz, seq // TM),
        in_specs=in_specs,
        out_specs=pl.BlockSpec((None, TM, B_OUT_W), lambda b, i: (b, i, 0)),
        scratch_shapes=[pltpu.VMEM(nat_shape, F32), pltpu.VMEM(nat_shape, F32)],
        compiler_params=_params(2, 40),
        name="b_combine",
    )(*args)


N_DROW = 2 * C_WIN_ROWS - 1
N_DCOL = 2 * C_WIN_COLS - 1
C_KEY_TILES = 3 * C_TILE_ROWS


def _c_bias_body(rpb_ref, o_ref):
    h = pl.program_id(0)
    qc = lax.broadcasted_iota(jnp.int32, (GRID_W, GRID_W), 0)
    kc = lax.broadcasted_iota(jnp.int32, (GRID_W, GRID_W), 1)
    dcol = jnp.clip(kc - qc, -(C_WIN_COLS - 1), C_WIN_COLS - 1) + (C_WIN_COLS - 1)
    col_start = jnp.clip(qc - C_WIN_COLS // 2, 0, GRID_W - C_WIN_COLS)
    col_ok = (kc >= col_start) & (kc < col_start + C_WIN_COLS)
    tiles = []
    for dr in range(N_DROW):
        t = jnp.zeros((GRID_W, GRID_W), F32)
        for d in range(N_DCOL):
            t = jnp.where(dcol == d, rpb_ref[(h * N_DROW + dr) * N_DCOL + d] * LOG2E, t)
        tiles.append(jnp.where(col_ok, t, NEG))
    masked = jnp.full((GRID_W, GRID_W), NEG, F32)
    half = C_WIN_ROWS // 2
    for var in range(3):
        for qi in range(C_TILE_ROWS):
            row = []
            for kj in range(C_KEY_TILES):
                if var == 0:
                    ok = C_TILE_ROWS <= kj < C_TILE_ROWS + C_WIN_ROWS
                elif var == 2:
                    ok = 2 * C_TILE_ROWS - C_WIN_ROWS <= kj < 2 * C_TILE_ROWS
                else:
                    ok = 0 <= kj - C_TILE_ROWS - qi + half < C_WIN_ROWS
                drow = kj - C_TILE_ROWS - qi
                row.append(tiles[drow + C_WIN_ROWS - 1] if ok else masked)
            o_ref[var, qi * GRID_W:(qi + 1) * GRID_W, :] = jnp.concatenate(row, axis=1)


def _c_bias(rpb_flat):
    return pl.pallas_call(
        _c_bias_body,
        out_shape=jax.ShapeDtypeStruct((3, C_HEADS, ATT_Q, C_KEY_TILES * GRID_W), F32),
        grid=(C_HEADS,),
        in_specs=[pl.BlockSpec(memory_space=pltpu.SMEM)],
        out_specs=pl.BlockSpec((3, None, ATT_Q, C_KEY_TILES * GRID_W), lambda h: (0, h, 0, 0)),
        compiler_params=_params(1, 32),
        name="c_bias",
    )(rpb_flat)


def _attn_c_body(q_ref, kp_ref, kc_ref, kn_ref, vp_ref, vc_ref, vn_ref, bias_ref, o_ref):
    head_cols = lambda h: slice(h * HEAD_DIM, (h + 1) * HEAD_DIM)
    cat = lambda refs, h: jnp.concatenate([r[:, head_cols(h)] for r in refs], axis=0)

    def finish(h, o, m, l):
        o_ref[:, head_cols(h)] = o.astype(o_ref.dtype)

    _pipelined_attention(
        C_HEADS,
        scores=lambda h: _qk(q_ref[:, head_cols(h)], cat((kp_ref, kc_ref, kn_ref), h)) + bias_ref[h],
        values=lambda h: cat((vp_ref, vc_ref, vn_ref), h),
        finish=finish)


def _attn_c(qkv, bias):
    bsz, seq, _ = qkv.shape
    n_tiles = seq // ATT_Q
    block = lambda r: (None, r, C_W)
    k_specs = _halo_specs(block, lambda g, rb: (g[0], rb, 1), ATT_Q, seq)
    v_specs = _halo_specs(block, lambda g, rb: (g[0], rb, 2), ATT_Q, seq)
    return pl.pallas_call(
        _attn_c_body,
        out_shape=jax.ShapeDtypeStruct((bsz, seq, C_W), BF16),
        grid=(bsz, n_tiles),
        in_specs=[
            pl.BlockSpec(block(ATT_Q), lambda b, i: (b, i, 0)),
            *k_specs, *v_specs,
            pl.BlockSpec((None, C_HEADS, ATT_Q, C_KEY_TILES * GRID_W),
                         lambda b, i: (jnp.where(i == 0, 0, jnp.where(i == n_tiles - 1, 2, 1)), 0, 0, 0)),
        ],
        out_specs=pl.BlockSpec(block(ATT_Q), lambda b, i: (b, i, 0)),
        compiler_params=_params(2, 40),
        name="attn_c",
    )(qkv, qkv, qkv, qkv, qkv, qkv, qkv, bias)


def _merge_body(h_ref, wga_ref, wgb_ref, wgc_ref, oa_ref, ob_ref, oc_ref, wa_ref, wb_ref, wc_ref, o_ref):
    h = h_ref[...]
    m = None
    for wg_ref, o_in_ref, w_ref in ((wga_ref, oa_ref, wa_ref), (wgb_ref, ob_ref, wb_ref), (wgc_ref, oc_ref, wc_ref)):
        logit = jnp.dot(h, wg_ref[...], preferred_element_type=F32)
        gate = 1.0 / (1.0 + jnp.exp(-logit))
        term = gate * jnp.dot(o_in_ref[...], w_ref[...], preferred_element_type=F32)
        m = term if m is None else m + term
    o_ref[...] = m.astype(o_ref.dtype)


def _merge(h, w_in, oa, ob, oc, wa, wb, wc, layer):
    t = oa.shape[0]
    gate_w = lambda k: pl.BlockSpec((None, D_MODEL, TN), lambda i, j: (layer, 0, (OFF_GATE + k * D_MODEL) // TN + j))
    wspec = lambda rows: pl.BlockSpec((None, rows, TN), lambda i, j: (layer, 0, j))
    return pl.pallas_call(
        _merge_body,
        out_shape=jax.ShapeDtypeStruct((t, D_MODEL), BF16),
        grid=(t // TM, D_MODEL // TN),
        in_specs=[
            pl.BlockSpec((TM, D_MODEL), lambda i, j: (i, 0)),
            gate_w(0), gate_w(1), gate_w(2),
            pl.BlockSpec((TM, A_Q_W), lambda i, j: (i, 0)),
            pl.BlockSpec((TM, B_OUT_W), lambda i, j: (i, 0)),
            pl.BlockSpec((TM, C_W), lambda i, j: (i, 0)),
            wspec(A_Q_W), wspec(B_OUT_W), wspec(C_W),
        ],
        out_specs=pl.BlockSpec((TM, TN), lambda i, j: (i, j)),
        compiler_params=_params(2, 56),
        name="merge",
    )(h, w_in, w_in, w_in, oa, ob, oc, wa, wb, wc)


def _residual_rows_body(a_ref, w_ref, x_ref, g_ref, xo_ref, h_ref, y_ref, rs_ref, *, tm):
    s = pl.program_id(0)
    parity = s % 2

    @pl.when(s == 0)
    def _():
        y_ref[1] = jnp.zeros(y_ref.shape[1:], F32)

    def step(p):
        prev = y_ref.at[1 - p]
        chunks = [pl.ds(r0, NORM_CHUNK) for r0 in range(0, tm, NORM_CHUNK)]
        for rows in chunks:
            y = prev[rows, :]
            ms = jnp.mean(y * y, axis=-1, keepdims=True)
            rs_ref[rows, :] = jnp.broadcast_to(lax.rsqrt(ms + NORM_EPS), (NORM_CHUNK, HEAD_DIM))
        for rows in chunks:
            rs = pltpu.repeat(rs_ref[rows, :], D_MODEL // HEAD_DIM, axis=1)
            h_ref[rows, :] = (prev[rows, :] * rs * g_ref[...]).astype(h_ref.dtype)
        y = x_ref[...] + jnp.dot(a_ref[...], w_ref[...], preferred_element_type=F32)
        xo_ref[...] = y
        y_ref[p] = y

    for p in range(2):
        pl.when(parity == p)(functools.partial(step, p))


def _residual_rows_plain_body(a_ref, w_ref, x_ref, xo_ref):
    xo_ref[...] = x_ref[...] + jnp.dot(a_ref[...], w_ref[...], preferred_element_type=F32)


def _residual_rows(a, w_all, layer, x, g, tm, name):
    t, k = a.shape
    n_tiles = t // tm
    w_spec = pl.BlockSpec((None, k, D_MODEL), lambda s: (layer, 0, 0), pipeline_mode=pl.Buffered(1))
    if g is None:
        row = lambda s: (s, 0)
        return pl.pallas_call(
            _residual_rows_plain_body,
            out_shape=jax.ShapeDtypeStruct((t, D_MODEL), F32),
            grid=(n_tiles,),
            in_specs=[pl.BlockSpec((tm, k), row), w_spec, pl.BlockSpec((tm, D_MODEL), row)],
            out_specs=pl.BlockSpec((tm, D_MODEL), row),
            compiler_params=_params(1, 56),
            name=name,
        )(a, w_all, x), None
    cur = lambda s: (jnp.minimum(s, n_tiles - 1), 0)
    prev = lambda s: (jnp.maximum(s - 1, 0), 0)
    return pl.pallas_call(
        functools.partial(_residual_rows_body, tm=tm),
        out_shape=[jax.ShapeDtypeStruct((t, D_MODEL), F32), jax.ShapeDtypeStruct((t, D_MODEL), BF16)],
        grid=(n_tiles + 1,),
        in_specs=[
            pl.BlockSpec((tm, k), cur),
            w_spec,
            pl.BlockSpec((tm, D_MODEL), cur),
            pl.BlockSpec((1, D_MODEL), lambda s: (0, 0)),
        ],
        out_specs=[pl.BlockSpec((tm, D_MODEL), cur), pl.BlockSpec((tm, D_MODEL), prev)],
        scratch_shapes=[pltpu.VMEM((2, tm, D_MODEL), F32), pltpu.VMEM((tm, HEAD_DIM), F32)],
        compiler_params=pltpu.CompilerParams(dimension_semantics=("arbitrary",), vmem_limit_bytes=56 * MIB),
        name=name,
    )(a, w_all, x, g)


def _ffn_up_body(h_ref, wg_ref, wu_ref, o_ref):
    h = h_ref[...]
    gt = jnp.dot(h, wg_ref[...], preferred_element_type=F32)
    up = jnp.dot(h, wu_ref[...], preferred_element_type=F32)
    o_ref[...] = (gt * (1.0 / (1.0 + jnp.exp(-gt))) * up).astype(o_ref.dtype)


def _ffn_up(h, w_all, layer):
    t = h.shape[0]
    n_tiles = D_FF // TN
    return pl.pallas_call(
        _ffn_up_body,
        out_shape=jax.ShapeDtypeStruct((t, D_FF), BF16),
        grid=(t // TM_FFN, n_tiles),
        in_specs=[
            pl.BlockSpec((TM_FFN, D_MODEL), lambda i, j: (i, 0)),
            pl.BlockSpec((None, D_MODEL, TN), lambda i, j: (layer, 0, j)),
            pl.BlockSpec((None, D_MODEL, TN), lambda i, j: (layer, 0, n_tiles + j)),
        ],
        out_specs=pl.BlockSpec((TM_FFN, TN), lambda i, j: (i, j)),
        compiler_params=_params(2, 52),
        name="ffn_up",
    )(h, w_all, w_all)


def _rope_tables(n):
    half = HEAD_DIM // 2
    inv_freq = ROPE_THETA ** (-jnp.arange(half, dtype=F32) * 2.0 / HEAD_DIM)
    ang = jnp.arange(n, dtype=F32)[:, None] * inv_freq[None, :]
    cos, sin = jnp.cos(ang), jnp.sin(ang)
    return jnp.concatenate([cos, cos], axis=-1), jnp.concatenate([-sin, sin], axis=-1)


def kernel(x, norm1_g, w_in, qk_norm_g, sink_a, rpb_c, w_br_a, w_br_b, w_br_c, w_o,
           norm2_g, w_gate_up, w_down):
    bsz, seq, d = x.shape
    assert d == D_MODEL and seq % TM_ROPE == 0 and seq // ATT_Q >= 2 and (bsz * seq) % TM_FFN == 0
    assert seq // GRID_W >= C_WIN_ROWS and (seq // B_PATTERNS[-1][1]) % ATT_Q == 0
    t = bsz * seq
    cos, sin_signed = _rope_tables(seq)
    bias_a = jnp.asarray(_band_bias(A_RADIUS))
    bias_b = jnp.asarray(_band_bias(B_RADIUS))
    w_in, w_br_a, w_br_b, w_br_c, w_o, w_gate_up, w_down = (
        w.astype(BF16) for w in (w_in, w_br_a, w_br_b, w_br_c, w_o, w_gate_up, w_down))
    qkg = jnp.pad(qk_norm_g, ((0, 0), (0, 2), (0, 0)))
    tile = lambda off, tn=TN: off // tn
    wide = lambda width: TN_WIDE if width % TN_WIDE == 0 else TN
    xf = x.reshape(t, d)
    h = _norm1(xf, norm1_g[0][None])
    for layer in range(DEPTH):
        proj = functools.partial(_proj, h, w_in, layer, cos=cos, sin_signed=sin_signed, qkg=qkg[layer],
                                 bsz=bsz, seq=seq)
        qkv_a = proj(range(tile(OFF_QA), tile(OFF_QB)), dil=1, name="proj_a", tm=TM_ROPE)
        oa = _attn_a(qkv_a.reshape(bsz, seq, -1), sink_a[layer], bias_a)
        parts = []
        for group, (_, dil) in enumerate(B_PATTERNS):
            tiles = [tile(off) + group for off in (OFF_QB, OFF_KB, OFF_VB)]
            qkv_b = proj(tiles, dil=dil, name=f"proj_b{group}", tm=TM_ROPE)
            if dil == 1:
                qkv_b = qkv_b.reshape(bsz, 1, seq, -1)
            parts.append(_attn_b_group(qkv_b, bias_b, group))
        ob = _b_combine(parts, seq)
        tn_c = wide(C_W)
        qkv_c = proj(range(tile(OFF_QC, tn_c), tile(OFF_GATE, tn_c)), dil=1, name="proj_c", tn=tn_c)
        oc = _attn_c(qkv_c.reshape(bsz, seq, -1), _c_bias(rpb_c[layer].reshape(-1)))
        merged = _merge(h, w_in, oa.reshape(t, A_Q_W), ob.reshape(t, B_OUT_W), oc.reshape(t, C_W),
                        w_br_a, w_br_b, w_br_c, layer)
        xf, h2 = _residual_rows(merged, w_o, layer, xf, norm2_g[layer][None], TM_OUT, "out_proj")
        act = _ffn_up(h2, w_gate_up, layer)
        next_g = norm1_g[layer + 1][None] if layer + 1 < DEPTH else None
        xf, h = _residual_rows(act, w_down, layer, xf, next_g, TM_DOWN, "ffn_down")
    return xf.reshape(bsz, seq, d)
```

```python
import functools

import numpy as np
import jax
import jax.numpy as jnp
from jax import lax
from jax.experimental import pallas as pl
from jax.experimental.pallas import tpu as pltpu

F32 = jnp.float32
BF16 = jnp.bfloat16

D_MODEL = 2048
DEPTH = 4
HEAD_DIM = 128
SUBLANES = 8
ROPE_THETA = 10000.0
NORM_EPS = 1e-6
SCALE = HEAD_DIM ** -0.5
LOG2E = 1.4426950408889634
NEG = -1e30

A_Q_HEADS = 8
A_KV_HEADS = 2
A_GROUP = A_Q_HEADS // A_KV_HEADS
A_RADIUS = 128
B_PATTERNS = ((128, 1), (512, 4), (2048, 16))
B_GROUPS = len(B_PATTERNS)
B_HEADS_PER_GROUP = 4
B_RADIUS = 64
C_HEADS = 8
GRID_W = 64
C_WIN_ROWS = 8
C_WIN_COLS = 16

A_Q_W = A_Q_HEADS * HEAD_DIM
A_KV_W = A_KV_HEADS * HEAD_DIM
B_W = B_GROUPS * B_HEADS_PER_GROUP * HEAD_DIM
B_OUT_W = B_HEADS_PER_GROUP * HEAD_DIM
C_W = C_HEADS * HEAD_DIM
N_IN = A_Q_W + 2 * A_KV_W + 3 * B_W + 3 * C_W + 3 * D_MODEL
D_FF = ((8 * D_MODEL + 3 * 256 - 1) // (3 * 256)) * 256

OFF_QA = 0
OFF_KA = OFF_QA + A_Q_W
OFF_VA = OFF_KA + A_KV_W
OFF_QB = OFF_VA + A_KV_W
OFF_KB = OFF_QB + B_W
OFF_VB = OFF_KB + B_W
OFF_QC = OFF_VB + B_W
OFF_KC = OFF_QC + C_W
OFF_VC = OFF_KC + C_W
OFF_GATE = OFF_VC + C_W

TM = 1024
TN = 512
TN_WIDE = 1024
TM_OUT = 512
TM_DOWN = 256
TM_FFN = 2048
TM_ROPE = 1024
EPI_CHUNK = 64
NORM_CHUNK = 64
ATT_Q = 256
B_CLASS_BLOCK = 8
C_TILE_ROWS = ATT_Q // GRID_W
COMBINE_CHUNK = 128
MIB = 1024 * 1024


def _params(n_axes, vmem_mib):
    return pltpu.CompilerParams(
        dimension_semantics=("parallel",) * n_axes,
        vmem_limit_bytes=vmem_mib * MIB,
    )


def _rmsnorm_rows(x_ref, g_ref, h_ref):
    rows = x_ref.shape[0]

    def body(c, carry):
        r0 = pl.multiple_of(c * NORM_CHUNK, NORM_CHUNK)
        x = x_ref[pl.ds(r0, NORM_CHUNK), :]
        ms = jnp.mean(x * x, axis=-1, keepdims=True)
        h_ref[pl.ds(r0, NORM_CHUNK), :] = (x * lax.rsqrt(ms + NORM_EPS) * g_ref[...]).astype(h_ref.dtype)
        return carry

    lax.fori_loop(0, rows // NORM_CHUNK, body, 0)


def _pipelined_attention(n_units, scores, values, finish, extra_logit=None):
    s_next = scores(0)
    prev = None
    for u in range(n_units + 1):
        s = s_next
        if u + 1 < n_units:
            s_next = scores(u + 1)
        cur = None
        if u < n_units:
            m = jnp.max(s, axis=-1, keepdims=True)
            if extra_logit is not None:
                m = jnp.maximum(m, extra_logit(u))
            p = jnp.exp2(s - m)
            l = jnp.sum(p, axis=-1, keepdims=True)
            if extra_logit is not None:
                l = l + jnp.exp2(extra_logit(u) - m)
            cur = (p.astype(BF16), m, l)
        if prev is not None:
            p_prev, m_prev, l_prev = prev
            o = jnp.dot(p_prev, values(u - 1), preferred_element_type=F32) * (1.0 / l_prev)
            finish(u - 1, o, m_prev, l_prev)
        prev = cur


def _qk(q, k):
    return lax.dot_general(q, k, (((1,), (1,)), ((), ())), preferred_element_type=F32)


def _norm1_body(x_ref, g_ref, o_ref):
    _rmsnorm_rows(x_ref, g_ref, o_ref)


def _norm1(x, g):
    t = x.shape[0]
    return pl.pallas_call(
        _norm1_body,
        out_shape=jax.ShapeDtypeStruct((t, D_MODEL), BF16),
        grid=(t // TM,),
        in_specs=[pl.BlockSpec((TM, D_MODEL), lambda i: (i, 0)),
                  pl.BlockSpec((1, D_MODEL), lambda i: (0, 0))],
        out_specs=pl.BlockSpec((TM, D_MODEL), lambda i: (i, 0)),
        compiler_params=_params(1, 40),
        name="norm1",
    )(x, g)


def _head_kind(head):
    c = head * HEAD_DIM
    if c < OFF_KA:
        return ("norm_rope", 0, True)
    if c < OFF_VA:
        return ("norm_rope", 1, False)
    if c < OFF_QB:
        return ("copy",)
    if c < OFF_KB:
        return ("norm_rope", 2, True)
    if c < OFF_VB:
        return ("norm_rope", 3, False)
    if c < OFF_QC:
        return ("copy",)
    if c < OFF_KC:
        return ("norm", 4, True)
    if c < OFF_VC:
        return ("norm", 5, False)
    assert c < OFF_GATE
    return ("copy",)


def _tile_groups(w_tiles, heads):
    groups = {}
    for j, wt in enumerate(w_tiles):
        kinds = tuple(_head_kind(wt * heads + h) for h in range(heads))
        groups.setdefault(kinds, []).append(j)
    return list(groups.items())


def _in_tiles(j, tiles):
    runs = []
    for t in tiles:
        if runs and runs[-1][1] == t:
            runs[-1][1] = t + 1
        else:
            runs.append([t, t + 1])
    cond = None
    for lo, hi in runs:
        c = (j == lo) if hi == lo + 1 else ((j >= lo) & (j < hi))
        cond = c if cond is None else (cond | c)
    return cond


def _proj_epilogue(kinds, acc_ref, rot_ref, scl_ref, o_ref, cos_ref, sin_ref, qkg_ref, dil, pitch, tm):
    rows_per_class = tm // dil
    blocks = []
    for c in range(dil):
        for m0 in range(0, rows_per_class, EPI_CHUNK):
            if dil == 1:
                blocks.append((pl.ds(m0, EPI_CHUNK), pl.ds(m0, EPI_CHUNK), c, m0))
            else:
                natural = pl.ds(c + dil * m0, EPI_CHUNK, stride=dil)
                blocks.append((natural if pitch is None else pl.ds(c * pitch + m0, EPI_CHUNK), natural, c, m0))
    half = HEAD_DIM // 2

    def pass1(h, kind):
        if kind[0] not in ("norm_rope", "norm"):
            return
        for src, _, c, m0 in blocks:
            tmp = pl.ds(c * rows_per_class + m0, EPI_CHUNK)
            a = acc_ref.at[h][src, :]
            ms = jnp.mean(a * a, axis=-1, keepdims=True)
            scl_ref[h, tmp, :] = jnp.broadcast_to(lax.rsqrt(ms + NORM_EPS), a.shape)
            if kind[0] == "norm_rope":
                rot_ref[h, tmp, :] = pltpu.roll(a, half, 1)

    def pass2():
        gains = {}
        for kind in kinds:
            if kind[0] in ("norm_rope", "norm") and kind[1:] not in gains:
                g = qkg_ref[kind[1]:kind[1] + 1, :]
                if kind[2]:
                    g = g * (SCALE * LOG2E)
                gains[kind[1:]] = (g, pltpu.roll(g, half, 1))
        for src, tab, c, m0 in blocks:
            tmp = pl.ds(c * rows_per_class + m0, EPI_CHUNK)
            tables = {}
            for h, kind in enumerate(kinds):
                cols = slice(h * HEAD_DIM, (h + 1) * HEAD_DIM)
                a = acc_ref.at[h][src, :]
                if kind[0] == "norm_rope":
                    if kind[1:] not in tables:
                        g, g_rot = gains[kind[1:]]
                        tables[kind[1:]] = (g * cos_ref[tab, :], g_rot * sin_ref[tab, :])
                    t_cos, t_sin = tables[kind[1:]]
                    y = scl_ref[h, tmp, :] * (a * t_cos + rot_ref[h, tmp, :] * t_sin)
                elif kind[0] == "norm":
                    y = a * scl_ref[h, tmp, :] * gains[kind[1:]][0]
                else:
                    y = a
                if dil == 1:
                    o_ref[pl.ds(m0, EPI_CHUNK), cols] = y.astype(o_ref.dtype)
                else:
                    o_ref[c, pl.ds(m0, EPI_CHUNK), cols] = y.astype(o_ref.dtype)

    for h, kind in enumerate(kinds):
        pass1(h, kind)
    pass2()


def _proj_body(h_ref, w_ref, cos_ref, sin_ref, qkg_ref, o_ref, acc_ref, *norm_scratch, n_cols, groups, dil, pitch, tm):
    rot_ref, scl_ref = norm_scratch if norm_scratch else (None, None)
    s = pl.program_id(0)
    je = jnp.maximum(s - 1, 0) % n_cols
    parity = s % 2

    @pl.when(s == 0)
    def _():
        acc_ref[1] = jnp.zeros(acc_ref.shape[1:], F32)

    def step(kinds, p):
        _proj_epilogue(kinds, acc_ref.at[1 - p], rot_ref, scl_ref, o_ref, cos_ref, sin_ref, qkg_ref, dil, pitch, tm)
        res = jnp.dot(h_ref[...], w_ref[...], preferred_element_type=F32)
        for h in range(len(kinds)):
            cols = slice(h * HEAD_DIM, (h + 1) * HEAD_DIM)
            if pitch is None:
                acc_ref[p, h] = res[:, cols]
                continue
            for r0 in range(0, tm, SUBLANES):
                dst = pl.ds((r0 % dil) * pitch + r0 // dil, SUBLANES, stride=pitch)
                acc_ref.at[p, h][dst, :] = res[r0:r0 + SUBLANES, cols]

    for kinds, tiles in groups:
        for p in range(2):
            cond = parity == p
            if len(groups) > 1:
                cond = cond & _in_tiles(je, tiles)
            pl.when(cond)(functools.partial(step, kinds, p))


def _proj(h, w_all, layer, w_tiles, cos, sin_signed, qkg, bsz, seq, dil, name, tn=TN, tm=TM):
    t = h.shape[0]
    heads = tn // HEAD_DIM
    n_cols = len(w_tiles)
    w_base = w_tiles[0]
    w_step = w_tiles[1] - w_tiles[0] if n_cols > 1 else 0
    assert list(w_tiles) == [w_base + w_step * j for j in range(n_cols)]
    pos_tiles = seq // tm
    n_tiles = (t // tm) * n_cols
    groups = _tile_groups(w_tiles, heads)
    needs_norm = any(k[0] in ("norm_rope", "norm") for kinds, _ in groups for k in kinds)
    head_buf = pltpu.VMEM((heads, tm, HEAD_DIM), F32)
    pitch = tm // dil + SUBLANES if dil % (2 * SUBLANES) == 0 else None
    acc_rows = tm if pitch is None else dil * pitch

    dot_tile = lambda s: jnp.minimum(s, n_tiles - 1)
    epi_tile = lambda s: jnp.maximum(s - 1, 0)
    if dil == 1:
        out_shape = jax.ShapeDtypeStruct((t, n_cols * tn), BF16)
        out_spec = pl.BlockSpec((tm, tn), lambda s: (epi_tile(s) // n_cols, epi_tile(s) % n_cols))
    else:
        out_shape = jax.ShapeDtypeStruct((bsz, dil, seq // dil, n_cols * tn), BF16)
        out_spec = pl.BlockSpec(
            (None, dil, tm // dil, tn),
            lambda s: ((epi_tile(s) // n_cols) // pos_tiles, 0, (epi_tile(s) // n_cols) % pos_tiles,
                       epi_tile(s) % n_cols))
    table_spec = pl.BlockSpec((tm, HEAD_DIM), lambda s: ((epi_tile(s) // n_cols) % pos_tiles, 0))
    return pl.pallas_call(
        functools.partial(_proj_body, n_cols=n_cols, groups=groups, dil=dil, pitch=pitch, tm=tm),
        out_shape=out_shape,
        grid=(n_tiles + 1,),
        in_specs=[
            pl.BlockSpec((tm, D_MODEL), lambda s: (dot_tile(s) // n_cols, 0)),
            pl.BlockSpec((None, D_MODEL, tn), lambda s: (layer, 0, w_base + w_step * (dot_tile(s) % n_cols))),
            table_spec, table_spec,
            pl.BlockSpec((8, HEAD_DIM), lambda s: (0, 0)),
        ],
        out_specs=out_spec,
        scratch_shapes=[pltpu.VMEM((2, heads, acc_rows, HEAD_DIM), F32)] + ([head_buf, head_buf] if needs_norm else []),
        compiler_params=pltpu.CompilerParams(dimension_semantics=("arbitrary",), vmem_limit_bytes=56 * MIB),
        name=name,
    )(h, w_all, cos, sin_signed, qkg)


def _band_bias(radius):
    r = np.arange(ATT_Q)[:, None]
    c = np.arange(ATT_Q + 2 * radius)[None, :]
    band = np.abs(c - radius - r) <= radius
    lo_ok = c >= radius
    hi_ok = c < ATT_Q + radius
    variants = [band & lo_ok, band, band & hi_ok, band & lo_ok & hi_ok]
    return np.stack([np.where(v, 0.0, NEG) for v in variants]).astype(np.float32)


def _band_variant(i, n_tiles):
    if n_tiles == 1:
        return 3
    return jnp.where(i == 0, 0, jnp.where(i == n_tiles - 1, 2, 1))


def _halo_specs(block, index, radius, n_rows, q_rows=ATT_Q):
    per = q_rows // radius
    last = n_rows // radius - 1
    return [
        pl.BlockSpec(block(radius), lambda *g: index(g, jnp.maximum(g[-1] * per - 1, 0))),
        pl.BlockSpec(block(q_rows), lambda *g: index(g, g[-1])),
        pl.BlockSpec(block(radius), lambda *g: index(g, jnp.minimum(g[-1] * per + per, last))),
    ]


def _attn_a_body(sink_ref, q_ref, kp_ref, kc_ref, kn_ref, vp_ref, vc_ref, vn_ref, bias_ref, o_ref):
    head_cols = lambda h: slice(h * HEAD_DIM, (h + 1) * HEAD_DIM)
    kcat = [jnp.concatenate([r[:, head_cols(kv)] for r in (kp_ref, kc_ref, kn_ref)], axis=0)
            for kv in range(A_KV_HEADS)]
    vcat = [jnp.concatenate([r[:, head_cols(kv)] for r in (vp_ref, vc_ref, vn_ref)], axis=0)
            for kv in range(A_KV_HEADS)]

    def finish(h, o, m, l):
        o_ref[:, head_cols(h)] = o.astype(o_ref.dtype)

    _pipelined_attention(
        A_Q_HEADS,
        scores=lambda h: _qk(q_ref[:, head_cols(h)], kcat[h // A_GROUP]) + bias_ref[...],
        values=lambda h: vcat[h // A_GROUP],
        finish=finish,
        extra_logit=lambda h: sink_ref[h] * LOG2E)


def _attn_a(qkv, sink, bias):
    bsz, seq, _ = qkv.shape
    n_tiles = seq // ATT_Q
    block = lambda r: (None, r, A_KV_W)
    k_specs = _halo_specs(block, lambda g, rb: (g[0], rb, OFF_KA // A_KV_W), A_RADIUS, seq)
    v_specs = _halo_specs(block, lambda g, rb: (g[0], rb, OFF_VA // A_KV_W), A_RADIUS, seq)
    return pl.pallas_call(
        _attn_a_body,
        out_shape=jax.ShapeDtypeStruct((bsz, seq, A_Q_W), BF16),
        grid=(bsz, n_tiles),
        in_specs=[
            pl.BlockSpec(memory_space=pltpu.SMEM),
            pl.BlockSpec((None, ATT_Q, A_Q_W), lambda b, i: (b, i, 0)),
            *k_specs, *v_specs,
            pl.BlockSpec((None, ATT_Q, ATT_Q + 2 * A_RADIUS), lambda b, i: (_band_variant(i, n_tiles), 0, 0)),
        ],
        out_specs=pl.BlockSpec((None, ATT_Q, A_Q_W), lambda b, i: (b, i, 0)),
        compiler_params=_params(2, 32),
        name="attn_a",
    )(sink, qkv, qkv, qkv, qkv, qkv, qkv, qkv, bias)


def _attn_b_body(q_ref, kp_ref, kc_ref, kn_ref, vp_ref, vc_ref, vn_ref, *rest):
    *bias_refs, o_ref, lse_ref = rest
    n_classes, n_qt = q_ref.shape[0], len(bias_refs)
    head_cols = lambda u: slice((u % B_HEADS_PER_GROUP) * HEAD_DIM, (u % B_HEADS_PER_GROUP + 1) * HEAD_DIM)
    tile = lambda u: (u // B_HEADS_PER_GROUP) % n_qt
    cls = lambda u: u // (B_HEADS_PER_GROUP * n_qt)
    q_rows = lambda u: slice(tile(u) * ATT_Q, (tile(u) + 1) * ATT_Q)

    def keys(refs, u):
        full = jnp.concatenate([r[cls(u), :, head_cols(u)] for r in refs], axis=0)
        return full[tile(u) * ATT_Q:(tile(u) + 1) * ATT_Q + 2 * B_RADIUS, :]

    def finish(u, o, m, l):
        o_ref[cls(u), q_rows(u), head_cols(u)] = o.astype(o_ref.dtype)
        lse_ref[cls(u), q_rows(u), head_cols(u)] = jnp.broadcast_to(m + jnp.log2(l), o.shape)

    _pipelined_attention(
        n_classes * n_qt * B_HEADS_PER_GROUP,
        scores=lambda u: (_qk(q_ref[cls(u), q_rows(u), head_cols(u)], keys((kp_ref, kc_ref, kn_ref), u))
                          + bias_refs[tile(u)][...]),
        values=lambda u: keys((vp_ref, vc_ref, vn_ref), u),
        finish=finish)


def _attn_b_group(qkv, bias, group):
    bsz, dil, m, _ = qkv.shape
    n_tiles = m // ATT_Q
    cb = min(dil, B_CLASS_BLOCK)
    qt = B_CLASS_BLOCK // cb
    q_rows = qt * ATT_Q
    block = lambda r: (None, cb, r, B_OUT_W)
    k_specs = _halo_specs(block, lambda g, rb: (g[0], g[1], rb, 1), B_RADIUS, m, q_rows)
    v_specs = _halo_specs(block, lambda g, rb: (g[0], g[1], rb, 2), B_RADIUS, m, q_rows)
    io_spec = pl.BlockSpec(block(q_rows), lambda b, c, i: (b, c, i, 0))
    bias_specs = [pl.BlockSpec((None, ATT_Q, ATT_Q + 2 * B_RADIUS),
                               lambda b, c, i, t=t: (_band_variant(i * qt + t, n_tiles), 0, 0)) for t in range(qt)]
    return pl.pallas_call(
        _attn_b_body,
        out_shape=[jax.ShapeDtypeStruct((bsz, dil, m, B_OUT_W), BF16),
                   jax.ShapeDtypeStruct((bsz, dil, m, B_OUT_W), F32)],
        grid=(bsz, dil // cb, n_tiles // qt),
        in_specs=[io_spec, *k_specs, *v_specs, *bias_specs],
        out_specs=[io_spec, io_spec],
        compiler_params=_params(3, 48),
        name=f"attn_b{group}",
    )(qkv, qkv, qkv, qkv, qkv, qkv, qkv, *([bias] * qt))


def _b_combine_body(*refs):
    in_refs = refs[:2 * B_GROUPS]
    out_ref, nat_o, nat_l = refs[2 * B_GROUPS:]
    o0_ref, l0_ref = in_refs[0], in_refs[1]
    for gi in range(1, B_GROUPS):
        o_ref, l_ref = in_refs[2 * gi], in_refs[2 * gi + 1]
        dil = B_PATTERNS[gi][1]
        for c in range(dil):
            dst = pl.ds(c, TM // dil, stride=dil)
            for h in range(B_HEADS_PER_GROUP):
                cols = slice(h * HEAD_DIM, (h + 1) * HEAD_DIM)
                nat_o.at[gi - 1, h][dst, :] = o_ref[c, :, cols].astype(F32)
                nat_l.at[gi - 1, h][dst, :] = l_ref[c, :, cols]

    def body(ci, carry):
        rows = pl.ds(pl.multiple_of(ci * COMBINE_CHUNK, COMBINE_CHUNK), COMBINE_CHUNK)
        for h in range(B_HEADS_PER_GROUP):
            cols = slice(h * HEAD_DIM, (h + 1) * HEAD_DIM)
            lses = [l0_ref[0, rows, cols]] + [nat_l[gi, h, rows, :] for gi in range(B_GROUPS - 1)]
            outs = [o0_ref[0, rows, cols].astype(F32)] + [nat_o[gi, h, rows, :] for gi in range(B_GROUPS - 1)]
            mx = functools.reduce(jnp.maximum, lses)
            ws = [jnp.exp2(l - mx) for l in lses]
            num = functools.reduce(lambda a, b: a + b, [w * o for w, o in zip(ws, outs)])
            den = functools.reduce(lambda a, b: a + b, ws)
            out_ref[rows, cols] = (num * (1.0 / den)).astype(out_ref.dtype)
        return carry

    lax.fori_loop(0, TM // COMBINE_CHUNK, body, 0)


def _b_combine(parts, seq):
    assert B_PATTERNS[0][1] == 1
    bsz = parts[0][0].shape[0]
    in_specs, args = [], []
    for (o, lse), (_, dil) in zip(parts, B_PATTERNS):
        spec = pl.BlockSpec((None, dil, TM // dil, B_OUT_W), lambda b, i: (b, 0, i, 0))
        in_specs += [spec, spec]
        args += [o, lse]
    nat_shape = (B_GROUPS - 1, B_HEADS_PER_GROUP, TM, HEAD_DIM)
    return pl.pallas_call(
        _b_combine_body,
        out_shape=jax.ShapeDtypeStruct((bsz, seq, B_OUT_W), BF16),
        grid=(bsz, seq // TM),
        in_specs=in_specs,
        out_specs=pl.BlockSpec((None, TM, B_OUT_W), lambda b, i: (b, i, 0)),
        scratch_shapes=[pltpu.VMEM(nat_shape, F32), pltpu.VMEM(nat_shape, F32)],
        compiler_params=_params(2, 40),
        name="b_combine",
    )(*args)


N_DROW = 2 * C_WIN_ROWS - 1
N_DCOL = 2 * C_WIN_COLS - 1
C_KEY_TILES = 3 * C_TILE_ROWS


def _c_bias_body(rpb_ref, o_ref):
    h = pl.program_id(0)
    qc = lax.broadcasted_iota(jnp.int32, (GRID_W, GRID_W), 0)
    kc = lax.broadcasted_iota(jnp.int32, (GRID_W, GRID_W), 1)
    dcol = jnp.clip(kc - qc, -(C_WIN_COLS - 1), C_WIN_COLS - 1) + (C_WIN_COLS - 1)
    col_start = jnp.clip(qc - C_WIN_COLS // 2, 0, GRID_W - C_WIN_COLS)
    col_ok = (kc >= col_start) & (kc < col_start + C_WIN_COLS)
    tiles = []
    for dr in range(N_DROW):
        t = jnp.zeros((GRID_W, GRID_W), F32)
        for d in range(N_DCOL):
            t = jnp.where(dcol == d, rpb_ref[(h * N_DROW + dr) * N_DCOL + d] * LOG2E, t)
        tiles.append(jnp.where(col_ok, t, NEG))
    masked = jnp.full((GRID_W, GRID_W), NEG, F32)
    half = C_WIN_ROWS // 2
    for var in range(3):
        for qi in range(C_TILE_ROWS):
            row = []
            for kj in range(C_KEY_TILES):
                if var == 0:
                    ok = C_TILE_ROWS <= kj < C_TILE_ROWS + C_WIN_ROWS
                elif var == 2:
                    ok = 2 * C_TILE_ROWS - C_WIN_ROWS <= kj < 2 * C_TILE_ROWS
                else:
                    ok = 0 <= kj - C_TILE_ROWS - qi + half < C_WIN_ROWS
                drow = kj - C_TILE_ROWS - qi
                row.append(tiles[drow + C_WIN_ROWS - 1] if ok else masked)
            o_ref[var, qi * GRID_W:(qi + 1) * GRID_W, :] = jnp.concatenate(row, axis=1)


def _c_bias(rpb_flat):
    return pl.pallas_call(
        _c_bias_body,
        out_shape=jax.ShapeDtypeStruct((3, C_HEADS, ATT_Q, C_KEY_TILES * GRID_W), F32),
        grid=(C_HEADS,),
        in_specs=[pl.BlockSpec(memory_space=pltpu.SMEM)],
        out_specs=pl.BlockSpec((3, None, ATT_Q, C_KEY_TILES * GRID_W), lambda h: (0, h, 0, 0)),
        compiler_params=_params(1, 32),
        name="c_bias",
    )(rpb_flat)


def _attn_c_body(q_ref, kp_ref, kc_ref, kn_ref, vp_ref, vc_ref, vn_ref, bias_ref, o_ref):
    head_cols = lambda h: slice(h * HEAD_DIM, (h + 1) * HEAD_DIM)
    cat = lambda refs, h: jnp.concatenate([r[:, head_cols(h)] for r in refs], axis=0)

    def finish(h, o, m, l):
        o_ref[:, head_cols(h)] = o.astype(o_ref.dtype)

    _pipelined_attention(
        C_HEADS,
        scores=lambda h: _qk(q_ref[:, head_cols(h)], cat((kp_ref, kc_ref, kn_ref), h)) + bias_ref[h],
        values=lambda h: cat((vp_ref, vc_ref, vn_ref), h),
        finish=finish)


def _attn_c(qkv, bias):
    bsz, seq, _ = qkv.shape
    n_tiles = seq // ATT_Q
    block = lambda r: (None, r, C_W)
    k_specs = _halo_specs(block, lambda g, rb: (g[0], rb, 1), ATT_Q, seq)
    v_specs = _halo_specs(block, lambda g, rb: (g[0], rb, 2), ATT_Q, seq)
    return pl.pallas_call(
        _attn_c_body,
        out_shape=jax.ShapeDtypeStruct((bsz, seq, C_W), BF16),
        grid=(bsz, n_tiles),
        in_specs=[
            pl.BlockSpec(block(ATT_Q), lambda b, i: (b, i, 0)),
            *k_specs, *v_specs,
            pl.BlockSpec((None, C_HEADS, ATT_Q, C_KEY_TILES * GRID_W),
                         lambda b, i: (jnp.where(i == 0, 0, jnp.where(i == n_tiles - 1, 2, 1)), 0, 0, 0)),
        ],
        out_specs=pl.BlockSpec(block(ATT_Q), lambda b, i: (b, i, 0)),
        compiler_params=_params(2, 40),
        name="attn_c",
    )(qkv, qkv, qkv, qkv, qkv, qkv, qkv, bias)


def _merge_body(h_ref, wga_ref, wgb_ref, wgc_ref, oa_ref, ob_ref, oc_ref, wa_ref, wb_ref, wc_ref, o_ref):
    h = h_ref[...]
    m = None
    for wg_ref, o_in_ref, w_ref in ((wga_ref, oa_ref, wa_ref), (wgb_ref, ob_ref, wb_ref), (wgc_ref, oc_ref, wc_ref)):
        logit = jnp.dot(h, wg_ref[...], preferred_element_type=F32)
        gate = 1.0 / (1.0 + jnp.exp(-logit))
        term = gate * jnp.dot(o_in_ref[...], w_ref[...], preferred_element_type=F32)
        m = term if m is None else m + term
    o_ref[...] = m.astype(o_ref.dtype)


def _merge(h, w_in, oa, ob, oc, wa, wb, wc, layer):
    t = oa.shape[0]
    gate_w = lambda k: pl.BlockSpec((None, D_MODEL, TN), lambda i, j: (layer, 0, (OFF_GATE + k * D_MODEL) // TN + j))
    wspec = lambda rows: pl.BlockSpec((None, rows, TN), lambda i, j: (layer, 0, j))
    return pl.pallas_call(
        _merge_body,
        out_shape=jax.ShapeDtypeStruct((t, D_MODEL), BF16),
        grid=(t // TM, D_MODEL // TN),
        in_specs=[
            pl.BlockSpec((TM, D_MODEL), lambda i, j: (i, 0)),
            gate_w(0), gate_w(1), gate_w(2),
            pl.BlockSpec((TM, A_Q_W), lambda i, j: (i, 0)),
            pl.BlockSpec((TM, B_OUT_W), lambda i, j: (i, 0)),
            pl.BlockSpec((TM, C_W), lambda i, j: (i, 0)),
            wspec(A_Q_W), wspec(B_OUT_W), wspec(C_W),
        ],
        out_specs=pl.BlockSpec((TM, TN), lambda i, j: (i, j)),
        compiler_params=_params(2, 56),
        name="merge",
    )(h, w_in, w_in, w_in, oa, ob, oc, wa, wb, wc)


def _residual_rows_body(a_ref, w_ref, x_ref, g_ref, xo_ref, h_ref, y_ref, rs_ref, *, tm):
    s = pl.program_id(0)
    parity = s % 2

    @pl.when(s == 0)
    def _():
        y_ref[1] = jnp.zeros(y_ref.shape[1:], F32)

    def step(p):
        prev = y_ref.at[1 - p]
        chunks = [pl.ds(r0, NORM_CHUNK) for r0 in range(0, tm, NORM_CHUNK)]
        for rows in chunks:
            y = prev[rows, :]
            ms = jnp.mean(y * y, axis=-1, keepdims=True)
            rs_ref[rows, :] = jnp.broadcast_to(lax.rsqrt(ms + NORM_EPS), (NORM_CHUNK, HEAD_DIM))
        for rows in chunks:
            rs = pltpu.repeat(rs_ref[rows, :], D_MODEL // HEAD_DIM, axis=1)
            h_ref[rows, :] = (prev[rows, :] * rs * g_ref[...]).astype(h_ref.dtype)
        y = x_ref[...] + jnp.dot(a_ref[...], w_ref[...], preferred_element_type=F32)
        xo_ref[...] = y
        y_ref[p] = y

    for p in range(2):
        pl.when(parity == p)(functools.partial(step, p))


def _residual_rows_plain_body(a_ref, w_ref, x_ref, xo_ref):
    xo_ref[...] = x_ref[...] + jnp.dot(a_ref[...], w_ref[...], preferred_element_type=F32)


def _residual_rows(a, w_all, layer, x, g, tm, name):
    t, k = a.shape
    n_tiles = t // tm
    w_spec = pl.BlockSpec((None, k, D_MODEL), lambda s: (layer, 0, 0), pipeline_mode=pl.Buffered(1))
    if g is None:
        row = lambda s: (s, 0)
        return pl.pallas_call(
            _residual_rows_plain_body,
            out_shape=jax.ShapeDtypeStruct((t, D_MODEL), F32),
            grid=(n_tiles,),
            in_specs=[pl.BlockSpec((tm, k), row), w_spec, pl.BlockSpec((tm, D_MODEL), row)],
            out_specs=pl.BlockSpec((tm, D_MODEL), row),
            compiler_params=_params(1, 56),
            name=name,
        )(a, w_all, x), None
    cur = lambda s: (jnp.minimum(s, n_tiles - 1), 0)
    prev = lambda s: (jnp.maximum(s - 1, 0), 0)
    return pl.pallas_call(
        functools.partial(_residual_rows_body, tm=tm),
        out_shape=[jax.ShapeDtypeStruct((t, D_MODEL), F32), jax.ShapeDtypeStruct((t, D_MODEL), BF16)],
        grid=(n_tiles + 1,),
        in_specs=[
            pl.BlockSpec((tm, k), cur),
            w_spec,
            pl.BlockSpec((tm, D_MODEL), cur),
            pl.BlockSpec((1, D_MODEL), lambda s: (0, 0)),
        ],
        out_specs=[pl.BlockSpec((tm, D_MODEL), cur), pl.BlockSpec((tm, D_MODEL), prev)],
        scratch_shapes=[pltpu.VMEM((2, tm, D_MODEL), F32), pltpu.VMEM((tm, HEAD_DIM), F32)],
        compiler_params=pltpu.CompilerParams(dimension_semantics=("arbitrary",), vmem_limit_bytes=56 * MIB),
        name=name,
    )(a, w_all, x, g)


def _ffn_up_body(h_ref, wg_ref, wu_ref, o_ref):
    h = h_ref[...]
    gt = jnp.dot(h, wg_ref[...], preferred_element_type=F32)
    up = jnp.dot(h, wu_ref[...], preferred_element_type=F32)
    o_ref[...] = (gt * (1.0 / (1.0 + jnp.exp(-gt))) * up).astype(o_ref.dtype)


def _ffn_up(h, w_all, layer):
    t = h.shape[0]
    n_tiles = D_FF // TN
    return pl.pallas_call(
        _ffn_up_body,
        out_shape=jax.ShapeDtypeStruct((t, D_FF), BF16),
        grid=(t // TM_FFN, n_tiles),
        in_specs=[
            pl.BlockSpec((TM_FFN, D_MODEL), lambda i, j: (i, 0)),
            pl.BlockSpec((None, D_MODEL, TN), lambda i, j: (layer, 0, j)),
            pl.BlockSpec((None, D_MODEL, TN), lambda i, j: (layer, 0, n_tiles + j)),
        ],
        out_specs=pl.BlockSpec((TM_FFN, TN), lambda i, j: (i, j)),
        compiler_params=_params(2, 52),
        name="ffn_up",
    )(h, w_all, w_all)


def _rope_tables(n):
    half = HEAD_DIM // 2
    inv_freq = ROPE_THETA ** (-jnp.arange(half, dtype=F32) * 2.0 / HEAD_DIM)
    ang = jnp.arange(n, dtype=F32)[:, None] * inv_freq[None, :]
    cos, sin = jnp.cos(ang), jnp.sin(ang)
    return jnp.concatenate([cos, cos], axis=-1), jnp.concatenate([-sin, sin], axis=-1)


def kernel(x, norm1_g, w_in, qk_norm_g, sink_a, rpb_c, w_br_a, w_br_b, w_br_c, w_o,
           norm2_g, w_gate_up, w_down):
    bsz, seq, d = x.shape
    assert d == D_MODEL and seq % TM_ROPE == 0 and seq // ATT_Q >= 2 and (bsz * seq) % TM_FFN == 0
    assert seq // GRID_W >= C_WIN_ROWS and (seq // B_PATTERNS[-1][1]) % ATT_Q == 0
    t = bsz * seq
    cos, sin_signed = _rope_tables(seq)
    bias_a = jnp.asarray(_band_bias(A_RADIUS))
    bias_b = jnp.asarray(_band_bias(B_RADIUS))
    w_in, w_br_a, w_br_b, w_br_c, w_o, w_gate_up, w_down = (
        w.astype(BF16) for w in (w_in, w_br_a, w_br_b, w_br_c, w_o, w_gate_up, w_down))
    qkg = jnp.pad(qk_norm_g, ((0, 0), (0, 2), (0, 0)))
    tile = lambda off, tn=TN: off // tn
    wide = lambda width: TN_WIDE if width % TN_WIDE == 0 else TN
    xf = x.reshape(t, d)
    h = _norm1(xf, norm1_g[0][None])
    for layer in range(DEPTH):
        proj = functools.partial(_proj, h, w_in, layer, cos=cos, sin_signed=sin_signed, qkg=qkg[layer],
                                 bsz=bsz, seq=seq)
        qkv_a = proj(range(tile(OFF_QA), tile(OFF_QB)), dil=1, name="proj_a", tm=TM_ROPE)
        oa = _attn_a(qkv_a.reshape(bsz, seq, -1), sink_a[layer], bias_a)
        parts = []
        for group, (_, dil) in enumerate(B_PATTERNS):
            tiles = [tile(off) + group for off in (OFF_QB, OFF_KB, OFF_VB)]
            qkv_b = proj(tiles, dil=dil, name=f"proj_b{group}", tm=TM_ROPE)
            if dil == 1:
                qkv_b = qkv_b.reshape(bsz, 1, seq, -1)
            parts.append(_attn_b_group(qkv_b, bias_b, group))
        ob = _b_combine(parts, seq)
        tn_c = wide(C_W)
        qkv_c = proj(range(tile(OFF_QC, tn_c), tile(OFF_GATE, tn_c)), dil=1, name="proj_c", tn=tn_c)
        oc = _attn_c(qkv_c.reshape(bsz, seq, -1), _c_bias(rpb_c[layer].reshape(-1)))
        merged = _merge(h, w_in, oa.reshape(t, A_Q_W), ob.reshape(t, B_OUT_W), oc.reshape(t, C_W),
                        w_br_a, w_br_b, w_br_c, layer)
        xf, h2 = _residual_rows(merged, w_o, layer, xf, norm2_g[layer][None], TM_OUT, "out_proj")
        act = _ffn_up(h2, w_gate_up, layer)
        next_g = norm1_g[layer + 1][None] if layer + 1 < DEPTH else None
        xf, h = _residual_rows(act, w_down, layer, xf, next_g, TM_DOWN, "ffn_down")
    return xf.reshape(bsz, seq, d)
```

```python
import functools

import numpy as np
import jax
import jax.numpy as jnp
from jax import lax
from jax.experimental import pallas as pl
from jax.experimental.pallas import tpu as pltpu

F32 = jnp.float32
BF16 = jnp.bfloat16

D_MODEL = 2048
DEPTH = 4
HEAD_DIM = 128
SUBLANES = 8
ROPE_THETA = 10000.0
NORM_EPS = 1e-6
SCALE = HEAD_DIM ** -0.5
LOG2E = 1.4426950408889634
NEG = -1e30

A_Q_HEADS = 8
A_KV_HEADS = 2
A_GROUP = A_Q_HEADS // A_KV_HEADS
A_RADIUS = 128
B_PATTERNS = ((128, 1), (512, 4), (2048, 16))
B_GROUPS = len(B_PATTERNS)
B_HEADS_PER_GROUP = 4
B_RADIUS = 64
C_HEADS = 8
GRID_W = 64
C_WIN_ROWS = 8
C_WIN_COLS = 16

A_Q_W = A_Q_HEADS * HEAD_DIM
A_KV_W = A_KV_HEADS * HEAD_DIM
B_W = B_GROUPS * B_HEADS_PER_GROUP * HEAD_DIM
B_OUT_W = B_HEADS_PER_GROUP * HEAD_DIM
C_W = C_HEADS * HEAD_DIM
N_IN = A_Q_W + 2 * A_KV_W + 3 * B_W + 3 * C_W + 3 * D_MODEL
D_FF = ((8 * D_MODEL + 3 * 256 - 1) // (3 * 256)) * 256

OFF_QA = 0
OFF_KA = OFF_QA + A_Q_W
OFF_VA = OFF_KA + A_KV_W
OFF_QB = OFF_VA + A_KV_W
OFF_KB = OFF_QB + B_W
OFF_VB = OFF_KB + B_W
OFF_QC = OFF_VB + B_W
OFF_KC = OFF_QC + C_W
OFF_VC = OFF_KC + C_W
OFF_GATE = OFF_VC + C_W

TM = 1024
TN = 512
TN_WIDE = 1024
TM_OUT = 512
TM_DOWN = 256
TM_FFN = 2048
TM_ROPE = 1024
EPI_CHUNK = 64
NORM_CHUNK = 64
ATT_Q = 256
B_CLASS_BLOCK = 8
C_TILE_ROWS = ATT_Q // GRID_W
COMBINE_CHUNK = 128
MIB = 1024 * 1024


def _params(n_axes, vmem_mib):
    return pltpu.CompilerParams(
        dimension_semantics=("parallel",) * n_axes,
        vmem_limit_bytes=vmem_mib * MIB,
    )


def _rmsnorm_rows(x_ref, g_ref, h_ref):
    rows = x_ref.shape[0]

    def body(c, carry):
        r0 = pl.multiple_of(c * NORM_CHUNK, NORM_CHUNK)
        x = x_ref[pl.ds(r0, NORM_CHUNK), :]
        ms = jnp.mean(x * x, axis=-1, keepdims=True)
        h_ref[pl.ds(r0, NORM_CHUNK), :] = (x * lax.rsqrt(ms + NORM_EPS) * g_ref[...]).astype(h_ref.dtype)
        return carry

    lax.fori_loop(0, rows // NORM_CHUNK, body, 0)


def _pipelined_attention(n_units, scores, values, finish, extra_logit=None):
    s_next = scores(0)
    prev = None
    for u in range(n_units + 1):
        s = s_next
        if u + 1 < n_units:
            s_next = scores(u + 1)
        cur = None
        if u < n_units:
            m = jnp.max(s, axis=-1, keepdims=True)
            if extra_logit is not None:
                m = jnp.maximum(m, extra_logit(u))
            p = jnp.exp2(s - m)
            l = jnp.sum(p, axis=-1, keepdims=True)
            if extra_logit is not None:
                l = l + jnp.exp2(extra_logit(u) - m)
            cur = (p.astype(BF16), m, l)
        if prev is not None:
            p_prev, m_prev, l_prev = prev
            o = jnp.dot(p_prev, values(u - 1), preferred_element_type=F32) * (1.0 / l_prev)
            finish(u - 1, o, m_prev, l_prev)
        prev = cur


def _qk(q, k):
    return lax.dot_general(q, k, (((1,), (1,)), ((), ())), preferred_element_type=F32)


def _norm1_body(x_ref, g_ref, o_ref):
    _rmsnorm_rows(x_ref, g_ref, o_ref)


def _norm1(x, g):
    t = x.shape[0]
    return pl.pallas_call(
        _norm1_body,
        out_shape=jax.ShapeDtypeStruct((t, D_MODEL), BF16),
        grid=(t // TM,),
        in_specs=[pl.BlockSpec((TM, D_MODEL), lambda i: (i, 0)),
                  pl.BlockSpec((1, D_MODEL), lambda i: (0, 0))],
        out_specs=pl.BlockSpec((TM, D_MODEL), lambda i: (i, 0)),
        compiler_params=_params(1, 40),
        name="norm1",
    )(x, g)


def _head_kind(head):
    c = head * HEAD_DIM
    if c < OFF_KA:
        return ("norm_rope", 0, True)
    if c < OFF_VA:
        return ("norm_rope", 1, False)
    if c < OFF_QB:
        return ("copy",)
    if c < OFF_KB:
        return ("norm_rope", 2, True)
    if c < OFF_VB:
        return ("norm_rope", 3, False)
    if c < OFF_QC:
        return ("copy",)
    if c < OFF_KC:
        return ("norm", 4, True)
    if c < OFF_VC:
        return ("norm", 5, False)
    assert c < OFF_GATE
    return ("copy",)


def _tile_groups(w_tiles, heads):
    groups = {}
    for j, wt in enumerate(w_tiles):
        kinds = tuple(_head_kind(wt * heads + h) for h in range(heads))
        groups.setdefault(kinds, []).append(j)
    return list(groups.items())


def _in_tiles(j, tiles):
    runs = []
    for t in tiles:
        if runs and runs[-1][1] == t:
            runs[-1][1] = t + 1
        else:
            runs.append([t, t + 1])
    cond = None
    for lo, hi in runs:
        c = (j == lo) if hi == lo + 1 else ((j >= lo) & (j < hi))
        cond = c if cond is None else (cond | c)
    return cond


def _proj_epilogue(kinds, acc_ref, rot_ref, scl_ref, o_ref, cos_ref, sin_ref, qkg_ref, dil, pitch, tm):
    rows_per_class = tm // dil
    blocks = []
    for c in range(dil):
        for m0 in range(0, rows_per_class, EPI_CHUNK):
            if dil == 1:
                blocks.append((pl.ds(m0, EPI_CHUNK), pl.ds(m0, EPI_CHUNK), c, m0))
            else:
                natural = pl.ds(c + dil * m0, EPI_CHUNK, stride=dil)
                blocks.append((natural if pitch is None else pl.ds(c * pitch + m0, EPI_CHUNK), natural, c, m0))
    half = HEAD_DIM // 2

    def pass1(h, kind):
        if kind[0] not in ("norm_rope", "norm"):
            return
        for src, _, c, m0 in blocks:
            tmp = pl.ds(c * rows_per_class + m0, EPI_CHUNK)
            a = acc_ref.at[h][src, :]
            ms = jnp.mean(a * a, axis=-1, keepdims=True)
            scl_ref[h, tmp, :] = jnp.broadcast_to(lax.rsqrt(ms + NORM_EPS), a.shape)
            if kind[0] == "norm_rope":
                rot_ref[h, tmp, :] = pltpu.roll(a, half, 1)

    def pass2():
        gains = {}
        for kind in kinds:
            if kind[0] in ("norm_rope", "norm") and kind[1:] not in gains:
                g = qkg_ref[kind[1]:kind[1] + 1, :]
                if kind[2]:
                    g = g * (SCALE * LOG2E)
                gains[kind[1:]] = (g, pltpu.roll(g, half, 1))
        for src, tab, c, m0 in blocks:
            tmp = pl.ds(c * rows_per_class + m0, EPI_CHUNK)
            tables = {}
            for h, kind in enumerate(kinds):
                cols = slice(h * HEAD_DIM, (h + 1) * HEAD_DIM)
                a = acc_ref.at[h][src, :]
                if kind[0] == "norm_rope":
                    if kind[1:] not in tables:
                        g, g_rot = gains[kind[1:]]
                        tables[kind[1:]] = (g * cos_ref[tab, :], g_rot * sin_ref[tab, :])
                    t_cos, t_sin = tables[kind[1:]]
                    y = scl_ref[h, tmp, :] * (a * t_cos + rot_ref[h, tmp, :] * t_sin)
                elif kind[0] == "norm":
                    y = a * scl_ref[h, tmp, :] * gains[kind[1:]][0]
                else:
                    y = a
                if dil == 1:
                    o_ref[pl.ds(m0, EPI_CHUNK), cols] = y.astype(o_ref.dtype)
                else:
                    o_ref[c, pl.ds(m0, EPI_CHUNK), cols] = y.astype(o_ref.dtype)

    for h, kind in enumerate(kinds):
        pass1(h, kind)
    pass2()


def _proj_body(h_ref, w_ref, cos_ref, sin_ref, qkg_ref, o_ref, acc_ref, *norm_scratch, n_cols, groups, dil, pitch, tm):
    rot_ref, scl_ref = norm_scratch if norm_scratch else (None, None)
    s = pl.program_id(0)
    je = jnp.maximum(s - 1, 0) % n_cols
    parity = s % 2

    @pl.when(s == 0)
    def _():
        acc_ref[1] = jnp.zeros(acc_ref.shape[1:], F32)

    def step(kinds, p):
        _proj_epilogue(kinds, acc_ref.at[1 - p], rot_ref, scl_ref, o_ref, cos_ref, sin_ref, qkg_ref, dil, pitch, tm)
        res = jnp.dot(h_ref[...], w_ref[...], preferred_element_type=F32)
        for h in range(len(kinds)):
            cols = slice(h * HEAD_DIM, (h + 1) * HEAD_DIM)
            if pitch is None:
                acc_ref[p, h] = res[:, cols]
                continue
            for r0 in range(0, tm, SUBLANES):
                dst = pl.ds((r0 % dil) * pitch + r0 // dil, SUBLANES, stride=pitch)
                acc_ref.at[p, h][dst, :] = res[r0:r0 + SUBLANES, cols]

    for kinds, tiles in groups:
        for p in range(2):
            cond = parity == p
            if len(groups) > 1:
                cond = cond & _in_tiles(je, tiles)
            pl.when(cond)(functools.partial(step, kinds, p))


def _proj(h, w_all, layer, w_tiles, cos, sin_signed, qkg, bsz, seq, dil, name, tn=TN, tm=TM):
    t = h.shape[0]
    heads = tn // HEAD_DIM
    n_cols = len(w_tiles)
    w_base = w_tiles[0]
    w_step = w_tiles[1] - w_tiles[0] if n_cols > 1 else 0
    assert list(w_tiles) == [w_base + w_step * j for j in range(n_cols)]
    pos_tiles = seq // tm
    n_tiles = (t // tm) * n_cols
    groups = _tile_groups(w_tiles, heads)
    needs_norm = any(k[0] in ("norm_rope", "norm") for kinds, _ in groups for k in kinds)
    head_buf = pltpu.VMEM((heads, tm, HEAD_DIM), F32)
    pitch = tm // dil + SUBLANES if dil % (2 * SUBLANES) == 0 else None
    acc_rows = tm if pitch is None else dil * pitch

    dot_tile = lambda s: jnp.minimum(s, n_tiles - 1)
    epi_tile = lambda s: jnp.maximum(s - 1, 0)
    if dil == 1:
        out_shape = jax.ShapeDtypeStruct((t, n_cols * tn), BF16)
        out_spec = pl.BlockSpec((tm, tn), lambda s: (epi_tile(s) // n_cols, epi_tile(s) % n_cols))
    else:
        out_shape = jax.ShapeDtypeStruct((bsz, dil, seq // dil, n_cols * tn), BF16)
        out_spec = pl.BlockSpec(
            (None, dil, tm // dil, tn),
            lambda s: ((epi_tile(s) // n_cols) // pos_tiles, 0, (epi_tile(s) // n_cols) % pos_tiles,
                       epi_tile(s) % n_cols))
    table_spec = pl.BlockSpec((tm, HEAD_DIM), lambda s: ((epi_tile(s) // n_cols) % pos_tiles, 0))
    return pl.pallas_call(
        functools.partial(_proj_body, n_cols=n_cols, groups=groups, dil=dil, pitch=pitch, tm=tm),
        out_shape=out_shape,
        grid=(n_tiles + 1,),
        in_specs=[
            pl.BlockSpec((tm, D_MODEL), lambda s: (dot_tile(s) // n_cols, 0)),
            pl.BlockSpec((None, D_MODEL, tn), lambda s: (layer, 0, w_base + w_step * (dot_tile(s) % n_cols))),
            table_spec, table_spec,
            pl.BlockSpec((8, HEAD_DIM), lambda s: (0, 0)),
        ],
        out_specs=out_spec,
        scratch_shapes=[pltpu.VMEM((2, heads, acc_rows, HEAD_DIM), F32)] + ([head_buf, head_buf] if needs_norm else []),
        compiler_params=pltpu.CompilerParams(dimension_semantics=("arbitrary",), vmem_limit_bytes=56 * MIB),
        name=name,
    )(h, w_all, cos, sin_signed, qkg)


def _band_bias(radius):
    r = np.arange(ATT_Q)[:, None]
    c = np.arange(ATT_Q + 2 * radius)[None, :]
    band = np.abs(c - radius - r) <= radius
    lo_ok = c >= radius
    hi_ok = c < ATT_Q + radius
    variants = [band & lo_ok, band, band & hi_ok, band & lo_ok & hi_ok]
    return np.stack([np.where(v, 0.0, NEG) for v in variants]).astype(np.float32)


def _band_variant(i, n_tiles):
    if n_tiles == 1:
        return 3
    return jnp.where(i == 0, 0, jnp.where(i == n_tiles - 1, 2, 1))


def _halo_specs(block, index, radius, n_rows, q_rows=ATT_Q):
    per = q_rows // radius
    last = n_rows // radius - 1
    return [
        pl.BlockSpec(block(radius), lambda *g: index(g, jnp.maximum(g[-1] * per - 1, 0))),
        pl.BlockSpec(block(q_rows), lambda *g: index(g, g[-1])),
        pl.BlockSpec(block(radius), lambda *g: index(g, jnp.minimum(g[-1] * per + per, last))),
    ]


def _attn_a_body(sink_ref, q_ref, kp_ref, kc_ref, kn_ref, vp_ref, vc_ref, vn_ref, bias_ref, o_ref):
    head_cols = lambda h: slice(h * HEAD_DIM, (h + 1) * HEAD_DIM)
    kcat = [jnp.concatenate([r[:, head_cols(kv)] for r in (kp_ref, kc_ref, kn_ref)], axis=0)
            for kv in range(A_KV_HEADS)]
    vcat = [jnp.concatenate([r[:, head_cols(kv)] for r in (vp_ref, vc_ref, vn_ref)], axis=0)
            for kv in range(A_KV_HEADS)]

    def finish(h, o, m, l):
        o_ref[:, head_cols(h)] = o.astype(o_ref.dtype)

    _pipelined_attention(
        A_Q_HEADS,
        scores=lambda h: _qk(q_ref[:, head_cols(h)], kcat[h // A_GROUP]) + bias_ref[...],
        values=lambda h: vcat[h // A_GROUP],
        finish=finish,
        extra_logit=lambda h: sink_ref[h] * LOG2E)


def _attn_a(qkv, sink, bias):
    bsz, seq, _ = qkv.shape
    n_tiles = seq // ATT_Q
    block = lambda r: (None, r, A_KV_W)
    k_specs = _halo_specs(block, lambda g, rb: (g[0], rb, OFF_KA // A_KV_W), A_RADIUS, seq)
    v_specs = _halo_specs(block, lambda g, rb: (g[0], rb, OFF_VA // A_KV_W), A_RADIUS, seq)
    return pl.pallas_call(
        _attn_a_body,
        out_shape=jax.ShapeDtypeStruct((bsz, seq, A_Q_W), BF16),
        grid=(bsz, n_tiles),
        in_specs=[
            pl.BlockSpec(memory_space=pltpu.SMEM),
            pl.BlockSpec((None, ATT_Q, A_Q_W), lambda b, i: (b, i, 0)),
            *k_specs, *v_specs,
            pl.BlockSpec((None, ATT_Q, ATT_Q + 2 * A_RADIUS), lambda b, i: (_band_variant(i, n_tiles), 0, 0)),
        ],
        out_specs=pl.BlockSpec((None, ATT_Q, A_Q_W), lambda b, i: (b, i, 0)),
        compiler_params=_params(2, 32),
        name="attn_a",
    )(sink, qkv, qkv, qkv, qkv, qkv, qkv, qkv, bias)


def _attn_b_body(q_ref, kp_ref, kc_ref, kn_ref, vp_ref, vc_ref, vn_ref, *rest):
    *bias_refs, o_ref, lse_ref = rest
    n_classes, n_qt = q_ref.shape[0], len(bias_refs)
    head_cols = lambda u: slice((u % B_HEADS_PER_GROUP) * HEAD_DIM, (u % B_HEADS_PER_GROUP + 1) * HEAD_DIM)
    tile = lambda u: (u // B_HEADS_PER_GROUP) % n_qt
    cls = lambda u: u // (B_HEADS_PER_GROUP * n_qt)
    q_rows = lambda u: slice(tile(u) * ATT_Q, (tile(u) + 1) * ATT_Q)

    def keys(refs, u):
        full = jnp.concatenate([r[cls(u), :, head_cols(u)] for r in refs], axis=0)
        return full[tile(u) * ATT_Q:(tile(u) + 1) * ATT_Q + 2 * B_RADIUS, :]

    def finish(u, o, m, l):
        o_ref[cls(u), q_rows(u), head_cols(u)] = o.astype(o_ref.dtype)
        lse_ref[cls(u), q_rows(u), head_cols(u)] = jnp.broadcast_to(m + jnp.log2(l), o.shape)

    _pipelined_attention(
        n_classes * n_qt * B_HEADS_PER_GROUP,
        scores=lambda u: (_qk(q_ref[cls(u), q_rows(u), head_cols(u)], keys((kp_ref, kc_ref, kn_ref), u))
                          + bias_refs[tile(u)][...]),
        values=lambda u: keys((vp_ref, vc_ref, vn_ref), u),
        finish=finish)


def _attn_b_group(qkv, bias, group):
    bsz, dil, m, _ = qkv.shape
    n_tiles = m // ATT_Q
    cb = min(dil, B_CLASS_BLOCK)
    qt = B_CLASS_BLOCK // cb
    q_rows = qt * ATT_Q
    block = lambda r: (None, cb, r, B_OUT_W)
    k_specs = _halo_specs(block, lambda g, rb: (g[0], g[1], rb, 1), B_RADIUS, m, q_rows)
    v_specs = _halo_specs(block, lambda g, rb: (g[0], g[1], rb, 2), B_RADIUS, m, q_rows)
    io_spec = pl.BlockSpec(block(q_rows), lambda b, c, i: (b, c, i, 0))
    bias_specs = [pl.BlockSpec((None, ATT_Q, ATT_Q + 2 * B_RADIUS),
                               lambda b, c, i, t=t: (_band_variant(i * qt + t, n_tiles), 0, 0)) for t in range(qt)]
    return pl.pallas_call(
        _attn_b_body,
        out_shape=[jax.ShapeDtypeStruct((bsz, dil, m, B_OUT_W), BF16),
                   jax.ShapeDtypeStruct((bsz, dil, m, B_OUT_W), F32)],
        grid=(bsz, dil // cb, n_tiles // qt),
        in_specs=[io_spec, *k_specs, *v_specs, *bias_specs],
        out_specs=[io_spec, io_spec],
        compiler_params=_params(3, 48),
        name=f"attn_b{group}",
    )(qkv, qkv, qkv, qkv, qkv, qkv, qkv, *([bias] * qt))


def _b_combine_body(*refs):
    in_refs = refs[:2 * B_GROUPS]
    out_ref, nat_o, nat_l = refs[2 * B_GROUPS:]
    o0_ref, l0_ref = in_refs[0], in_refs[1]
    for gi in range(1, B_GROUPS):
        o_ref, l_ref = in_refs[2 * gi], in_refs[2 * gi + 1]
        dil = B_PATTERNS[gi][1]
        for c in range(dil):
            dst = pl.ds(c, TM // dil, stride=dil)
            for h in range(B_HEADS_PER_GROUP):
                cols = slice(h * HEAD_DIM, (h + 1) * HEAD_DIM)
                nat_o.at[gi - 1, h][dst, :] = o_ref[c, :, cols].astype(F32)
                nat_l.at[gi - 1, h][dst, :] = l_ref[c, :, cols]

    def body(ci, carry):
        rows = pl.ds(pl.multiple_of(ci * COMBINE_CHUNK, COMBINE_CHUNK), COMBINE_CHUNK)
        for h in range(B_HEADS_PER_GROUP):
            cols = slice(h * HEAD_DIM, (h + 1) * HEAD_DIM)
            lses = [l0_ref[0, rows, cols]] + [nat_l[gi, h, rows, :] for gi in range(B_GROUPS - 1)]
            outs = [o0_ref[0, rows, cols].astype(F32)] + [nat_o[gi, h, rows, :] for gi in range(B_GROUPS - 1)]
            mx = functools.reduce(jnp.maximum, lses)
            ws = [jnp.exp2(l - mx) for l in lses]
            num = functools.reduce(lambda a, b: a + b, [w * o for w, o in zip(ws, outs)])
            den = functools.reduce(lambda a, b: a + b, ws)
            out_ref[rows, cols] = (num * (1.0 / den)).astype(out_ref.dtype)
        return carry

    lax.fori_loop(0, TM // COMBINE_CHUNK, body, 0)


def _b_combine(parts, seq):
    assert B_PATTERNS[0][1] == 1
    bsz = parts[0][0].shape[0]
    in_specs, args = [], []
    for (o, lse), (_, dil) in zip(parts, B_PATTERNS):
        spec = pl.BlockSpec((None, dil, TM // dil, B_OUT_W), lambda b, i: (b, 0, i, 0))
        in_specs += [spec, spec]
        args += [o, lse]
    nat_shape = (B_GROUPS - 1, B_HEADS_PER_GROUP, TM, HEAD_DIM)
    return pl.pallas_call(
        _b_combine_body,
        out_shape=jax.ShapeDtypeStruct((bsz, seq, B_OUT_W), BF16),
        grid=(bsz, seq // TM),
        in_specs=in_specs,
        out_specs=pl.BlockSpec((None, TM, B_OUT_W), lambda b, i: (b, i, 0)),
        scratch_shapes=[pltpu.VMEM(nat_shape, F32), pltpu.VMEM(nat_shape, F32)],
        compiler_params=_params(2, 40),
        name="b_combine",
    )(*args)


N_DROW = 2 * C_WIN_ROWS - 1
N_DCOL = 2 * C_WIN_COLS - 1
C_KEY_TILES = 3 * C_TILE_ROWS


def _c_bias_body(rpb_ref, o_ref):
    h = pl.program_id(0)
    qc = lax.broadcasted_iota(jnp.int32, (GRID_W, GRID_W), 0)
    kc = lax.broadcasted_iota(jnp.int32, (GRID_W, GRID_W), 1)
    dcol = jnp.clip(kc - qc, -(C_WIN_COLS - 1), C_WIN_COLS - 1) + (C_WIN_COLS - 1)
    col_start = jnp.clip(qc - C_WIN_COLS // 2, 0, GRID_W - C_WIN_COLS)
    col_ok = (kc >= col_start) & (kc < col_start + C_WIN_COLS)
    tiles = []
    for dr in range(N_DROW):
        t = jnp.zeros((GRID_W, GRID_W), F32)
        for d in range(N_DCOL):
            t = jnp.where(dcol == d, rpb_ref[(h * N_DROW + dr) * N_DCOL + d] * LOG2E, t)
        tiles.append(jnp.where(col_ok, t, NEG))
    masked = jnp.full((GRID_W, GRID_W), NEG, F32)
    half = C_WIN_ROWS // 2
    for var in range(3):
        for qi in range(C_TILE_ROWS):
            row = []
            for kj in range(C_KEY_TILES):
                if var == 0:
                    ok = C_TILE_ROWS <= kj < C_TILE_ROWS + C_WIN_ROWS
                elif var == 2:
                    ok = 2 * C_TILE_ROWS - C_WIN_ROWS <= kj < 2 * C_TILE_ROWS
                else:
                    ok = 0 <= kj - C_TILE_ROWS - qi + half < C_WIN_ROWS
                drow = kj - C_TILE_ROWS - qi
                row.append(tiles[drow + C_WIN_ROWS - 1] if ok else masked)
            o_ref[var, qi * GRID_W:(qi + 1) * GRID_W, :] = jnp.concatenate(row, axis=1)


def _c_bias(rpb_flat):
    n_heads = rpb_flat.shape[0] // (N_DROW * N_DCOL)
    return pl.pallas_call(
        _c_bias_body,
        out_shape=jax.ShapeDtypeStruct((3, n_heads, ATT_Q, C_KEY_TILES * GRID_W), F32),
        grid=(n_heads,),
        in_specs=[pl.BlockSpec(memory_space=pltpu.SMEM)],
        out_specs=pl.BlockSpec((3, None, ATT_Q, C_KEY_TILES * GRID_W), lambda h: (0, h, 0, 0)),
        compiler_params=_params(1, 32),
        name="c_bias",
    )(rpb_flat)


def _attn_c_body(q_ref, kp_ref, kc_ref, kn_ref, vp_ref, vc_ref, vn_ref, bias_ref, o_ref):
    head_cols = lambda h: slice(h * HEAD_DIM, (h + 1) * HEAD_DIM)
    cat = lambda refs, h: jnp.concatenate([r[:, head_cols(h)] for r in refs], axis=0)

    def finish(h, o, m, l):
        o_ref[:, head_cols(h)] = o.astype(o_ref.dtype)

    _pipelined_attention(
        C_HEADS,
        scores=lambda h: _qk(q_ref[:, head_cols(h)], cat((kp_ref, kc_ref, kn_ref), h)) + bias_ref[h],
        values=lambda h: cat((vp_ref, vc_ref, vn_ref), h),
        finish=finish)


def _attn_c(qkv, bias, layer):
    bsz, seq, _ = qkv.shape
    n_tiles = seq // ATT_Q
    block = lambda r: (None, r, C_W)
    k_specs = _halo_specs(block, lambda g, rb: (g[0], rb, 1), ATT_Q, seq)
    v_specs = _halo_specs(block, lambda g, rb: (g[0], rb, 2), ATT_Q, seq)
    return pl.pallas_call(
        _attn_c_body,
        out_shape=jax.ShapeDtypeStruct((bsz, seq, C_W), BF16),
        grid=(bsz, n_tiles),
        in_specs=[
            pl.BlockSpec(block(ATT_Q), lambda b, i: (b, i, 0)),
            *k_specs, *v_specs,
            pl.BlockSpec((None, C_HEADS, ATT_Q, C_KEY_TILES * GRID_W),
                         lambda b, i: (jnp.where(i == 0, 0, jnp.where(i == n_tiles - 1, 2, 1)), layer, 0, 0)),
        ],
        out_specs=pl.BlockSpec(block(ATT_Q), lambda b, i: (b, i, 0)),
        compiler_params=_params(2, 40),
        name="attn_c",
    )(qkv, qkv, qkv, qkv, qkv, qkv, qkv, bias)


def _merge_body(h_ref, wga_ref, wgb_ref, wgc_ref, oa_ref, ob_ref, oc_ref, wa_ref, wb_ref, wc_ref, o_ref):
    h = h_ref[...]
    m = None
    for wg_ref, o_in_ref, w_ref in ((wga_ref, oa_ref, wa_ref), (wgb_ref, ob_ref, wb_ref), (wgc_ref, oc_ref, wc_ref)):
        logit = jnp.dot(h, wg_ref[...], preferred_element_type=F32)
        gate = 1.0 / (1.0 + jnp.exp(-logit))
        term = gate * jnp.dot(o_in_ref[...], w_ref[...], preferred_element_type=F32)
        m = term if m is None else m + term
    o_ref[...] = m.astype(o_ref.dtype)


def _merge(h, w_in, oa, ob, oc, wa, wb, wc, layer):
    t = oa.shape[0]
    gate_w = lambda k: pl.BlockSpec((None, D_MODEL, TN), lambda i, j: (layer, 0, (OFF_GATE + k * D_MODEL) // TN + j))
    wspec = lambda rows: pl.BlockSpec((None, rows, TN), lambda i, j: (layer, 0, j))
    return pl.pallas_call(
        _merge_body,
        out_shape=jax.ShapeDtypeStruct((t, D_MODEL), BF16),
        grid=(t // TM, D_MODEL // TN),
        in_specs=[
            pl.BlockSpec((TM, D_MODEL), lambda i, j: (i, 0)),
            gate_w(0), gate_w(1), gate_w(2),
            pl.BlockSpec((TM, A_Q_W), lambda i, j: (i, 0)),
            pl.BlockSpec((TM, B_OUT_W), lambda i, j: (i, 0)),
            pl.BlockSpec((TM, C_W), lambda i, j: (i, 0)),
            wspec(A_Q_W), wspec(B_OUT_W), wspec(C_W),
        ],
        out_specs=pl.BlockSpec((TM, TN), lambda i, j: (i, j)),
        compiler_params=_params(2, 56),
        name="merge",
    )(h, w_in, w_in, w_in, oa, ob, oc, wa, wb, wc)


def _residual_rows_body(a_ref, w_ref, x_ref, g_ref, xo_ref, h_ref, y_ref, rs_ref, *, tm):
    s = pl.program_id(0)
    parity = s % 2

    @pl.when(s == 0)
    def _():
        y_ref[1] = jnp.zeros(y_ref.shape[1:], F32)

    def step(p):
        prev = y_ref.at[1 - p]
        chunks = [pl.ds(r0, NORM_CHUNK) for r0 in range(0, tm, NORM_CHUNK)]
        for rows in chunks:
            y = prev[rows, :]
            ms = jnp.mean(y * y, axis=-1, keepdims=True)
            rs_ref[rows, :] = jnp.broadcast_to(lax.rsqrt(ms + NORM_EPS), (NORM_CHUNK, HEAD_DIM))
        for rows in chunks:
            rs = pltpu.repeat(rs_ref[rows, :], D_MODEL // HEAD_DIM, axis=1)
            h_ref[rows, :] = (prev[rows, :] * rs * g_ref[...]).astype(h_ref.dtype)
        y = x_ref[...] + jnp.dot(a_ref[...], w_ref[...], preferred_element_type=F32)
        xo_ref[...] = y
        y_ref[p] = y

    for p in range(2):
        pl.when(parity == p)(functools.partial(step, p))


def _residual_rows_plain_body(a_ref, w_ref, x_ref, xo_ref):
    xo_ref[...] = x_ref[...] + jnp.dot(a_ref[...], w_ref[...], preferred_element_type=F32)


def _residual_rows(a, w_all, layer, x, g, tm, name):
    t, k = a.shape
    n_tiles = t // tm
    w_spec = pl.BlockSpec((None, k, D_MODEL), lambda s: (layer, 0, 0), pipeline_mode=pl.Buffered(1))
    if g is None:
        row = lambda s: (s, 0)
        return pl.pallas_call(
            _residual_rows_plain_body,
            out_shape=jax.ShapeDtypeStruct((t, D_MODEL), F32),
            grid=(n_tiles,),
            in_specs=[pl.BlockSpec((tm, k), row), w_spec, pl.BlockSpec((tm, D_MODEL), row)],
            out_specs=pl.BlockSpec((tm, D_MODEL), row),
            compiler_params=_params(1, 56),
            name=name,
        )(a, w_all, x), None
    cur = lambda s: (jnp.minimum(s, n_tiles - 1), 0)
    prev = lambda s: (jnp.maximum(s - 1, 0), 0)
    return pl.pallas_call(
        functools.partial(_residual_rows_body, tm=tm),
        out_shape=[jax.ShapeDtypeStruct((t, D_MODEL), F32), jax.ShapeDtypeStruct((t, D_MODEL), BF16)],
        grid=(n_tiles + 1,),
        in_specs=[
            pl.BlockSpec((tm, k), cur),
            w_spec,
            pl.BlockSpec((tm, D_MODEL), cur),
            pl.BlockSpec((1, D_MODEL), lambda s: (0, 0)),
        ],
        out_specs=[pl.BlockSpec((tm, D_MODEL), cur), pl.BlockSpec((tm, D_MODEL), prev)],
        scratch_shapes=[pltpu.VMEM((2, tm, D_MODEL), F32), pltpu.VMEM((tm, HEAD_DIM), F32)],
        compiler_params=pltpu.CompilerParams(dimension_semantics=("arbitrary",), vmem_limit_bytes=56 * MIB),
        name=name,
    )(a, w_all, x, g)


def _ffn_up_body(h_ref, wg_ref, wu_ref, o_ref):
    h = h_ref[...]
    gt = jnp.dot(h, wg_ref[...], preferred_element_type=F32)
    up = jnp.dot(h, wu_ref[...], preferred_element_type=F32)
    o_ref[...] = (gt * (1.0 / (1.0 + jnp.exp(-gt))) * up).astype(o_ref.dtype)


def _ffn_up(h, w_all, layer):
    t = h.shape[0]
    n_tiles = D_FF // TN
    return pl.pallas_call(
        _ffn_up_body,
        out_shape=jax.ShapeDtypeStruct((t, D_FF), BF16),
        grid=(t // TM_FFN, n_tiles),
        in_specs=[
            pl.BlockSpec((TM_FFN, D_MODEL), lambda i, j: (i, 0)),
            pl.BlockSpec((None, D_MODEL, TN), lambda i, j: (layer, 0, j)),
            pl.BlockSpec((None, D_MODEL, TN), lambda i, j: (layer, 0, n_tiles + j)),
        ],
        out_specs=pl.BlockSpec((TM_FFN, TN), lambda i, j: (i, j)),
        compiler_params=_params(2, 52),
        name="ffn_up",
    )(h, w_all, w_all)


def _rope_tables(n):
    half = HEAD_DIM // 2
    inv_freq = ROPE_THETA ** (-jnp.arange(half, dtype=F32) * 2.0 / HEAD_DIM)
    ang = jnp.arange(n, dtype=F32)[:, None] * inv_freq[None, :]
    cos, sin = jnp.cos(ang), jnp.sin(ang)
    return jnp.concatenate([cos, cos], axis=-1), jnp.concatenate([-sin, sin], axis=-1)


def kernel(x, norm1_g, w_in, qk_norm_g, sink_a, rpb_c, w_br_a, w_br_b, w_br_c, w_o,
           norm2_g, w_gate_up, w_down):
    bsz, seq, d = x.shape
    assert d == D_MODEL and seq % TM_ROPE == 0 and seq // ATT_Q >= 2 and (bsz * seq) % TM_FFN == 0
    assert seq // GRID_W >= C_WIN_ROWS and (seq // B_PATTERNS[-1][1]) % ATT_Q == 0
    t = bsz * seq
    cos, sin_signed = _rope_tables(seq)
    bias_a = jnp.asarray(_band_bias(A_RADIUS))
    bias_b = jnp.asarray(_band_bias(B_RADIUS))
    bias_c = _c_bias(rpb_c.reshape(-1))
    w_in, w_br_a, w_br_b, w_br_c, w_o, w_gate_up, w_down = (
        w.astype(BF16) for w in (w_in, w_br_a, w_br_b, w_br_c, w_o, w_gate_up, w_down))
    qkg = jnp.pad(qk_norm_g, ((0, 0), (0, 2), (0, 0)))
    tile = lambda off, tn=TN: off // tn
    wide = lambda width: TN_WIDE if width % TN_WIDE == 0 else TN
    xf = x.reshape(t, d)
    h = _norm1(xf, norm1_g[0][None])
    for layer in range(DEPTH):
        proj = functools.partial(_proj, h, w_in, layer, cos=cos, sin_signed=sin_signed, qkg=qkg[layer],
                                 bsz=bsz, seq=seq)
        qkv_a = proj(range(tile(OFF_QA), tile(OFF_QB)), dil=1, name="proj_a", tm=TM_ROPE)
        oa = _attn_a(qkv_a.reshape(bsz, seq, -1), sink_a[layer], bias_a)
        parts = []
        for group, (_, dil) in enumerate(B_PATTERNS):
            tiles = [tile(off) + group for off in (OFF_QB, OFF_KB, OFF_VB)]
            qkv_b = proj(tiles, dil=dil, name=f"proj_b{group}", tm=TM_ROPE)
            if dil == 1:
                qkv_b = qkv_b.reshape(bsz, 1, seq, -1)
            parts.append(_attn_b_group(qkv_b, bias_b, group))
        ob = _b_combine(parts, seq)
        tn_c = wide(C_W)
        qkv_c = proj(range(tile(OFF_QC, tn_c), tile(OFF_GATE, tn_c)), dil=1, name="proj_c", tn=tn_c)
        oc = _attn_c(qkv_c.reshape(bsz, seq, -1), bias_c, layer)
        merged = _merge(h, w_in, oa.reshape(t, A_Q_W), ob.reshape(t, B_OUT_W), oc.reshape(t, C_W),
                        w_br_a, w_br_b, w_br_c, layer)
        xf, h2 = _residual_rows(merged, w_o, layer, xf, norm2_g[layer][None], TM_OUT, "out_proj")
        act = _ffn_up(h2, w_gate_up, layer)
        next_g = norm1_g[layer + 1][None] if layer + 1 < DEPTH else None
        xf, h = _residual_rows(act, w_down, layer, xf, next_g, TM_DOWN, "ffn_down")
    return xf.reshape(bsz, seq, d)
```
